```python
import math
import jax, jax.numpy as jnp
from jax import lax
import numpy as np

D_MODEL = 4096
BATCH = 2
SEQ = 4096
DEPTH = 4
DEC_BATCH = 16
DEC_SEQ = 16
PAST_LEN = 1024

CHUNK = 64
N_EVEN = (DEPTH + 1) // 2
N_ODD = DEPTH // 2
MIX_A = D_MODEL // 2
S5_GROUP = 16
S5_GROUPS = MIX_A // S5_GROUP
S5_STATE = 64
S5_DT_MIN = 1e-3
S5_DT_MAX = 1e-1
SB_WIDTH = D_MODEL // 2
SB_HEAD_DIM = 128
SB_HEADS = SB_WIDTH // SB_HEAD_DIM
SB_BLOCK = 128
GLA_WIDTH = D_MODEL // 2
GLA_HEADS = 4
GLA_DK = GLA_WIDTH // 2 // GLA_HEADS
GLA_DV = GLA_WIDTH // GLA_HEADS
GLA_RANK = 16
GLA_GATE_NORM = 16.0
CA_WIDTH = D_MODEL // 2
CA_HEAD_DIM = 128
CA_HEADS = CA_WIDTH // CA_HEAD_DIM
BAND_PREV = 8
BAND_PAST = BAND_PREV * CHUNK
REL_CLIP = 128
D_FF = ((8 * D_MODEL + 3 * 256 - 1) // (3 * 256)) * 256
ALPHA = (2 * DEPTH) ** 0.25
BETA = (8 * DEPTH) ** -0.25
LN_EPS = 1e-5
RMS_EPS = 1e-6
IN_AB = MIX_A + 3 * SB_WIDTH
IN_CD = 2 * GLA_HEADS * GLA_DK + 2 * GLA_WIDTH + GLA_RANK + 3 * CA_WIDTH
F32 = jnp.float32

kernel_name = 'hybrid_streaming_encoder_step'


def split_cols(t, sizes):
    idx = [int(s) for s in np.cumsum(sizes)[:-1]]
    return jnp.split(t, idx, axis=-1)


def layer_norm(x, w, b):
    xf = x.astype(F32)
    xc = xf - jnp.mean(xf, -1, keepdims=True)
    var = jnp.mean(xc * xc, -1, keepdims=True)
    return (xc * lax.rsqrt(var + LN_EPS) * w.astype(F32) + b.astype(F32)).astype(x.dtype)


def swiglu(h, w_gate, w_up, w_down):
    return (jax.nn.silu(h @ w_gate) * (h @ w_up)) @ w_down


def s5_ssm(u, x0_re, x0_im, a_re, a_im, log_dt, b_re, b_im, c_re, c_im, d_skip, w_glu, b_glu):
    n, l, _ = u.shape
    ug = u.astype(F32).reshape(n, l, S5_GROUPS, S5_GROUP)
    lam = lax.complex(a_re.astype(F32), a_im.astype(F32))
    dt = jnp.exp(log_dt.astype(F32))[:, None]
    a_bar = jnp.exp(lam * dt)
    b_bar = ((a_bar - 1.0) / lam)[..., None] * lax.complex(b_re.astype(F32), b_im.astype(F32))
    bu = lax.complex(jnp.einsum('nlgh,gph->nlgp', ug, b_bar.real),
                     jnp.einsum('nlgh,gph->nlgp', ug, b_bar.imag))
    a_seq = jnp.broadcast_to(a_bar, bu.shape)

    def combine(e1, e2):
        return e1[0] * e2[0], e2[0] * e1[1] + e2[1]

    a_cum, xs = lax.associative_scan(combine, (a_seq, bu), axis=1)
    xs = xs + a_cum * lax.complex(x0_re.astype(F32), x0_im.astype(F32))[:, None]
    y = (jnp.einsum('gkp,nlgp->nlgk', c_re.astype(F32), xs.real)
         - jnp.einsum('gkp,nlgp->nlgk', c_im.astype(F32), xs.imag)
         + d_skip.astype(F32).reshape(S5_GROUPS, S5_GROUP) * ug)
    z = jax.nn.gelu(y.reshape(n, l, MIX_A))
    out = z * jax.nn.sigmoid(z @ w_glu.astype(F32) + b_glu.astype(F32))
    return out, xs[:, -1].real, xs[:, -1].imag


def stick_breaking_attention(q, k, v, q_offset):
    n, lq, h, d = q.shape
    lk = k.shape[1]
    bq = min(SB_BLOCK, lq)
    nb = lq // bq
    qb = jnp.moveaxis(q.reshape(n, nb, bq, h, d), 1, 0)
    kpos = jnp.arange(lk)
    scale = d ** -0.5

    def block(args):
        qi, bi = args
        qpos = q_offset + bi * bq + jnp.arange(bq)
        z = jnp.einsum('nqhd,nkhd->nhqk', qi, k).astype(F32) * scale
        mask = kpos[None, :] < qpos[:, None]
        log_fail = jnp.where(mask, jax.nn.log_sigmoid(-z), 0.0)
        log_w = jax.nn.log_sigmoid(z) + lax.cumsum(log_fail, axis=3, reverse=True) - log_fail
        w = jnp.where(mask, jnp.exp(log_w), 0.0)
        return jnp.einsum('nhqk,nkhd->nqhd', w.astype(v.dtype), v)

    out = lax.map(block, (qb, jnp.arange(nb)))
    return jnp.moveaxis(out, 0, 1).reshape(n, lq, h, d)


def gla_recurrence(q, k, v, logg, s0):
    n, l, h, dk = q.shape
    dv = v.shape[-1]
    c = min(CHUNK, l)
    nc = l // c

    def to_chunks(t):
        return jnp.moveaxis(t.reshape(n, nc, c, *t.shape[2:]), 1, 0)

    tri = jnp.tril(jnp.ones((c, c), bool))[None, :, :, None, None]

    def step(s, inp):
        qc, kc, vc, gc = inp
        b = jnp.cumsum(gc, axis=1)
        decay = jnp.exp(jnp.where(tri, b[:, :, None] - b[:, None, :], -jnp.inf))
        att = jnp.einsum('nthd,nshd,ntshd->nhts', qc, kc, decay)
        o = (jnp.einsum('nhts,nshv->nthv', att, vc)
             + jnp.einsum('nthd,nhdv->nthv', qc * jnp.exp(b), s))
        b_last = b[:, -1]
        s = (jnp.exp(b_last)[..., None] * s
             + jnp.einsum('nshd,nshv->nhdv', kc * jnp.exp(b_last[:, None] - b), vc))
        return s, o

    s, o = lax.scan(step, s0, (to_chunks(q), to_chunks(k), to_chunks(v), to_chunks(logg)))
    return jnp.moveaxis(o, 0, 1).reshape(n, l, h, dv), s


def gla_mixer(q, k, v, g, lr, s0, w_lr, b_lr, norm_w):
    n, l, _ = q.shape

    def heads(t, d):
        return t.astype(F32).reshape(n, l, GLA_HEADS, d)

    qh = heads(q, GLA_DK) * GLA_DK ** -0.5
    kh = heads(k, GLA_DK)
    vh = heads(v, GLA_DV)
    logg = heads(jax.nn.log_sigmoid(lr.astype(F32) @ w_lr.astype(F32) + b_lr.astype(F32)) / GLA_GATE_NORM, GLA_DK)
    o, s = gla_recurrence(qh, kh, vh, logg, s0.astype(F32))
    o = o * lax.rsqrt(jnp.mean(o * o, -1, keepdims=True) + RMS_EPS) * norm_w.astype(F32)
    o = o * jax.nn.silu(heads(g, GLA_DV))
    return o.reshape(n, l, GLA_WIDTH), s


def rel_bias_attention(q, k, v, rel, valid, table):
    bias = jnp.take(table, jnp.clip(rel, -REL_CLIP, REL_CLIP) + REL_CLIP, axis=0)
    bias = jnp.transpose(bias, (2, 0, 1)).astype(F32)
    s = jnp.einsum('...qhd,...khd->...hqk', q, k).astype(F32) * CA_HEAD_DIM ** -0.5 + bias
    s = jnp.where(valid, s, -jnp.inf)
    p = jax.nn.softmax(s, axis=-1)
    return jnp.einsum('...hqk,...khd->...qhd', p.astype(v.dtype), v)


def band_attention(q, k, v, k_cache, v_cache, table):
    n, l, h, d = q.shape
    band = (BAND_PREV + 1) * CHUNK
    if k_cache is None:
        nc = l // CHUNK
        qc = q.reshape(n, nc, CHUNK, h, d)
        pad = jnp.zeros((n, BAND_PREV, CHUNK, h, d), k.dtype)
        kp = jnp.concatenate([pad, k.reshape(n, nc, CHUNK, h, d)], axis=1)
        vp = jnp.concatenate([pad.astype(v.dtype), v.reshape(n, nc, CHUNK, h, d)], axis=1)
        idx = jnp.arange(nc)[:, None] + jnp.arange(BAND_PREV + 1)[None, :]
        kb = kp[:, idx].reshape(n, nc, band, h, d)
        vb = vp[:, idx].reshape(n, nc, band, h, d)
        k_local = jnp.arange(band) - BAND_PAST
        rel = jnp.arange(CHUNK)[:, None] - k_local[None, :]
        k_glob = jnp.arange(nc)[:, None] * CHUNK + k_local[None, :]
        valid = (k_glob >= 0)[:, None, None, :]
        out = rel_bias_attention(qc, kb, vb, rel, valid, table).reshape(n, l, h, d)
        keep = min(BAND_PAST, l)
        return out, k[:, l - keep:], v[:, l - keep:]
    r = k_cache.shape[1]
    kk = jnp.concatenate([k_cache.astype(k.dtype), k], axis=1)
    vv = jnp.concatenate([v_cache.astype(v.dtype), v], axis=1)
    k_local = jnp.arange(r + l) - r
    rel = jnp.arange(l)[:, None] - k_local[None, :]
    valid = jnp.ones((1, 1, r + l), bool)
    out = rel_bias_attention(q, kk, vv, rel, valid, table)
    return out, k, v


def mixer_ab(h, p, i, x0_re, x0_im, k_cache, v_cache):
    n, l, _ = h.shape
    u, q, k, v = split_cols(h @ p['ab_w_in'][i], [MIX_A, SB_WIDTH, SB_WIDTH, SB_WIDTH])
    ya, xr, xi = s5_ssm(u, x0_re, x0_im, p['s5_a_re'][i], p['s5_a_im'][i], p['s5_log_dt'][i],
                        p['s5_b_re'][i], p['s5_b_im'][i], p['s5_c_re'][i], p['s5_c_im'][i],
                        p['s5_d'][i], p['s5_w_glu'][i], p['s5_b_glu'][i])

    def heads(t):
        return t.reshape(n, l, SB_HEADS, SB_HEAD_DIM)

    qh, kh, vh = heads(q), heads(k), heads(v)
    yb = stick_breaking_attention(qh, jnp.concatenate([k_cache.astype(kh.dtype), kh], axis=1),
                                  jnp.concatenate([v_cache.astype(vh.dtype), vh], axis=1),
                                  k_cache.shape[1])
    merged = jnp.concatenate([ya.astype(h.dtype), yb.reshape(n, l, SB_WIDTH).astype(h.dtype)], axis=-1)
    return merged @ p['ab_w_out'][i], xr, xi, kh, vh


def mixer_cd(h, p, i, s0, k_cache, v_cache):
    n, l, _ = h.shape
    q, k, v, g, lr, qa, ka, va = split_cols(
        h @ p['cd_w_in'][i],
        [GLA_HEADS * GLA_DK, GLA_HEADS * GLA_DK, GLA_WIDTH, GLA_WIDTH, GLA_RANK, CA_WIDTH, CA_WIDTH, CA_WIDTH])
    yc, s = gla_mixer(q, k, v, g, lr, s0, p['gla_w_lr'][i], p['gla_b_lr'][i], p['gla_norm_w'][i])

    def heads(t):
        return t.reshape(n, l, CA_HEADS, CA_HEAD_DIM)

    yd, kn, vn = band_attention(heads(qa), heads(ka), heads(va), k_cache, v_cache, p['ca_rel_bias'][i])
    merged = jnp.concatenate([yc.astype(h.dtype), yd.reshape(n, l, CA_WIDTH).astype(h.dtype)], axis=-1)
    return merged @ p['cd_w_out'][i], s, kn, vn


def run_trunk(x, c, s5_re, s5_im, sb_k, sb_v, gla_s, band_k, band_v, p):
    n_s5_re, n_s5_im, n_sb_k, n_sb_v, n_gla, n_bk, n_bv = [], [], [], [], [], [], []
    cmod = jax.nn.silu(c)
    for layer in range(DEPTH):
        i = layer // 2
        mod = (cmod @ p['w_mod'][layer] + p['b_mod'][layer])[:, None, :]
        sh1, sc1, g1, sh2, sc2, g2 = jnp.split(mod, 6, axis=-1)
        h = x * (1.0 + sc1) + sh1
        if layer % 2 == 0:
            f, xr, xi, kn, vn = mixer_ab(h, p, i, s5_re[i], s5_im[i], sb_k[i], sb_v[i])
            n_s5_re.append(xr)
            n_s5_im.append(xi)
            n_sb_k.append(kn)
            n_sb_v.append(vn)
        else:
            f, s, kn, vn = mixer_cd(h, p, i, gla_s[i],
                                    None if band_k is None else band_k[i],
                                    None if band_v is None else band_v[i])
            n_gla.append(s)
            n_bk.append(kn)
            n_bv.append(vn)
        x = layer_norm(ALPHA * x + (1.0 + g1) * f.astype(x.dtype), p['ln1_w'][layer], p['ln1_b'][layer])
        h = x * (1.0 + sc2) + sh2
        ff = swiglu(h, p['ffn_w_gate'][layer], p['ffn_w_up'][layer], p['ffn_w_down'][layer])
        x = layer_norm(ALPHA * x + (1.0 + g2) * ff, p['ln2_w'][layer], p['ln2_b'][layer])
    return (x, jnp.stack(n_s5_re), jnp.stack(n_s5_im), jnp.stack(n_sb_k), jnp.stack(n_sb_v),
            jnp.stack(n_gla), jnp.stack(n_bk), jnp.stack(n_bv))


def setup_inputs(seed: int = 0) -> dict:
    key = jax.random.key(seed)
    ks = jax.random.split(key, 38)

    def nrm(j, shape, scale):
        return jax.random.normal(ks[j], shape, F32) * scale

    band_rows = min(BAND_PAST, PAST_LEN)
    pn = jnp.arange(S5_STATE, dtype=F32)
    return {
        'x_prompt': nrm(0, (BATCH, SEQ, D_MODEL), 1.0),
        'x_sample': nrm(1, (DEC_BATCH, DEC_SEQ, D_MODEL), 1.0),
        'state_s5_re': nrm(2, (N_EVEN, DEC_BATCH, S5_GROUPS, S5_STATE), 0.5),
        'state_s5_im': nrm(3, (N_EVEN, DEC_BATCH, S5_GROUPS, S5_STATE), 0.5),
        'cache_sb_k': nrm(4, (N_EVEN, DEC_BATCH, PAST_LEN, SB_HEADS, SB_HEAD_DIM), 1.0),
        'cache_sb_v': nrm(5, (N_EVEN, DEC_BATCH, PAST_LEN, SB_HEADS, SB_HEAD_DIM), 1.0),
        'state_gla': nrm(6, (N_ODD, DEC_BATCH, GLA_HEADS, GLA_DK, GLA_DV), 1.0),
        'cache_band_k': nrm(7, (N_ODD, DEC_BATCH, band_rows, CA_HEADS, CA_HEAD_DIM), 1.0),
        'cache_band_v': nrm(8, (N_ODD, DEC_BATCH, band_rows, CA_HEADS, CA_HEAD_DIM), 1.0),
        'c_prompt': nrm(9, (BATCH, D_MODEL), 1.0),
        'c_sample': nrm(10, (DEC_BATCH, D_MODEL), 1.0),
        'w_mod': nrm(11, (DEPTH, D_MODEL, 6 * D_MODEL), 0.5 * D_MODEL ** -0.5),
        'b_mod': nrm(12, (DEPTH, 6 * D_MODEL), 0.01),
        'ln1_w': 1.0 + nrm(13, (DEPTH, D_MODEL), 0.02),
        'ln1_b': nrm(14, (DEPTH, D_MODEL), 0.02),
        'ln2_w': 1.0 + nrm(15, (DEPTH, D_MODEL), 0.02),
        'ln2_b': nrm(16, (DEPTH, D_MODEL), 0.02),
        'ab_w_in': nrm(17, (N_EVEN, D_MODEL, IN_AB), D_MODEL ** -0.5),
        'ab_w_out': nrm(18, (N_EVEN, MIX_A + SB_WIDTH, D_MODEL), BETA * (MIX_A + SB_WIDTH) ** -0.5),
        's5_a_re': -0.5 + nrm(19, (N_EVEN, S5_GROUPS, S5_STATE), 0.01),
        's5_a_im': math.pi * pn + nrm(20, (N_EVEN, S5_GROUPS, S5_STATE), 0.01),
        's5_log_dt': jax.random.uniform(ks[21], (N_EVEN, S5_GROUPS), F32,
                                        math.log(S5_DT_MIN), math.log(S5_DT_MAX)),
        's5_b_re': nrm(22, (N_EVEN, S5_GROUPS, S5_STATE, S5_GROUP), (2 * S5_GROUP) ** -0.5),
        's5_b_im': nrm(23, (N_EVEN, S5_GROUPS, S5_STATE, S5_GROUP), (2 * S5_GROUP) ** -0.5),
        's5_c_re': nrm(24, (N_EVEN, S5_GROUPS, S5_GROUP, S5_STATE), S5_STATE ** -0.5),
        's5_c_im': nrm(25, (N_EVEN, S5_GROUPS, S5_GROUP, S5_STATE), S5_STATE ** -0.5),
        's5_d': nrm(26, (N_EVEN, MIX_A), 1.0),
        's5_w_glu': nrm(27, (N_EVEN, MIX_A, MIX_A), MIX_A ** -0.5),
        's5_b_glu': nrm(28, (N_EVEN, MIX_A), 0.01),
        'cd_w_in': nrm(29, (N_ODD, D_MODEL, IN_CD), D_MODEL ** -0.5),
        'cd_w_out': nrm(30, (N_ODD, GLA_WIDTH + CA_WIDTH, D_MODEL), BETA * (GLA_WIDTH + CA_WIDTH) ** -0.5),
        'gla_w_lr': nrm(31, (N_ODD, GLA_RANK, GLA_HEADS * GLA_DK), GLA_RANK ** -0.5),
        'gla_b_lr': nrm(32, (N_ODD, GLA_HEADS * GLA_DK), 0.1),
        'gla_norm_w': 1.0 + nrm(33, (N_ODD, GLA_DV), 0.02),
        'ca_rel_bias': nrm(34, (N_ODD, 2 * REL_CLIP + 1, CA_HEADS), 0.5),
        'ffn_w_gate': nrm(35, (DEPTH, D_MODEL, D_FF), D_MODEL ** -0.5),
        'ffn_w_up': nrm(36, (DEPTH, D_MODEL, D_FF), D_MODEL ** -0.5),
        'ffn_w_down': nrm(37, (DEPTH, D_FF, D_MODEL), BETA * D_FF ** -0.5),
    }


def reference(x_prompt, x_sample, state_s5_re, state_s5_im, cache_sb_k, cache_sb_v, state_gla,
              cache_band_k, cache_band_v, c_prompt, c_sample, w_mod, b_mod, ln1_w, ln1_b, ln2_w, ln2_b,
              ab_w_in, ab_w_out, s5_a_re, s5_a_im, s5_log_dt, s5_b_re, s5_b_im, s5_c_re, s5_c_im, s5_d,
              s5_w_glu, s5_b_glu, cd_w_in, cd_w_out, gla_w_lr, gla_b_lr, gla_norm_w, ca_rel_bias,
              ffn_w_gate, ffn_w_up, ffn_w_down):
    p = dict(w_mod=w_mod, b_mod=b_mod, ln1_w=ln1_w, ln1_b=ln1_b, ln2_w=ln2_w, ln2_b=ln2_b,
             ab_w_in=ab_w_in, ab_w_out=ab_w_out, s5_a_re=s5_a_re, s5_a_im=s5_a_im, s5_log_dt=s5_log_dt,
             s5_b_re=s5_b_re, s5_b_im=s5_b_im, s5_c_re=s5_c_re, s5_c_im=s5_c_im, s5_d=s5_d,
             s5_w_glu=s5_w_glu, s5_b_glu=s5_b_glu, cd_w_in=cd_w_in, cd_w_out=cd_w_out,
             gla_w_lr=gla_w_lr, gla_b_lr=gla_b_lr, gla_norm_w=gla_norm_w, ca_rel_bias=ca_rel_bias,
             ffn_w_gate=ffn_w_gate, ffn_w_up=ffn_w_up, ffn_w_down=ffn_w_down)
    nb = x_prompt.shape[0]
    (y_prompt, p_s5_re, p_s5_im, p_sb_k, p_sb_v, p_gla, p_band_k, p_band_v) = run_trunk(
        x_prompt, c_prompt,
        jnp.zeros((N_EVEN, nb, S5_GROUPS, S5_STATE), F32),
        jnp.zeros((N_EVEN, nb, S5_GROUPS, S5_STATE), F32),
        jnp.zeros((N_EVEN, nb, 0, SB_HEADS, SB_HEAD_DIM), x_prompt.dtype),
        jnp.zeros((N_EVEN, nb, 0, SB_HEADS, SB_HEAD_DIM), x_prompt.dtype),
        jnp.zeros((N_ODD, nb, GLA_HEADS, GLA_DK, GLA_DV), F32),
        None, None, p)
    (y_sample, s_s5_re, s_s5_im, s_sb_k, s_sb_v, s_gla, s_band_k, s_band_v) = run_trunk(
        x_sample, c_sample, state_s5_re, state_s5_im, cache_sb_k, cache_sb_v, state_gla,
        cache_band_k, cache_band_v, p)
    return (y_prompt, y_sample, p_s5_re, p_s5_im, p_sb_k, p_sb_v, p_gla, p_band_k, p_band_v,
            s_s5_re, s_s5_im, s_sb_k, s_sb_v, s_gla, s_band_k, s_band_v)
```

```python
import functools
import math

import numpy as np
import jax
import jax.numpy as jnp
from jax import lax
from jax.experimental import pallas as pl
from jax.experimental.pallas import tpu as pltpu

F32 = jnp.float32
BF16 = jnp.bfloat16

CHUNK = 64
BAND_PREV = 8
BAND_PAST = BAND_PREV * CHUNK
GLA_GATE_NORM = 16.0
LN_EPS = 1e-5
RMS_EPS = 1e-6

LANES = 128
SUBLANES = 8
V7X_VMEM_BYTES = 64 * 1024 * 1024
VMEM_LIMIT = V7X_VMEM_BYTES - 8 * 1024 * 1024

GLA_SUB = 16
S5_LANE_GROUPS = 8
NEG = -1e30

_NT = (((1,), (1,)), ((), ()))
_TN = (((0,), (0,)), ((), ()))


def _pick(dim, candidates):
    for c in candidates:
        if c <= dim and dim % c == 0:
            return c
    return dim


def _params(*sem):
    return pltpu.CompilerParams(dimension_semantics=sem, vmem_limit_bytes=VMEM_LIMIT)


def _bf(x):
    return x if x.dtype == BF16 else x.astype(BF16)


def _log_sigmoid(z):
    return jnp.minimum(z, 0.0) - jnp.log1p(jnp.exp(-jnp.abs(z)))


def _silu(x):
    return x * jax.nn.sigmoid(x)


def _mm_body(*refs, nb, ne, no, nk, prologue, epilogue):
    a_ref = refs[0]
    b_refs = refs[1:1 + nb]
    e_refs = refs[1 + nb:1 + nb + ne]
    o_refs = refs[1 + nb + ne:1 + nb + ne + no]
    acc_refs = refs[1 + nb + ne + no:]
    a = a_ref[...]
    if prologue is not None:
        a = prologue(a)
    a = _bf(a)
    parts = [jnp.dot(a, _bf(b_ref[...]), preferred_element_type=F32) for b_ref in b_refs]

    def finish(accs):
        outs = epilogue(accs, [e[...] for e in e_refs])
        for o_ref, o in zip(o_refs, outs):
            o_ref[...] = o.astype(o_ref.dtype)

    if nk == 1:
        finish(parts)
        return
    k = pl.program_id(2)

    @pl.when(k == 0)
    def _():
        for acc, p in zip(acc_refs, parts):
            acc[...] = p

    @pl.when(k > 0)
    def _():
        for acc, p in zip(acc_refs, parts):
            acc[...] += p

    @pl.when(k == nk - 1)
    def _():
        finish([acc[...] for acc in acc_refs])


def _matmul(a, bs, *, n, tm, tn, tk, out_dtypes, epilogue, extras=(), prologue=None, name="matmul"):
    m, kdim = a.shape
    nk = kdim // tk
    grid = (m // tm, n // tn, nk)
    in_specs = [pl.BlockSpec((tm, tk), lambda i, j, k: (i, k))]
    operands = [a]
    for arr, layer, c0 in bs:
        if arr.ndim == 3:
            in_specs.append(pl.BlockSpec((None, tk, tn), lambda i, j, k, layer=layer, c0=c0: (layer, k, c0 + j)))
        else:
            in_specs.append(pl.BlockSpec((tk, tn), lambda i, j, k, c0=c0: (k, c0 + j)))
        operands.append(arr)
    for arr, shape, imap in extras:
        in_specs.append(pl.BlockSpec(shape, lambda i, j, k, imap=imap: imap(i, j)))
        operands.append(arr)
    out_shape = [jax.ShapeDtypeStruct((m, n), dt) for dt in out_dtypes]
    out_specs = [pl.BlockSpec((tm, tn), lambda i, j, k: (i, j)) for _ in out_dtypes]
    scratch = [pltpu.VMEM((tm, tn), F32) for _ in bs] if nk > 1 else []
    body = functools.partial(_mm_body, nb=len(bs), ne=len(extras), no=len(out_dtypes), nk=nk,
                             prologue=prologue, epilogue=epilogue)
    outs = pl.pallas_call(
        body, grid=grid, in_specs=in_specs, out_specs=out_specs, out_shape=out_shape,
        scratch_shapes=scratch, compiler_params=_params("parallel", "parallel", "arbitrary"), name=name,
    )(*operands)
    return outs


def _epi_identity(accs, extras):
    return [accs[0]]


def _epi_bias(accs, extras):
    return [accs[0] + extras[0]]


def _epi_swiglu(accs, extras):
    return [_silu(accs[0]) * accs[1]]


def _epi_glu(accs, extras):
    z, bias = extras
    return [z * jax.nn.sigmoid(accs[0] + bias)]


class _Mod:
    def __init__(self, arr, d, l, expanded):
        self.arr, self.d, self.l, self.expanded = arr, d, l, expanded

    def spec(self, piece, tm):
        d = self.d
        if self.expanded:
            return pl.BlockSpec((None, tm, d), lambda i, *_: (0, i, piece))
        per = self.l // tm
        return pl.BlockSpec((None, 1, d), lambda i, *_: (i // per, 0, piece))


def _modulate_body(x_ref, sc_ref, sh_ref, h_ref):
    h_ref[...] = (x_ref[...] * (1.0 + sc_ref[...]) + sh_ref[...]).astype(h_ref.dtype)


def _modulate(x, mod, tm):
    m, d = x.shape
    row = pl.BlockSpec((tm, d), lambda i: (i, 0))
    return pl.pallas_call(
        _modulate_body, grid=(m // tm,), in_specs=[row, mod.spec(1, tm), mod.spec(0, tm)],
        out_specs=row, out_shape=jax.ShapeDtypeStruct((m, d), BF16),
        compiler_params=_params("parallel"), name="modulate",
    )(x, mod.arr, mod.arr)


def _ln_body(*refs, alpha, has_next):
    if has_next:
        x_ref, f_ref, g_ref, w_ref, b_ref, sc_ref, sh_ref, xo_ref, ho_ref = refs
    else:
        x_ref, f_ref, g_ref, w_ref, b_ref, xo_ref = refs
    y = alpha * x_ref[...] + (1.0 + g_ref[...]) * f_ref[...]
    yc = y - jnp.mean(y, axis=-1, keepdims=True)
    var = jnp.mean(yc * yc, axis=-1, keepdims=True)
    xn = yc * lax.rsqrt(var + LN_EPS) * w_ref[...] + b_ref[...]
    xo_ref[...] = xn
    if has_next:
        ho_ref[...] = (xn * (1.0 + sc_ref[...]) + sh_ref[...]).astype(ho_ref.dtype)


def _deepnorm(x, f, mod, gate_piece, w, b, layer, alpha, tm, nxt=None):
    m, d = x.shape
    row = pl.BlockSpec((tm, d), lambda i: (i, 0))
    vec = pl.BlockSpec((None, 1, d), lambda i: (layer, 0, 0))
    in_specs = [row, row, mod.spec(gate_piece, tm), vec, vec]
    operands = [x, f, mod.arr, w, b]
    out_shape = [jax.ShapeDtypeStruct((m, d), F32)]
    out_specs = [row]
    if nxt is not None:
        nmod, sc_piece, sh_piece = nxt
        in_specs += [nmod.spec(sc_piece, tm), nmod.spec(sh_piece, tm)]
        operands += [nmod.arr, nmod.arr]
        out_shape.append(jax.ShapeDtypeStruct((m, d), BF16))
        out_specs.append(row)
    outs = pl.pallas_call(
        functools.partial(_ln_body, alpha=alpha, has_next=nxt is not None),
        grid=(m // tm,), in_specs=in_specs, out_specs=out_specs, out_shape=out_shape,
        compiler_params=_params("parallel"), name="deepnorm",
    )(*operands)
    return outs if nxt is not None else (outs[0], None)


def _s5_body(u_ref, bt_ref, ct_ref, sc_ref, d_ref, x0_ref, z_ref, xo_ref, xs_ref, st_ref, *, t_rows):
    t = pl.program_id(2)
    sw = st_ref.shape[-1] // 2

    @pl.when(t == 0)
    def _():
        st_ref[...] = x0_ref[...]

    u = u_ref[...]
    xs_ref[...] = jnp.dot(_bf(u), bt_ref[...], preferred_element_type=F32)
    row = lax.broadcasted_iota(jnp.int32, (SUBLANES, sw), 0)

    def cmul_add(xr, xi, ar, ai, sr, si):
        return xr + ar * sr - ai * si, xi + ar * si + ai * sr

    def block(j, carry):
        cr, ci = carry
        off = pl.multiple_of(j * SUBLANES, SUBLANES)
        xr = xs_ref[pl.ds(off, SUBLANES), 0:sw]
        xi = xs_ref[pl.ds(off, SUBLANES), sw:2 * sw]
        for step, d in enumerate((1, 2, 4)):
            ar = sc_ref[16 * step:16 * step + 8, :]
            ai = sc_ref[16 * step + 8:16 * step + 16, :]
            sr = jnp.where(row >= d, pltpu.roll(xr, d, 0), 0.0)
            si = jnp.where(row >= d, pltpu.roll(xi, d, 0), 0.0)
            xr, xi = cmul_add(xr, xi, ar, ai, sr, si)
        xr, xi = cmul_add(xr, xi, sc_ref[48:56, :], sc_ref[56:64, :], cr, ci)
        xs_ref[pl.ds(off, SUBLANES), 0:sw] = xr
        xs_ref[pl.ds(off, SUBLANES), sw:2 * sw] = xi
        return xr[SUBLANES - 1:SUBLANES, :], xi[SUBLANES - 1:SUBLANES, :]

    st = st_ref[...]
    cr, ci = lax.fori_loop(0, t_rows // SUBLANES, block, (st[:, 0:sw], st[:, sw:2 * sw]))
    st_ref[:, 0:sw] = cr
    st_ref[:, sw:2 * sw] = ci
    y = jnp.dot(_bf(xs_ref[...]), ct_ref[...], preferred_element_type=F32) + d_ref[...] * u
    z_ref[...] = 0.5 * y * (1.0 + jnp.tanh(math.sqrt(2.0 / math.pi) * (y + 0.044715 * (y * y * y))))

    @pl.when(t == pl.num_programs(2) - 1)
    def _():
        xo_ref[:, 0:sw] = cr
        xo_ref[:, sw:2 * sw] = ci


def _s5_tables(a_re, a_im, log_dt, b_re, b_im, c_re, c_im):
    g, p = a_re.shape
    hg = b_re.shape[-1]
    gb = g // S5_LANE_GROUPS
    lam = lax.complex(a_re.astype(F32), a_im.astype(F32))
    dt = jnp.exp(log_dt.astype(F32))[:, None]
    a_bar = jnp.exp(lam * dt)
    b_bar = ((a_bar - 1.0) / lam)[..., None] * lax.complex(b_re.astype(F32), b_im.astype(F32))
    eye = jnp.eye(S5_LANE_GROUPS, dtype=F32)

    def blockdiag_in(w):
        w = w.reshape(gb, S5_LANE_GROUPS, p, hg)
        return jnp.einsum('bgph,gk->bghkp', w, eye).reshape(gb, S5_LANE_GROUPS * hg, S5_LANE_GROUPS * p)

    def blockdiag_out(w):
        w = w.reshape(gb, S5_LANE_GROUPS, hg, p)
        return jnp.einsum('bgkp,gj->bgpjk', w, eye).reshape(gb, S5_LANE_GROUPS * p, S5_LANE_GROUPS * hg)

    bt = jnp.concatenate([blockdiag_in(b_bar.real), blockdiag_in(b_bar.imag)], axis=2).astype(BF16)
    ct = jnp.concatenate([blockdiag_out(c_re.astype(F32)), blockdiag_out(-c_im.astype(F32))], axis=1).astype(BF16)
    sw = S5_LANE_GROUPS * p

    def rows8(z):
        z = z.reshape(gb, 1, sw)
        return (jnp.broadcast_to(z.real, (gb, SUBLANES, sw)), jnp.broadcast_to(z.imag, (gb, SUBLANES, sw)))

    pieces = []
    apow = a_bar
    pows = [a_bar]
    for _ in range(SUBLANES - 1):
        apow = apow * a_bar
        pows.append(apow)
    for d in (1, 2, 4):
        pieces += list(rows8(pows[d - 1]))
    ap = jnp.stack(pows, axis=0).reshape(SUBLANES, gb, sw).transpose(1, 0, 2)
    pieces += [ap.real, ap.imag]
    sc = jnp.concatenate(pieces, axis=1)
    return bt, ct, sc


def _s5(proj, n, l, tables, d_skip, layer, x0, t_rows):
    bt, ct, sc = tables
    gb, cw, sw2 = bt.shape
    nt = l // t_rows
    grid = (n, gb, nt)
    z, xo = pl.pallas_call(
        functools.partial(_s5_body, t_rows=t_rows), grid=grid,
        in_specs=[
            pl.BlockSpec((t_rows, cw), lambda b, g, t: (b * nt + t, g)),
            pl.BlockSpec((None, cw, sw2), lambda b, g, t: (g, 0, 0)),
            pl.BlockSpec((None, sw2, cw), lambda b, g, t: (g, 0, 0)),
            pl.BlockSpec((None, 8 * SUBLANES, sw2 // 2), lambda b, g, t: (g, 0, 0)),
            pl.BlockSpec((None, 1, cw), lambda b, g, t: (layer, 0, g)),
            pl.BlockSpec((None, 1, sw2), lambda b, g, t: (b * gb + g, 0, 0)),
        ],
        out_specs=[
            pl.BlockSpec((t_rows, cw), lambda b, g, t: (b * nt + t, g)),
            pl.BlockSpec((None, 1, sw2), lambda b, g, t: (b * gb + g, 0, 0)),
        ],
        out_shape=[jax.ShapeDtypeStruct((n * l, gb * cw), F32),
                   jax.ShapeDtypeStruct((n * gb, 1, sw2), F32)],
        scratch_shapes=[pltpu.VMEM((t_rows, sw2), F32), pltpu.VMEM((1, sw2), F32)],
        compiler_params=_params("parallel", "parallel", "arbitrary"), name="s5_scan",
    )(proj, bt, ct, sc, d_skip, x0)
    return z, xo


def _sb_tile(q, kt, vt, carry, mask, scale):
    tk = kt.shape[0]
    z = lax.dot_general(q, _bf(kt), _NT, preferred_element_type=F32) * scale
    ls = _log_sigmoid(z)
    lf = ls - z
    if mask is not None:
        lf = jnp.where(mask, lf, 0.0)
    rj = lax.broadcasted_iota(jnp.int32, (tk, tk), 0)
    cs = lax.broadcasted_iota(jnp.int32, (tk, tk), 1)
    upper = jnp.where(rj > cs, 1.0, 0.0).astype(BF16)
    hi = lf.astype(BF16)
    lo = (lf - hi.astype(F32)).astype(BF16)
    newer = (jnp.dot(hi, upper, preferred_element_type=F32) + jnp.dot(lo, upper, preferred_element_type=F32))
    w = jnp.exp(ls + carry + newer)
    if mask is not None:
        w = jnp.where(mask, w, 0.0)
    out = jnp.dot(w.astype(BF16), _bf(vt), preferred_element_type=F32)
    return out, carry + jnp.sum(lf, axis=1, keepdims=True)


def _sb_body(q_ref, kd_ref, vd_ref, kp_ref, vp_ref, o_ref, *, tk, prev_tiles):
    tq, d = q_ref.shape
    scale = d ** -0.5
    q = _bf(q_ref[...])
    r = lax.broadcasted_iota(jnp.int32, (tq, tq), 0)
    c = lax.broadcasted_iota(jnp.int32, (tq, tq), 1)
    acc, carry = _sb_tile(q, kd_ref[...], vd_ref[...], jnp.zeros((tq, 1), F32), c < r, scale)
    nprev = prev_tiles(pl.program_id(2))

    def body(jj, state):
        acc, carry = state
        off = pl.multiple_of((nprev - 1 - jj) * tk, tk)
        out, carry = _sb_tile(q, kp_ref[pl.ds(off, tk), :], vp_ref[pl.ds(off, tk), :], carry, None, scale)
        return acc + out, carry

    acc, _ = lax.fori_loop(0, nprev, body, (acc, carry))
    o_ref[...] = acc.astype(o_ref.dtype)


def _stick_breaking(proj, n, l, heads, hd, qcol, kcol, vcol, prev=None):
    if prev is None:
        tq = _pick(l, (128, 64, 32, 16, 8))
        tk = tq
        nq = l // tq
        kp, vp = proj, proj
        kp_spec = pl.BlockSpec((l, hd), lambda b, h, i: (b, kcol + h))
        vp_spec = pl.BlockSpec((l, hd), lambda b, h, i: (b, vcol + h))
        prev_tiles = lambda i: i
    else:
        kc, vc, layer = prev
        past = kc.shape[1] // n
        tq, nq = l, 1
        tk = _pick(past, (128, 64, 32, 16, 8))
        kp, vp = kc, vc
        kp_spec = pl.BlockSpec((None, past, hd), lambda b, h, i: (layer, b, h))
        vp_spec = pl.BlockSpec((None, past, hd), lambda b, h, i: (layer, b, h))
        prev_tiles = lambda i: past // tk
    return pl.pallas_call(
        functools.partial(_sb_body, tk=tk, prev_tiles=prev_tiles), grid=(n, heads, nq),
        in_specs=[
            pl.BlockSpec((tq, hd), lambda b, h, i: (b * nq + i, qcol + h)),
            pl.BlockSpec((tq, hd), lambda b, h, i: (b * nq + i, kcol + h)),
            pl.BlockSpec((tq, hd), lambda b, h, i: (b * nq + i, vcol + h)),
            kp_spec, vp_spec,
        ],
        out_specs=pl.BlockSpec((tq, hd), lambda b, h, i: (b * nq + i, h)),
        out_shape=jax.ShapeDtypeStruct((n * l, heads * hd), BF16),
        compiler_params=_params("parallel", "parallel", "arbitrary"), name="stick_breaking",
    )(proj, proj, proj, kp, vp)


def _gla_body(q_ref, k_ref, v_ref, g_ref, lr_ref, wlr_ref, blr_ref, nw_ref, s0_ref, y_ref, so_ref, st_ref):
    t = pl.program_id(2)
    c, dk = q_ref.shape

    @pl.when(t == 0)
    def _():
        st_ref[...] = s0_ref[...]

    q = q_ref[...] * (dk ** -0.5)
    k = k_ref[...]
    v = v_ref[...]
    vb = _bf(v)
    gate_in = jnp.dot(_bf(lr_ref[...]), _bf(wlr_ref[...]), preferred_element_type=F32) + blr_ref[...]
    logg = _log_sigmoid(gate_in) * (1.0 / GLA_GATE_NORM)
    rt = lax.broadcasted_iota(jnp.int32, (c, c), 0)
    cs = lax.broadcasted_iota(jnp.int32, (c, c), 1)
    lower = jnp.where(cs <= rt, 1.0, 0.0).astype(BF16)
    hi = logg.astype(BF16)
    lo = (logg - hi.astype(F32)).astype(BF16)
    b = jnp.dot(lower, hi, preferred_element_type=F32) + jnp.dot(lower, lo, preferred_element_type=F32)

    st = st_ref[...]
    o_inter = lax.dot_general(_bf(q * jnp.exp(b)), _bf(st), _NT, preferred_element_type=F32)
    row = lax.broadcasted_iota(jnp.int32, (GLA_SUB, 1), 0)
    lane = lax.broadcasted_iota(jnp.int32, (GLA_SUB, GLA_SUB), 1)
    outs = []
    for blk in range(c // GLA_SUB):
        r0, r1 = blk * GLA_SUB, (blk + 1) * GLA_SUB
        bi, qi, ki = b[r0:r1], q[r0:r1], k[r0:r1]
        att = jnp.zeros((GLA_SUB, GLA_SUB), F32)
        for s in range(GLA_SUB):
            dec = jnp.exp(jnp.where(row >= s, bi - bi[s:s + 1, :], NEG))
            col = jnp.sum(qi * dec * ki[s:s + 1, :], axis=1, keepdims=True)
            att = jnp.where(lane == s, col, att)
        oi = jnp.dot(_bf(att), vb[r0:r1], preferred_element_type=F32)
        if blk > 0:
            bref = b[r0 - 1:r0, :]
            qt = qi * jnp.exp(bi - bref)
            kt = k[0:r0] * jnp.exp(bref - b[0:r0])
            a_off = lax.dot_general(_bf(qt), _bf(kt), _NT, preferred_element_type=F32)
            oi = oi + jnp.dot(_bf(a_off), vb[0:r0], preferred_element_type=F32)
        outs.append(oi)
    o = (outs[0] if len(outs) == 1 else jnp.concatenate(outs, axis=0)) + o_inter
    o = o * lax.rsqrt(jnp.mean(o * o, axis=-1, keepdims=True) + RMS_EPS) * nw_ref[...]
    y_ref[...] = (o * _silu(g_ref[...])).astype(y_ref.dtype)

    bl = b[c - 1:c, :]
    kh = k * jnp.exp(bl - b)
    new = st * jnp.exp(bl) + lax.dot_general(vb, _bf(kh), _TN, preferred_element_type=F32)
    st_ref[...] = new

    @pl.when(t == pl.num_programs(2) - 1)
    def _():
        so_ref[...] = new


def _gla(proj, lr, n, l, heads, dk, dv, w_lr, b_lr, norm_w, layer, s0t):
    c = min(CHUNK, l)
    nc = l // c
    qb, kb, vb_, gb_ = 0, heads, (2 * heads * dk) // dv, (2 * heads * dk) // dv + heads
    rank_pad = lr.shape[1]
    y, so = pl.pallas_call(
        _gla_body, grid=(n, heads, nc),
        in_specs=[
            pl.BlockSpec((c, dk), lambda b, h, t: (b * nc + t, qb + h)),
            pl.BlockSpec((c, dk), lambda b, h, t: (b * nc + t, kb + h)),
            pl.BlockSpec((c, dv), lambda b, h, t: (b * nc + t, vb_ + h)),
            pl.BlockSpec((c, dv), lambda b, h, t: (b * nc + t, gb_ + h)),
            pl.BlockSpec((c, rank_pad), lambda b, h, t: (b * nc + t, 0)),
            pl.BlockSpec((None, rank_pad, dk), lambda b, h, t: (layer, 0, h)),
            pl.BlockSpec((None, 1, dk), lambda b, h, t: (layer, 0, h)),
            pl.BlockSpec((None, 1, dv), lambda b, h, t: (layer, 0, 0)),
            pl.BlockSpec((None, dv, dk), lambda b, h, t: (b * heads + h, 0, 0)),
        ],
        out_specs=[
            pl.BlockSpec((c, dv), lambda b, h, t: (b * nc + t, h)),
            pl.BlockSpec((None, dv, dk), lambda b, h, t: (b * heads + h, 0, 0)),
        ],
        out_shape=[jax.ShapeDtypeStruct((n * l, heads * dv), BF16),
                   jax.ShapeDtypeStruct((n * heads, dv, dk), F32)],
        scratch_shapes=[pltpu.VMEM((dv, dk), F32)],
        compiler_params=_params("parallel", "parallel", "arbitrary"), name="gla",
    )(proj, proj, proj, proj, lr, w_lr, b_lr, norm_w, s0t)
    return y, so


def _band_body(q_ref, k_ref, v_ref, bias_ref, o_ref, *, window, first_valid):
    tq, d = q_ref.shape
    i = pl.program_id(2)
    start = pl.multiple_of(i * tq, tq)
    q = _bf(q_ref[...])
    kw = _bf(k_ref[pl.ds(start, window), :])
    vw = _bf(v_ref[pl.ds(start, window), :])
    s = lax.dot_general(q, kw, _NT, preferred_element_type=F32) * (d ** -0.5) + bias_ref[...]
    if first_valid is not None:
        kl = lax.broadcasted_iota(jnp.int32, (tq, window), 1)
        s = jnp.where(kl >= first_valid - start, s, NEG)
    p = jnp.exp(s - jnp.max(s, axis=-1, keepdims=True))
    o = jnp.dot(p.astype(BF16), vw, preferred_element_type=F32) / jnp.sum(p, axis=-1, keepdims=True)
    o_ref[...] = o.astype(o_ref.dtype)


def _toeplitz_bias(table, tq, window, offset, static_mask=None):
    rows, heads = table.shape
    clip = (rows - 1) // 2
    period = tq + window
    m = np.arange(period)
    m = np.where(m >= window, m - period, m)
    idx = np.clip(offset - m, -clip, clip) + clip
    vec = table.astype(F32)[idx, :].T
    flat = jnp.tile(vec, (1, tq))[:, :tq * (period - 1)]
    bias = flat.reshape(heads, tq, period - 1)[:, :, :window]
    if static_mask is not None:
        bias = jnp.where(static_mask[None], bias, NEG)
    return bias


def _band(q_arr, qcol, k_arr, v_arr, kcol, vcol, n, l, heads, hd, bias, tq, window, first_valid, out_rows):
    nq = l // tq
    lk = k_arr.shape[0] // n
    return pl.pallas_call(
        functools.partial(_band_body, window=window, first_valid=first_valid), grid=(n, heads, nq),
        in_specs=[
            pl.BlockSpec((tq, hd), lambda b, h, i: (b * nq + i, qcol + h)),
            pl.BlockSpec((lk, hd), lambda b, h, i: (b, kcol + h)),
            pl.BlockSpec((lk, hd), lambda b, h, i: (b, vcol + h)),
            pl.BlockSpec((None, tq, window), lambda b, h, i: (h, 0, 0)),
        ],
        out_specs=pl.BlockSpec((tq, hd), lambda b, h, i: (b * nq + i, h)),
        out_shape=jax.ShapeDtypeStruct((out_rows, heads * hd), BF16),
        compiler_params=_params("parallel", "parallel", "arbitrary"), name="band_attention",
    )(q_arr, k_arr, v_arr, bias)


def _trunk(x, n, l, mods, st, p, dims):
    m, d = x.shape
    depth = p['w_mod'].shape[0]
    alpha = (2 * depth) ** 0.25
    expanded = l % 256 != 0
    tm_big = m if expanded else _pick(l, (1024, 512, 256))
    tm_row = m if expanded else 256
    d_ff = p['ffn_w_gate'].shape[2]
    g_, p_, hg = dims['s5']
    mix_a = g_ * hg
    sb_heads, sb_hd = dims['sb']
    gla_heads, gla_dk, gla_dv = dims['gla']
    ca_heads, ca_hd = dims['ca']
    sb_w = sb_heads * sb_hd
    ca_w = ca_heads * ca_hd
    gla_w = gla_heads * gla_dv

    def mod_of(layer):
        if expanded:
            arr = jnp.repeat(mods[layer], l, axis=0)[None]
        else:
            arr = mods[layer][:, None, :]
        return _Mod(arr, d, l, expanded)

    outs = {k: [] for k in ('s5_re', 's5_im', 'sb_k', 'sb_v', 'gla', 'band_k', 'band_v')}
    mod = mod_of(0)
    h = _modulate(x, mod, tm_row)
    for layer in range(depth):
        i = layer // 2
        if layer % 2 == 0:
            in_ab = p['ab_w_in'].shape[2]
            proj, = _matmul(h, [(p['ab_w_in'], i, 0)], n=in_ab, tm=tm_big, tn=256, tk=d, out_dtypes=[F32],
                            epilogue=_epi_identity, name="ab_in")
            gb = g_ // S5_LANE_GROUPS
            sw = S5_LANE_GROUPS * p_
            x0 = jnp.concatenate([st['s5_re'][i].reshape(n * gb, 1, sw), st['s5_im'][i].reshape(n * gb, 1, sw)], axis=2)
            z, xo = _s5(proj, n, l, p['s5_tables'][i], p['s5_d'], i, x0, _pick(l, (256, 128, 64, 32, 16, 8)))
            outs['s5_re'].append(xo[:, 0, :sw].reshape(n, g_, p_))
            outs['s5_im'].append(xo[:, 0, sw:].reshape(n, g_, p_))
            tn_glu = 256
            ya, = _matmul(z, [(p['s5_w_glu'], i, 0)], n=mix_a, tm=tm_big, tn=tn_glu, tk=mix_a, out_dtypes=[BF16],
                          epilogue=_epi_glu, name="s5_glu",
                          extras=[(z, (tm_big, tn_glu), lambda a, b: (a, b)),
                                  (p['s5_b_glu'], (None, 1, tn_glu), lambda a, b, i=i: (i, 0, b))])
            qc, kc, vc = mix_a // sb_hd, (mix_a + sb_w) // sb_hd, (mix_a + 2 * sb_w) // sb_hd
            prev = None if st['sb_k'] is None else (st['sb_k'], st['sb_v'], i)
            yb = _stick_breaking(proj, n, l, sb_heads, sb_hd, qc, kc, vc, prev)
            outs['sb_k'].append(proj[:, mix_a + sb_w:mix_a + 2 * sb_w].reshape(n, l, sb_heads, sb_hd))
            outs['sb_v'].append(proj[:, mix_a + 2 * sb_w:mix_a + 3 * sb_w].reshape(n, l, sb_heads, sb_hd))
            merged = jnp.concatenate([ya, yb], axis=1)
            w_out = p['ab_w_out']
        else:
            w_main, w_lrin = p['cd_w_main'][i], p['cd_w_lrin'][i]
            n_main = w_main.shape[1]
            proj, = _matmul(h, [(w_main, None, 0)], n=n_main, tm=tm_big, tn=256, tk=d, out_dtypes=[F32],
                            epilogue=_epi_identity, name="cd_in")
            lr, = _matmul(h, [(w_lrin, None, 0)], n=LANES, tm=tm_big, tn=LANES, tk=d, out_dtypes=[F32],
                          epilogue=_epi_identity, name="cd_lr")
            s0t = jnp.swapaxes(st['gla'][i], -1, -2).reshape(n * gla_heads, gla_dv, gla_dk)
            yc, so = _gla(proj, lr, n, l, gla_heads, gla_dk, gla_dv, p['gla_w_lr_pad'], p['gla_b_lr'],
                          p['gla_norm_w'], i, s0t)
            outs['gla'].append(jnp.swapaxes(so.reshape(n, gla_heads, gla_dv, gla_dk), -1, -2))
            c0 = 2 * gla_heads * gla_dk + 2 * gla_w
            qa_col = c0 // ca_hd
            ka = proj[:, c0 + ca_w:c0 + 2 * ca_w]
            va = proj[:, c0 + 2 * ca_w:c0 + 3 * ca_w]
            table = p['ca_rel_bias'][i]
            if st['band_k'] is None:
                tq = _pick(l, (256, 128, 64))
                window = tq + BAND_PAST
                pad = jnp.zeros((n, BAND_PAST, ca_w), F32)
                kpad = jnp.concatenate([pad, ka.reshape(n, l, ca_w)], axis=1).reshape(n * (l + BAND_PAST), ca_w)
                vpad = jnp.concatenate([pad, va.reshape(n, l, ca_w)], axis=1).reshape(n * (l + BAND_PAST), ca_w)
                qchunk = np.arange(tq)[:, None] // CHUNK
                kchunk = np.arange(window)[None, :] // CHUNK
                static_mask = (kchunk >= qchunk) & (kchunk <= qchunk + BAND_PREV)
                bias = _toeplitz_bias(table, tq, window, BAND_PAST, static_mask)
                yd = _band(proj, qa_col, kpad, vpad, 0, 0, n, l, ca_heads, ca_hd, bias, tq, window, BAND_PAST, m)
                keep = min(BAND_PAST, l)
                outs['band_k'].append(ka.reshape(n, l, ca_heads, ca_hd)[:, l - keep:])
                outs['band_v'].append(va.reshape(n, l, ca_heads, ca_hd)[:, l - keep:])
            else:
                r = st['band_k'].shape[2]
                kcat = jnp.concatenate([st['band_k'][i].reshape(n, r, ca_w), ka.reshape(n, l, ca_w)], axis=1)
                vcat = jnp.concatenate([st['band_v'][i].reshape(n, r, ca_w), va.reshape(n, l, ca_w)], axis=1)
                bias = _toeplitz_bias(table, l, r + l, r)
                yd = _band(proj, qa_col, kcat.reshape(n * (r + l), ca_w), vcat.reshape(n * (r + l), ca_w), 0, 0,
                           n, l, ca_heads, ca_hd, bias, l, r + l, None, m)
                outs['band_k'].append(ka.reshape(n, l, ca_heads, ca_hd))
                outs['band_v'].append(va.reshape(n, l, ca_heads, ca_hd))
            merged = jnp.concatenate([yc, yd], axis=1)
            w_out = p['cd_w_out']
        f, = _matmul(merged, [(w_out, i, 0)], n=d, tm=tm_big, tn=256, tk=merged.shape[1], out_dtypes=[F32],
                     epilogue=_epi_identity, name="mix_out")
        x, h = _deepnorm(x, f, mod, 2, p['ln1_w'], p['ln1_b'], layer, alpha, tm_row, nxt=(mod, 4, 3))
        tn_ff = _pick(d_ff, (256, 128))
        act, = _matmul(h, [(p['ffn_w_gate'], layer, 0), (p['ffn_w_up'], layer, 0)], n=d_ff, tm=tm_big, tn=tn_ff,
                       tk=d, out_dtypes=[BF16], epilogue=_epi_swiglu, name="ffn_in")
        tk_down = d_ff // 2 if (d_ff // 2) % LANES == 0 else d_ff
        ff, = _matmul(act, [(p['ffn_w_down'], layer, 0)], n=d, tm=tm_big, tn=256, tk=tk_down, out_dtypes=[F32],
                      epilogue=_epi_identity, name="ffn_out")
        if layer + 1 < depth:
            nmod = mod_of(layer + 1)
            x, h = _deepnorm(x, ff, mod, 5, p['ln2_w'], p['ln2_b'], layer, alpha, tm_row, nxt=(nmod, 1, 0))
            mod = nmod
        else:
            x, _ = _deepnorm(x, ff, mod, 5, p['ln2_w'], p['ln2_b'], layer, alpha, tm_row)
    return x, {k: jnp.stack(v) for k, v in outs.items()}


def kernel(x_prompt, x_sample, state_s5_re, state_s5_im, cache_sb_k, cache_sb_v, state_gla, cache_band_k, cache_band_v, c_prompt, c_sample, w_mod, b_mod, ln1_w, ln1_b, ln2_w, ln2_b, ab_w_in, ab_w_out, s5_a_re, s5_a_im, s5_log_dt, s5_b_re, s5_b_im, s5_c_re, s5_c_im, s5_d, s5_w_glu, s5_b_glu, cd_w_in, cd_w_out, gla_w_lr, gla_b_lr, gla_norm_w, ca_rel_bias, ffn_w_gate, ffn_w_up, ffn_w_down):
    nb, seq, d = x_prompt.shape
    db, dseq, _ = x_sample.shape
    depth = w_mod.shape[0]
    n_even, g_, p_ = s5_a_re.shape
    hg = s5_b_re.shape[-1]
    _, _, past, sb_heads, sb_hd = cache_sb_k.shape
    n_odd, _, gla_heads, gla_dk, gla_dv = state_gla.shape
    _, _, band_rows, ca_heads, ca_hd = cache_band_k.shape
    rank = gla_w_lr.shape[1]
    dims = {'s5': (g_, p_, hg), 'sb': (sb_heads, sb_hd), 'gla': (gla_heads, gla_dk, gla_dv), 'ca': (ca_heads, ca_hd)}

    lr0 = 2 * gla_heads * gla_dk + 2 * gla_heads * gla_dv
    p = dict(
        w_mod=w_mod, ln1_w=ln1_w[:, None, :], ln1_b=ln1_b[:, None, :], ln2_w=ln2_w[:, None, :], ln2_b=ln2_b[:, None, :],
        ab_w_in=ab_w_in, ab_w_out=ab_w_out, s5_d=s5_d[:, None, :], s5_w_glu=s5_w_glu, s5_b_glu=s5_b_glu[:, None, :],
        cd_w_out=cd_w_out, gla_b_lr=gla_b_lr[:, None, :], gla_norm_w=gla_norm_w[:, None, :], ca_rel_bias=ca_rel_bias, ffn_w_gate=ffn_w_gate, ffn_w_up=ffn_w_up,
        ffn_w_down=ffn_w_down,
        cd_w_main=[jnp.concatenate([cd_w_in[i, :, :lr0], cd_w_in[i, :, lr0 + rank:]], axis=1).astype(BF16)
                   for i in range(n_odd)],
        cd_w_lrin=[jnp.pad(cd_w_in[i, :, lr0:lr0 + rank], ((0, 0), (0, LANES - rank))).astype(BF16)
                   for i in range(n_odd)],
        gla_w_lr_pad=jnp.pad(gla_w_lr, ((0, 0), (0, LANES - rank), (0, 0))),
        s5_tables=[_s5_tables(s5_a_re[i], s5_a_im[i], s5_log_dt[i], s5_b_re[i], s5_b_im[i], s5_c_re[i], s5_c_im[i])
                   for i in range(n_even)],
    )

    rows = nb + db
    rows_pad = -(-rows // 16) * 16
    c_all = jnp.concatenate([c_prompt, c_sample, jnp.zeros((rows_pad - rows, d), F32)], axis=0)
    mods = []
    for layer in range(depth):
        tn_mod = _pick(6 * d, (512, 256, 128))
        mod_l, = _matmul(c_all, [(w_mod, layer, 0)], n=6 * d, tm=rows_pad, tn=tn_mod, tk=d, out_dtypes=[F32],
                         epilogue=_epi_bias, prologue=_silu, name="adaln_mod",
                         extras=[(b_mod[:, None, :], (None, 1, tn_mod), lambda a, b, layer=layer: (layer, 0, b))])
        mods.append(mod_l)
    mods = jnp.stack(mods)
    mods_p, mods_s = mods[:, :nb], mods[:, nb:rows]

    zeros_s5 = jnp.zeros((n_even, nb, g_, p_), F32)
    st_p = dict(s5_re=zeros_s5, s5_im=zeros_s5, sb_k=None, sb_v=None,
                gla=jnp.zeros((n_odd, nb, gla_heads, gla_dk, gla_dv), F32), band_k=None, band_v=None)
    y_p, o_p = _trunk(x_prompt.reshape(nb * seq, d), nb, seq, mods_p, st_p, p, dims)
    st_s = dict(s5_re=state_s5_re, s5_im=state_s5_im,
                sb_k=cache_sb_k.reshape(n_even, db * past, sb_heads * sb_hd),
                sb_v=cache_sb_v.reshape(n_even, db * past, sb_heads * sb_hd),
                gla=state_gla, band_k=cache_band_k, band_v=cache_band_v)
    y_s, o_s = _trunk(x_sample.reshape(db * dseq, d), db, dseq, mods_s, st_s, p, dims)
    return (y_p.reshape(nb, seq, d), y_s.reshape(db, dseq, d),
            o_p['s5_re'], o_p['s5_im'], o_p['sb_k'], o_p['sb_v'], o_p['gla'], o_p['band_k'], o_p['band_v'],
            o_s['s5_re'], o_s['s5_im'], o_s['sb_k'], o_s['sb_v'], o_s['gla'], o_s['band_k'], o_s['band_v'])
```

```python
import functools
import math

import numpy as np
import jax
import jax.numpy as jnp
from jax import lax
from jax.experimental import pallas as pl
from jax.experimental.pallas import tpu as pltpu

F32 = jnp.float32
BF16 = jnp.bfloat16

CHUNK = 64
BAND_PREV = 8
BAND_PAST = BAND_PREV * CHUNK
GLA_GATE_NORM = 16.0
LN_EPS = 1e-5
RMS_EPS = 1e-6

LANES = 128
SUBLANES = 8
V7X_VMEM_BYTES = 64 * 1024 * 1024
VMEM_LIMIT = V7X_VMEM_BYTES - 8 * 1024 * 1024

GLA_SUB = 16
S5_LANE_GROUPS = 8
NEG = -1e30
SB_EXIT = -104.0

_NT = (((1,), (1,)), ((), ()))
_TN = (((0,), (0,)), ((), ()))


def _pick(dim, candidates):
    for c in candidates:
        if c <= dim and dim % c == 0:
            return c
    return dim


def _params(*sem):
    return pltpu.CompilerParams(dimension_semantics=sem, vmem_limit_bytes=VMEM_LIMIT)


def _bf(x):
    return x if x.dtype == BF16 else x.astype(BF16)


def _log_sigmoid(z):
    return jnp.minimum(z, 0.0) - jnp.log1p(jnp.exp(-jnp.abs(z)))


def _silu(x):
    return x * jax.nn.sigmoid(x)


def _mm_body(*refs, na, nb, ne, no, nk, prologue, epilogue):
    a_refs, refs = refs[:na], refs[na:]
    b_refs = refs[0:nb]
    e_refs = refs[nb:nb + ne]
    o_refs = refs[nb + ne:nb + ne + no]
    acc_refs = refs[nb + ne + no:]
    a_parts = [a_ref[...] for a_ref in a_refs]
    if prologue is not None:
        a_parts = [prologue(a) for a in a_parts]
    a_parts = [_bf(a) for a in a_parts]
    a = a_parts[0] if na == 1 else jnp.concatenate(a_parts, axis=1)
    parts = [jnp.dot(a, _bf(b_ref[...]), preferred_element_type=F32) for b_ref in b_refs]

    def finish(accs):
        outs = epilogue(accs, [e[...] for e in e_refs])
        for o_ref, o in zip(o_refs, outs):
            o_ref[...] = o.astype(o_ref.dtype)

    if nk == 1:
        finish(parts)
        return
    k = pl.program_id(2)

    @pl.when(k == 0)
    def _():
        for acc, p in zip(acc_refs, parts):
            acc[...] = p

    @pl.when(k > 0)
    def _():
        for acc, p in zip(acc_refs, parts):
            acc[...] += p

    @pl.when(k == nk - 1)
    def _():
        finish([acc[...] for acc in acc_refs])


def _matmul(a, bs, *, n, tm, tn, tk, out_dtypes, epilogue, extras=(), prologue=None, name="matmul"):
    a_list = list(a) if isinstance(a, (list, tuple)) else [a]
    m = a_list[0].shape[0]
    kdim = sum(x.shape[1] for x in a_list)
    nk = kdim // tk
    assert len(a_list) == 1 or nk == 1
    grid = (m // tm, n // tn, nk)
    if len(a_list) == 1:
        in_specs = [pl.BlockSpec((tm, tk), lambda i, j, k: (i, k))]
    else:
        in_specs = [pl.BlockSpec((tm, x.shape[1]), lambda i, j, k: (i, 0)) for x in a_list]
    operands = list(a_list)
    for arr, layer, c0 in bs:
        if arr.ndim == 3:
            in_specs.append(pl.BlockSpec((None, tk, tn), lambda i, j, k, layer=layer, c0=c0: (layer, k, c0 + j)))
        else:
            in_specs.append(pl.BlockSpec((tk, tn), lambda i, j, k, c0=c0: (k, c0 + j)))
        operands.append(arr)
    for arr, shape, imap in extras:
        in_specs.append(pl.BlockSpec(shape, lambda i, j, k, imap=imap: imap(i, j)))
        operands.append(arr)
    out_shape = [jax.ShapeDtypeStruct((m, n), dt) for dt in out_dtypes]
    out_specs = [pl.BlockSpec((tm, tn), lambda i, j, k: (i, j)) for _ in out_dtypes]
    scratch = [pltpu.VMEM((tm, tn), F32) for _ in bs] if nk > 1 else []
    body = functools.partial(_mm_body, na=len(a_list), nb=len(bs), ne=len(extras), no=len(out_dtypes), nk=nk,
                             prologue=prologue, epilogue=epilogue)
    outs = pl.pallas_call(
        body, grid=grid, in_specs=in_specs, out_specs=out_specs, out_shape=out_shape,
        scratch_shapes=scratch, compiler_params=_params("parallel", "parallel", "arbitrary"), name=name,
    )(*operands)
    return outs


def _epi_identity(accs, extras):
    return [accs[0]]


def _epi_bias(accs, extras):
    return [accs[0] + extras[0]]


def _epi_swiglu(accs, extras):
    return [_silu(accs[0]) * accs[1]]


def _epi_glu(accs, extras):
    z, bias = extras
    return [z * jax.nn.sigmoid(accs[0] + bias)]


class _Mod:
    def __init__(self, arr, d, l, expanded):
        self.arr, self.d, self.l, self.expanded = arr, d, l, expanded

    def spec(self, piece, tm):
        d = self.d
        if self.expanded:
            return pl.BlockSpec((None, tm, d), lambda i, *_: (0, i, piece))
        per = self.l // tm
        return pl.BlockSpec((None, 1, d), lambda i, *_: (i // per, 0, piece))


def _modulate_body(x_ref, sc_ref, sh_ref, h_ref):
    h_ref[...] = (x_ref[...] * (1.0 + sc_ref[...]) + sh_ref[...]).astype(h_ref.dtype)


def _modulate(x, mod, tm):
    m, d = x.shape
    row = pl.BlockSpec((tm, d), lambda i: (i, 0))
    return pl.pallas_call(
        _modulate_body, grid=(m // tm,), in_specs=[row, mod.spec(1, tm), mod.spec(0, tm)],
        out_specs=row, out_shape=jax.ShapeDtypeStruct((m, d), BF16),
        compiler_params=_params("parallel"), name="modulate",
    )(x, mod.arr, mod.arr)


def _ln_body(*refs, alpha, has_next):
    if has_next:
        x_ref, f_ref, g_ref, w_ref, b_ref, sc_ref, sh_ref, xo_ref, ho_ref = refs
    else:
        x_ref, f_ref, g_ref, w_ref, b_ref, xo_ref = refs
    y = alpha * x_ref[...] + (1.0 + g_ref[...]) * f_ref[...]
    yc = y - jnp.mean(y, axis=-1, keepdims=True)
    var = jnp.mean(yc * yc, axis=-1, keepdims=True)
    xn = yc * lax.rsqrt(var + LN_EPS) * w_ref[...] + b_ref[...]
    xo_ref[...] = xn
    if has_next:
        ho_ref[...] = (xn * (1.0 + sc_ref[...]) + sh_ref[...]).astype(ho_ref.dtype)


def _deepnorm(x, f, mod, gate_piece, w, b, layer, alpha, tm, nxt=None):
    m, d = x.shape
    row = pl.BlockSpec((tm, d), lambda i: (i, 0))
    vec = pl.BlockSpec((None, 1, d), lambda i: (layer, 0, 0))
    in_specs = [row, row, mod.spec(gate_piece, tm), vec, vec]
    operands = [x, f, mod.arr, w, b]
    out_shape = [jax.ShapeDtypeStruct((m, d), F32)]
    out_specs = [row]
    if nxt is not None:
        nmod, sc_piece, sh_piece = nxt
        in_specs += [nmod.spec(sc_piece, tm), nmod.spec(sh_piece, tm)]
        operands += [nmod.arr, nmod.arr]
        out_shape.append(jax.ShapeDtypeStruct((m, d), BF16))
        out_specs.append(row)
    outs = pl.pallas_call(
        functools.partial(_ln_body, alpha=alpha, has_next=nxt is not None),
        grid=(m // tm,), in_specs=in_specs, out_specs=out_specs, out_shape=out_shape,
        compiler_params=_params("parallel"), name="deepnorm",
    )(*operands)
    return outs if nxt is not None else (outs[0], None)


def _s5_body(u_ref, bt_ref, ct_ref, sc_ref, d_ref, x0_ref, z_ref, xo_ref, xs_ref, st_ref, *, t_rows):
    t = pl.program_id(2)
    sw = st_ref.shape[-1] // 2

    @pl.when(t == 0)
    def _():
        st_ref[...] = x0_ref[...]

    u = u_ref[...]
    xs_ref[...] = jnp.dot(_bf(u), bt_ref[...], preferred_element_type=F32)
    row = lax.broadcasted_iota(jnp.int32, (SUBLANES, sw), 0)

    def cmul_add(xr, xi, ar, ai, sr, si):
        return xr + ar * sr - ai * si, xi + ar * si + ai * sr

    def block(j, carry):
        cr, ci = carry
        off = pl.multiple_of(j * SUBLANES, SUBLANES)
        xr = xs_ref[pl.ds(off, SUBLANES), 0:sw]
        xi = xs_ref[pl.ds(off, SUBLANES), sw:2 * sw]
        for step, d in enumerate((1, 2, 4)):
            ar = sc_ref[16 * step:16 * step + 8, :]
            ai = sc_ref[16 * step + 8:16 * step + 16, :]
            sr = jnp.where(row >= d, pltpu.roll(xr, d, 0), 0.0)
            si = jnp.where(row >= d, pltpu.roll(xi, d, 0), 0.0)
            xr, xi = cmul_add(xr, xi, ar, ai, sr, si)
        xr, xi = cmul_add(xr, xi, sc_ref[48:56, :], sc_ref[56:64, :], cr, ci)
        xs_ref[pl.ds(off, SUBLANES), 0:sw] = xr
        xs_ref[pl.ds(off, SUBLANES), sw:2 * sw] = xi
        return xr[SUBLANES - 1:SUBLANES, :], xi[SUBLANES - 1:SUBLANES, :]

    st = st_ref[...]
    cr, ci = lax.fori_loop(0, t_rows // SUBLANES, block, (st[:, 0:sw], st[:, sw:2 * sw]))
    st_ref[:, 0:sw] = cr
    st_ref[:, sw:2 * sw] = ci
    y = jnp.dot(_bf(xs_ref[...]), ct_ref[...], preferred_element_type=F32) + d_ref[...] * u
    z_ref[...] = 0.5 * y * (1.0 + jnp.tanh(math.sqrt(2.0 / math.pi) * (y + 0.044715 * (y * y * y))))

    @pl.when(t == pl.num_programs(2) - 1)
    def _():
        xo_ref[:, 0:sw] = cr
        xo_ref[:, sw:2 * sw] = ci


def _s5_tables(a_re, a_im, log_dt, b_re, b_im, c_re, c_im):
    g, p = a_re.shape
    hg = b_re.shape[-1]
    gb = g // S5_LANE_GROUPS
    lr_, li_ = a_re.astype(F32), a_im.astype(F32)
    dt = jnp.exp(log_dt.astype(F32))[:, None]
    mag = jnp.exp(lr_ * dt)
    ab_r, ab_i = mag * jnp.cos(li_ * dt), mag * jnp.sin(li_ * dt)
    den = lr_ * lr_ + li_ * li_
    cr_ = ((ab_r - 1.0) * lr_ + ab_i * li_) / den
    ci_ = (ab_i * lr_ - (ab_r - 1.0) * li_) / den
    br_, bi_ = b_re.astype(F32), b_im.astype(F32)
    bb_r = cr_[..., None] * br_ - ci_[..., None] * bi_
    bb_i = cr_[..., None] * bi_ + ci_[..., None] * br_
    eye = jnp.eye(S5_LANE_GROUPS, dtype=F32)

    def blockdiag_in(w):
        w = w.reshape(gb, S5_LANE_GROUPS, p, hg)
        return jnp.einsum('bgph,gk->bghkp', w, eye).reshape(gb, S5_LANE_GROUPS * hg, S5_LANE_GROUPS * p)

    def blockdiag_out(w):
        w = w.reshape(gb, S5_LANE_GROUPS, hg, p)
        return jnp.einsum('bgkp,gj->bgpjk', w, eye).reshape(gb, S5_LANE_GROUPS * p, S5_LANE_GROUPS * hg)

    bt = jnp.concatenate([blockdiag_in(bb_r), blockdiag_in(bb_i)], axis=2).astype(BF16)
    ct = jnp.concatenate([blockdiag_out(c_re.astype(F32)), blockdiag_out(-c_im.astype(F32))], axis=1).astype(BF16)
    sw = S5_LANE_GROUPS * p

    def rows8(z):
        return jnp.broadcast_to(z.reshape(gb, 1, sw), (gb, SUBLANES, sw))

    pows = [(ab_r, ab_i)]
    for _ in range(SUBLANES - 1):
        pr, pi = pows[-1]
        pows.append((pr * ab_r - pi * ab_i, pr * ab_i + pi * ab_r))
    pieces = []
    for d in (1, 2, 4):
        pieces += [rows8(pows[d - 1][0]), rows8(pows[d - 1][1])]
    for part in (0, 1):
        pieces.append(jnp.stack([pw[part] for pw in pows], axis=0).reshape(SUBLANES, gb, sw).transpose(1, 0, 2))
    sc = jnp.concatenate(pieces, axis=1)
    return bt, ct, sc


def _s5(proj, n, l, tables, d_skip, layer, x0, t_rows):
    bt, ct, sc = tables
    gb, cw, sw2 = bt.shape
    nt = l // t_rows
    grid = (n, gb, nt)
    z, xo = pl.pallas_call(
        functools.partial(_s5_body, t_rows=t_rows), grid=grid,
        in_specs=[
            pl.BlockSpec((t_rows, cw), lambda b, g, t: (b * nt + t, g)),
            pl.BlockSpec((None, cw, sw2), lambda b, g, t: (g, 0, 0)),
            pl.BlockSpec((None, sw2, cw), lambda b, g, t: (g, 0, 0)),
            pl.BlockSpec((None, 8 * SUBLANES, sw2 // 2), lambda b, g, t: (g, 0, 0)),
            pl.BlockSpec((None, 1, cw), lambda b, g, t: (layer, 0, g)),
            pl.BlockSpec((None, 1, sw2), lambda b, g, t: (b * gb + g, 0, 0)),
        ],
        out_specs=[
            pl.BlockSpec((t_rows, cw), lambda b, g, t: (b * nt + t, g)),
            pl.BlockSpec((None, 1, sw2), lambda b, g, t: (b * gb + g, 0, 0)),
        ],
        out_shape=[jax.ShapeDtypeStruct((n * l, gb * cw), F32),
                   jax.ShapeDtypeStruct((n * gb, 1, sw2), F32)],
        scratch_shapes=[pltpu.VMEM((t_rows, sw2), F32), pltpu.VMEM((1, sw2), F32)],
        compiler_params=_params("parallel", "parallel", "arbitrary"), name="s5_scan",
    )(proj, bt, ct, sc, d_skip, x0)
    return z, xo


def _strict_upper2(tk):
    rj = lax.broadcasted_iota(jnp.int32, (tk, tk), 0)
    cs = lax.broadcasted_iota(jnp.int32, (tk, tk), 1)
    upper = jnp.where(rj > cs, 1.0, 0.0).astype(BF16)
    return jnp.concatenate([upper, upper], axis=0)


def _sb_tile(q, kt, vt, carry, mask, scale, upper2):
    z = lax.dot_general(q, _bf(kt), _NT, preferred_element_type=F32) * scale
    ls = _log_sigmoid(z)
    lf = ls - z
    if mask is not None:
        lf = jnp.where(mask, lf, 0.0)
    hi = lf.astype(BF16)
    lo = (lf - hi.astype(F32)).astype(BF16)
    newer = jnp.dot(jnp.concatenate([hi, lo], axis=1), upper2, preferred_element_type=F32)
    w = jnp.exp(ls + carry + newer)
    if mask is not None:
        w = jnp.where(mask, w, 0.0)
    out = jnp.dot(w.astype(BF16), _bf(vt), preferred_element_type=F32)
    return out, carry + jnp.sum(lf, axis=1, keepdims=True)


def _sb_body(q_ref, kd_ref, vd_ref, kp_ref, vp_ref, o_ref, *, tk, prev_tiles):
    tq, d = q_ref.shape
    scale = d ** -0.5
    q = _bf(q_ref[...])
    r = lax.broadcasted_iota(jnp.int32, (tq, tq), 0)
    c = lax.broadcasted_iota(jnp.int32, (tq, tq), 1)
    upper2 = _strict_upper2(tk)
    acc, carry = _sb_tile(q, kd_ref[...], vd_ref[...], jnp.zeros((tq, 1), F32), c < r, scale,
                          upper2 if tq == tk else _strict_upper2(tq))
    nprev = prev_tiles(pl.program_id(2))

    def prev_tile(j, carry, mask):
        off = pl.multiple_of(j * tk, tk)
        return _sb_tile(q, kp_ref[pl.ds(off, tk), :], vp_ref[pl.ds(off, tk), :], carry, mask, scale, upper2)

    out, carry = prev_tile(jnp.maximum(nprev - 1, 0), carry, nprev > 0)
    acc = acc + out

    def cond(state):
        j, _, _, cmax = state
        return jnp.logical_and(j >= 0, cmax > SB_EXIT)

    def body(state):
        j, acc, carry, _ = state
        out, carry = prev_tile(j, carry, None)
        return j - 1, acc + out, carry, jnp.max(carry)

    first = jnp.asarray(nprev - 2, jnp.int32)
    _, acc, _, _ = lax.while_loop(cond, body, (first, acc, carry, jnp.max(carry)))
    o_ref[...] = acc.astype(o_ref.dtype)


def _stick_breaking(proj, n, l, heads, hd, qcol, kcol, vcol, prev=None):
    if prev is None:
        tq = _pick(l, (256, 128, 64, 32, 16, 8))
        tk = tq
        nq = l // tq
        kp, vp = proj, proj
        kp_spec = pl.BlockSpec((l, hd), lambda b, h, i: (b, kcol + h))
        vp_spec = pl.BlockSpec((l, hd), lambda b, h, i: (b, vcol + h))
        prev_tiles = lambda i: i
    else:
        kc, vc, layer = prev
        past = kc.shape[1] // n
        tq, nq = l, 1
        tk = _pick(past, (256, 128, 64, 32, 16, 8))
        kp, vp = kc, vc
        kp_spec = pl.BlockSpec((None, past, hd), lambda b, h, i: (layer, b, h))
        vp_spec = pl.BlockSpec((None, past, hd), lambda b, h, i: (layer, b, h))
        prev_tiles = lambda i: past // tk
    return pl.pallas_call(
        functools.partial(_sb_body, tk=tk, prev_tiles=prev_tiles), grid=(n, heads, nq),
        in_specs=[
            pl.BlockSpec((tq, hd), lambda b, h, i: (b * nq + i, qcol + h)),
            pl.BlockSpec((tq, hd), lambda b, h, i: (b * nq + i, kcol + h)),
            pl.BlockSpec((tq, hd), lambda b, h, i: (b * nq + i, vcol + h)),
            kp_spec, vp_spec,
        ],
        out_specs=pl.BlockSpec((tq, hd), lambda b, h, i: (b * nq + i, h)),
        out_shape=jax.ShapeDtypeStruct((n * l, heads * hd), BF16),
        compiler_params=_params("parallel", "parallel", "arbitrary"), name="stick_breaking",
    )(proj, proj, proj, kp, vp)


def _gla_body(q_ref, k_ref, v_ref, g_ref, lr_ref, wlr_ref, blr_ref, nw_ref, s0_ref, y_ref, so_ref, st_ref):
    t = pl.program_id(2)
    c, dk = q_ref.shape

    @pl.when(t == 0)
    def _():
        st_ref[...] = s0_ref[...]

    q = q_ref[...] * (dk ** -0.5)
    k = k_ref[...]
    v = v_ref[...]
    vb = _bf(v)
    gate_in = jnp.dot(_bf(lr_ref[...]), _bf(wlr_ref[...]), preferred_element_type=F32) + blr_ref[...]
    logg = _log_sigmoid(gate_in) * (1.0 / GLA_GATE_NORM)
    rt = lax.broadcasted_iota(jnp.int32, (c, c), 0)
    cs = lax.broadcasted_iota(jnp.int32, (c, c), 1)
    lower = jnp.where(cs <= rt, 1.0, 0.0).astype(BF16)
    hi = logg.astype(BF16)
    lo = (logg - hi.astype(F32)).astype(BF16)
    b = jnp.dot(lower, hi, preferred_element_type=F32) + jnp.dot(lower, lo, preferred_element_type=F32)

    st = st_ref[...]
    o_inter = lax.dot_general(_bf(q * jnp.exp(b)), _bf(st), _NT, preferred_element_type=F32)
    row = lax.broadcasted_iota(jnp.int32, (GLA_SUB, 1), 0)
    lane = lax.broadcasted_iota(jnp.int32, (GLA_SUB, GLA_SUB), 1)
    outs = []
    for blk in range(c // GLA_SUB):
        r0, r1 = blk * GLA_SUB, (blk + 1) * GLA_SUB
        bi, qi, ki = b[r0:r1], q[r0:r1], k[r0:r1]
        att = jnp.zeros((GLA_SUB, GLA_SUB), F32)
        for s in range(GLA_SUB):
            dec = jnp.exp(jnp.where(row >= s, bi - bi[s:s + 1, :], NEG))
            col = jnp.sum(qi * dec * ki[s:s + 1, :], axis=1, keepdims=True)
            att = jnp.where(lane == s, col, att)
        oi = jnp.dot(_bf(att), vb[r0:r1], preferred_element_type=F32)
        if blk > 0:
            bref = b[r0 - 1:r0, :]
            qt = qi * jnp.exp(bi - bref)
            kt = k[0:r0] * jnp.exp(bref - b[0:r0])
            a_off = lax.dot_general(_bf(qt), _bf(kt), _NT, preferred_element_type=F32)
            oi = oi + jnp.dot(_bf(a_off), vb[0:r0], preferred_element_type=F32)
        outs.append(oi)
    o = (outs[0] if len(outs) == 1 else jnp.concatenate(outs, axis=0)) + o_inter
    o = o * lax.rsqrt(jnp.mean(o * o, axis=-1, keepdims=True) + RMS_EPS) * nw_ref[...]
    y_ref[...] = (o * _silu(g_ref[...])).astype(y_ref.dtype)

    bl = b[c - 1:c, :]
    kh = k * jnp.exp(bl - b)
    new = st * jnp.exp(bl) + lax.dot_general(vb, _bf(kh), _TN, preferred_element_type=F32)
    st_ref[...] = new

    @pl.when(t == pl.num_programs(2) - 1)
    def _():
        so_ref[...] = new


def _gla(proj, lr, n, l, heads, dk, dv, w_lr, b_lr, norm_w, layer, s0t):
    c = min(CHUNK, l)
    nc = l // c
    qb, kb, vb_, gb_ = 0, heads, (2 * heads * dk) // dv, (2 * heads * dk) // dv + heads
    rank_pad = lr.shape[1]
    y, so = pl.pallas_call(
        _gla_body, grid=(n, heads, nc),
        in_specs=[
            pl.BlockSpec((c, dk), lambda b, h, t: (b * nc + t, qb + h)),
            pl.BlockSpec((c, dk), lambda b, h, t: (b * nc + t, kb + h)),
            pl.BlockSpec((c, dv), lambda b, h, t: (b * nc + t, vb_ + h)),
            pl.BlockSpec((c, dv), lambda b, h, t: (b * nc + t, gb_ + h)),
            pl.BlockSpec((c, rank_pad), lambda b, h, t: (b * nc + t, 0)),
            pl.BlockSpec((None, rank_pad, dk), lambda b, h, t: (layer, 0, h)),
            pl.BlockSpec((None, 1, dk), lambda b, h, t: (layer, 0, h)),
            pl.BlockSpec((None, 1, dv), lambda b, h, t: (layer, 0, 0)),
            pl.BlockSpec((None, dv, dk), lambda b, h, t: (b * heads + h, 0, 0)),
        ],
        out_specs=[
            pl.BlockSpec((c, dv), lambda b, h, t: (b * nc + t, h)),
            pl.BlockSpec((None, dv, dk), lambda b, h, t: (b * heads + h, 0, 0)),
        ],
        out_shape=[jax.ShapeDtypeStruct((n * l, heads * dv), BF16),
                   jax.ShapeDtypeStruct((n * heads, dv, dk), F32)],
        scratch_shapes=[pltpu.VMEM((dv, dk), F32)],
        compiler_params=_params("parallel", "parallel", "arbitrary"), name="gla",
    )(proj, proj, proj, proj, lr, w_lr, b_lr, norm_w, s0t)
    return y, so


def _band_body(q_ref, k_ref, v_ref, bias_ref, o_ref, *, window, past):
    tq, d = q_ref.shape
    q = _bf(q_ref[...])
    if past is None:
        kw, vw = _bf(k_ref[...]), _bf(v_ref[...])
    else:
        qs = pl.program_id(2) * tq
        ks, vs = [], []
        for blk in range(window // tq):
            start = pl.multiple_of(jnp.maximum(qs - past + blk * tq, 0), tq)
            ks.append(_bf(k_ref[pl.ds(start, tq), :]))
            vs.append(_bf(v_ref[pl.ds(start, tq), :]))
        kw, vw = jnp.concatenate(ks, axis=0), jnp.concatenate(vs, axis=0)
    s = lax.dot_general(q, kw, _NT, preferred_element_type=F32) * (d ** -0.5) + bias_ref[...]
    if past is not None:
        kl = lax.broadcasted_iota(jnp.int32, (tq, window), 1)
        s = jnp.where(kl >= past - qs, s, NEG)
    p = jnp.exp(s - jnp.max(s, axis=-1, keepdims=True))
    o = jnp.dot(p.astype(BF16), vw, preferred_element_type=F32) / jnp.sum(p, axis=-1, keepdims=True)
    o_ref[...] = o.astype(o_ref.dtype)


def _toeplitz_bias(table, tq, window, offset, static_mask=None):
    rows, heads = table.shape
    clip = (rows - 1) // 2
    period = tq + window
    m = np.arange(period)
    m = np.where(m >= window, m - period, m)
    idx = np.clip(offset - m, -clip, clip) + clip
    vec = table.astype(F32)[idx, :].T
    flat = jnp.tile(vec, (1, tq))[:, :tq * (period - 1)]
    bias = flat.reshape(heads, tq, period - 1)[:, :, :window]
    if static_mask is not None:
        bias = jnp.where(static_mask[None], bias, NEG)
    return bias


def _band(q_arr, qcol, k_arr, v_arr, kcol, vcol, n, l, heads, hd, bias, tq, window, past, out_rows):
    nq = l // tq
    lk = k_arr.shape[0] // n
    return pl.pallas_call(
        functools.partial(_band_body, window=window, past=past), grid=(n, heads, nq),
        in_specs=[
            pl.BlockSpec((tq, hd), lambda b, h, i: (b * nq + i, qcol + h)),
            pl.BlockSpec((lk, hd), lambda b, h, i: (b, kcol + h)),
            pl.BlockSpec((lk, hd), lambda b, h, i: (b, vcol + h)),
            pl.BlockSpec((None, tq, window), lambda b, h, i: (h, 0, 0)),
        ],
        out_specs=pl.BlockSpec((tq, hd), lambda b, h, i: (b * nq + i, h)),
        out_shape=jax.ShapeDtypeStruct((out_rows, heads * hd), BF16),
        compiler_params=_params("parallel", "parallel", "arbitrary"), name="band_attention",
    )(q_arr, k_arr, v_arr, bias)


def _trunk(x, n, l, mods, st, p, dims):
    m, d = x.shape
    depth = p['w_mod'].shape[0]
    alpha = (2 * depth) ** 0.25
    expanded = l % 256 != 0
    tm_big = m if expanded else _pick(l, (1024, 512, 256))
    tm_row = m if expanded else 256
    d_ff = p['ffn_w_gate'].shape[2]
    g_, p_, hg = dims['s5']
    mix_a = g_ * hg
    sb_heads, sb_hd = dims['sb']
    gla_heads, gla_dk, gla_dv = dims['gla']
    ca_heads, ca_hd = dims['ca']
    sb_w = sb_heads * sb_hd
    ca_w = ca_heads * ca_hd
    gla_w = gla_heads * gla_dv

    def mod_of(layer):
        if expanded:
            arr = jnp.repeat(mods[layer], l, axis=0)[None]
        else:
            arr = mods[layer][:, None, :]
        return _Mod(arr, d, l, expanded)

    outs = {k: [] for k in ('s5_re', 's5_im', 'sb_k', 'sb_v', 'gla', 'band_k', 'band_v')}
    mod = mod_of(0)
    h = _modulate(x, mod, tm_row)
    for layer in range(depth):
        i = layer // 2
        if layer % 2 == 0:
            in_ab = p['ab_w_in'].shape[2]
            proj, = _matmul(h, [(p['ab_w_in'], i, 0)], n=in_ab, tm=tm_big, tn=256, tk=d, out_dtypes=[F32],
                            epilogue=_epi_identity, name="ab_in")
            gb = g_ // S5_LANE_GROUPS
            sw = S5_LANE_GROUPS * p_
            x0 = jnp.concatenate([st['s5_re'][i].reshape(n * gb, 1, sw), st['s5_im'][i].reshape(n * gb, 1, sw)], axis=2)
            z, xo = _s5(proj, n, l, p['s5_tables'][i], p['s5_d'], i, x0, _pick(l, (256, 128, 64, 32, 16, 8)))
            outs['s5_re'].append(xo[:, 0, :sw].reshape(n, g_, p_))
            outs['s5_im'].append(xo[:, 0, sw:].reshape(n, g_, p_))
            tn_glu = 256
            ya, = _matmul(z, [(p['s5_w_glu'], i, 0)], n=mix_a, tm=tm_big, tn=tn_glu, tk=mix_a, out_dtypes=[BF16],
                          epilogue=_epi_glu, name="s5_glu",
                          extras=[(z, (tm_big, tn_glu), lambda a, b: (a, b)),
                                  (p['s5_b_glu'], (None, 1, tn_glu), lambda a, b, i=i: (i, 0, b))])
            qc, kc, vc = mix_a // sb_hd, (mix_a + sb_w) // sb_hd, (mix_a + 2 * sb_w) // sb_hd
            prev = None if st['sb_k'] is None else (st['sb_k'], st['sb_v'], i)
            yb = _stick_breaking(proj, n, l, sb_heads, sb_hd, qc, kc, vc, prev)
            outs['sb_k'].append(proj[:, mix_a + sb_w:mix_a + 2 * sb_w].reshape(n, l, sb_heads, sb_hd))
            outs['sb_v'].append(proj[:, mix_a + 2 * sb_w:mix_a + 3 * sb_w].reshape(n, l, sb_heads, sb_hd))
            merged = [ya, yb]
            w_out = p['ab_w_out']
        else:
            n_gla = 2 * gla_heads * gla_dk + 2 * gla_w
            proj, = _matmul(h, [(p['cd_w_in'], i, 0)], n=n_gla, tm=tm_big, tn=256, tk=d, out_dtypes=[F32],
                            epilogue=_epi_identity, name="cd_in_gla")
            projb, = _matmul(h, [(p['cd_w_band'][i], None, 0)], n=3 * ca_w, tm=tm_big, tn=256, tk=d,
                             out_dtypes=[F32], epilogue=_epi_identity, name="cd_in_band")
            lr, = _matmul(h, [(p['cd_w_lrin'][i], None, 0)], n=LANES, tm=tm_big, tn=LANES, tk=d, out_dtypes=[F32],
                          epilogue=_epi_identity, name="cd_lr")
            s0t = jnp.swapaxes(st['gla'][i], -1, -2).reshape(n * gla_heads, gla_dv, gla_dk)
            yc, so = _gla(proj, lr, n, l, gla_heads, gla_dk, gla_dv, p['gla_w_lr_pad'], p['gla_b_lr'],
                          p['gla_norm_w'], i, s0t)
            outs['gla'].append(jnp.swapaxes(so.reshape(n, gla_heads, gla_dv, gla_dk), -1, -2))
            ka = projb[:, ca_w:2 * ca_w]
            va = projb[:, 2 * ca_w:3 * ca_w]
            table = p['ca_rel_bias'][i]
            if st['band_k'] is None:
                tq = _pick(l, (256, 128, 64))
                window = tq + BAND_PAST
                qchunk = np.arange(tq)[:, None] // CHUNK
                kchunk = np.arange(window)[None, :] // CHUNK
                static_mask = (kchunk >= qchunk) & (kchunk <= qchunk + BAND_PREV)
                bias = _toeplitz_bias(table, tq, window, BAND_PAST, static_mask)
                yd = _band(projb, 0, projb, projb, ca_heads, 2 * ca_heads, n, l, ca_heads, ca_hd, bias, tq, window,
                           BAND_PAST, m)
                keep = min(BAND_PAST, l)
                outs['band_k'].append(ka.reshape(n, l, ca_heads, ca_hd)[:, l - keep:])
                outs['band_v'].append(va.reshape(n, l, ca_heads, ca_hd)[:, l - keep:])
            else:
                r = st['band_k'].shape[2]
                kcat = jnp.concatenate([st['band_k'][i].reshape(n, r, ca_w), ka.reshape(n, l, ca_w)], axis=1)
                vcat = jnp.concatenate([st['band_v'][i].reshape(n, r, ca_w), va.reshape(n, l, ca_w)], axis=1)
                bias = _toeplitz_bias(table, l, r + l, r)
                yd = _band(projb, 0, kcat.reshape(n * (r + l), ca_w), vcat.reshape(n * (r + l), ca_w), 0, 0,
                           n, l, ca_heads, ca_hd, bias, l, r + l, None, m)
                outs['band_k'].append(ka.reshape(n, l, ca_heads, ca_hd))
                outs['band_v'].append(va.reshape(n, l, ca_heads, ca_hd))
            merged = [yc, yd]
            w_out = p['cd_w_out']
        f, = _matmul(merged, [(w_out, i, 0)], n=d, tm=tm_big, tn=256, tk=sum(y.shape[1] for y in merged),
                     out_dtypes=[F32], epilogue=_epi_identity, name="mix_out")
        x, h = _deepnorm(x, f, mod, 2, p['ln1_w'], p['ln1_b'], layer, alpha, tm_row, nxt=(mod, 4, 3))
        tn_ff = _pick(d_ff, (256, 128))
        act, = _matmul(h, [(p['ffn_w_gate'], layer, 0), (p['ffn_w_up'], layer, 0)], n=d_ff, tm=tm_big, tn=tn_ff,
                       tk=d, out_dtypes=[BF16], epilogue=_epi_swiglu, name="ffn_in")
        ff, = _matmul(act, [(p['ffn_w_down'], layer, 0)], n=d, tm=min(tm_big, 512), tn=256, tk=d_ff,
                      out_dtypes=[F32], epilogue=_epi_identity, name="ffn_out")
        if layer + 1 < depth:
            nmod = mod_of(layer + 1)
            x, h = _deepnorm(x, ff, mod, 5, p['ln2_w'], p['ln2_b'], layer, alpha, tm_row, nxt=(nmod, 1, 0))
            mod = nmod
        else:
            x, _ = _deepnorm(x, ff, mod, 5, p['ln2_w'], p['ln2_b'], layer, alpha, tm_row)
    return x, {k: jnp.stack(v) for k, v in outs.items()}


def kernel(x_prompt, x_sample, state_s5_re, state_s5_im, cache_sb_k, cache_sb_v, state_gla, cache_band_k, cache_band_v, c_prompt, c_sample, w_mod, b_mod, ln1_w, ln1_b, ln2_w, ln2_b, ab_w_in, ab_w_out, s5_a_re, s5_a_im, s5_log_dt, s5_b_re, s5_b_im, s5_c_re, s5_c_im, s5_d, s5_w_glu, s5_b_glu, cd_w_in, cd_w_out, gla_w_lr, gla_b_lr, gla_norm_w, ca_rel_bias, ffn_w_gate, ffn_w_up, ffn_w_down):
    nb, seq, d = x_prompt.shape
    db, dseq, _ = x_sample.shape
    depth = w_mod.shape[0]
    n_even, g_, p_ = s5_a_re.shape
    hg = s5_b_re.shape[-1]
    _, _, past, sb_heads, sb_hd = cache_sb_k.shape
    n_odd, _, gla_heads, gla_dk, gla_dv = state_gla.shape
    _, _, band_rows, ca_heads, ca_hd = cache_band_k.shape
    rank = gla_w_lr.shape[1]
    dims = {'s5': (g_, p_, hg), 'sb': (sb_heads, sb_hd), 'gla': (gla_heads, gla_dk, gla_dv), 'ca': (ca_heads, ca_hd)}

    lr0 = 2 * gla_heads * gla_dk + 2 * gla_heads * gla_dv
    p = dict(
        w_mod=w_mod, ln1_w=ln1_w[:, None, :], ln1_b=ln1_b[:, None, :], ln2_w=ln2_w[:, None, :], ln2_b=ln2_b[:, None, :],
        ab_w_in=ab_w_in, ab_w_out=ab_w_out, s5_d=s5_d[:, None, :], s5_w_glu=s5_w_glu, s5_b_glu=s5_b_glu[:, None, :],
        cd_w_out=cd_w_out, gla_b_lr=gla_b_lr[:, None, :], gla_norm_w=gla_norm_w[:, None, :], ca_rel_bias=ca_rel_bias,
        ffn_w_gate=ffn_w_gate, ffn_w_up=ffn_w_up, ffn_w_down=ffn_w_down.astype(BF16), cd_w_in=cd_w_in,
        cd_w_band=[cd_w_in[i, :, lr0 + rank:].astype(BF16) for i in range(n_odd)],
        cd_w_lrin=[jnp.pad(cd_w_in[i, :, lr0:lr0 + rank], ((0, 0), (0, LANES - rank))).astype(BF16)
                   for i in range(n_odd)],
        gla_w_lr_pad=jnp.pad(gla_w_lr, ((0, 0), (0, LANES - rank), (0, 0))),
        s5_tables=[_s5_tables(s5_a_re[i], s5_a_im[i], s5_log_dt[i], s5_b_re[i], s5_b_im[i], s5_c_re[i], s5_c_im[i])
                   for i in range(n_even)],
    )

    rows = nb + db
    rows_pad = -(-rows // 16) * 16
    c_all = jnp.concatenate([c_prompt, c_sample, jnp.zeros((rows_pad - rows, d), F32)], axis=0)
    mods = []
    for layer in range(depth):
        tn_mod = _pick(6 * d, (512, 256, 128))
        mod_l, = _matmul(c_all, [(w_mod, layer, 0)], n=6 * d, tm=rows_pad, tn=tn_mod, tk=d, out_dtypes=[F32],
                         epilogue=_epi_bias, prologue=_silu, name="adaln_mod",
                         extras=[(b_mod[:, None, :], (None, 1, tn_mod), lambda a, b, layer=layer: (layer, 0, b))])
        mods.append(mod_l)
    mods = jnp.stack(mods)
    mods_p, mods_s = mods[:, :nb], mods[:, nb:rows]

    zeros_s5 = jnp.zeros((n_even, nb, g_, p_), F32)
    st_p = dict(s5_re=zeros_s5, s5_im=zeros_s5, sb_k=None, sb_v=None,
                gla=jnp.zeros((n_odd, nb, gla_heads, gla_dk, gla_dv), F32), band_k=None, band_v=None)
    y_p, o_p = _trunk(x_prompt.reshape(nb * seq, d), nb, seq, mods_p, st_p, p, dims)
    st_s = dict(s5_re=state_s5_re, s5_im=state_s5_im,
                sb_k=cache_sb_k.reshape(n_even, db * past, sb_heads * sb_hd),
                sb_v=cache_sb_v.reshape(n_even, db * past, sb_heads * sb_hd),
                gla=state_gla, band_k=cache_band_k, band_v=cache_band_v)
    y_s, o_s = _trunk(x_sample.reshape(db * dseq, d), db, dseq, mods_s, st_s, p, dims)
    return (y_p.reshape(nb, seq, d), y_s.reshape(db, dseq, d),
            o_p['s5_re'], o_p['s5_im'], o_p['sb_k'], o_p['sb_v'], o_p['gla'], o_p['band_k'], o_p['band_v'],
            o_s['s5_re'], o_s['s5_im'], o_s['sb_k'], o_s['sb_v'], o_s['gla'], o_s['band_k'], o_s['band_v'])
```

```python
import functools
import math

import numpy as np
import jax
import jax.numpy as jnp
from jax import lax
from jax.experimental import pallas as pl
from jax.experimental.pallas import tpu as pltpu

F32 = jnp.float32
BF16 = jnp.bfloat16

CHUNK = 64
BAND_PREV = 8
BAND_PAST = BAND_PREV * CHUNK
GLA_GATE_NORM = 16.0
LN_EPS = 1e-5
RMS_EPS = 1e-6

LANES = 128
SUBLANES = 8
V7X_VMEM_BYTES = 64 * 1024 * 1024
VMEM_LIMIT = V7X_VMEM_BYTES - 8 * 1024 * 1024

GLA_SUB = 16
S5_LANE_GROUPS = 8
NEG = -1e30
SB_EXIT = -104.0

_NT = (((1,), (1,)), ((), ()))
_TN = (((0,), (0,)), ((), ()))


def _pick(dim, candidates):
    for c in candidates:
        if c <= dim and dim % c == 0:
            return c
    return dim


def _params(*sem):
    return pltpu.CompilerParams(dimension_semantics=sem, vmem_limit_bytes=VMEM_LIMIT)


def _bf(x):
    return x if x.dtype == BF16 else x.astype(BF16)


def _log_sigmoid(z):
    return jnp.minimum(z, 0.0) - jnp.log1p(jnp.exp(-jnp.abs(z)))


def _silu(x):
    return x * jax.nn.sigmoid(x)


def _mm_body(*refs, na, nb, ne, no, nk, prologue, epilogue):
    a_refs, refs = refs[:na], refs[na:]
    b_refs = refs[0:nb]
    e_refs = refs[nb:nb + ne]
    o_refs = refs[nb + ne:nb + ne + no]
    acc_refs = refs[nb + ne + no:]
    a_parts = [a_ref[...] for a_ref in a_refs]
    if prologue is not None:
        a_parts = [prologue(a) for a in a_parts]
    a_parts = [_bf(a) for a in a_parts]
    a = a_parts[0] if na == 1 else jnp.concatenate(a_parts, axis=1)
    parts = [jnp.dot(a, _bf(b_ref[...]), preferred_element_type=F32) for b_ref in b_refs]

    def finish(accs):
        outs = epilogue(accs, [e[...] for e in e_refs])
        for o_ref, o in zip(o_refs, outs):
            o_ref[...] = o.astype(o_ref.dtype)

    if nk == 1:
        finish(parts)
        return
    k = pl.program_id(2)

    @pl.when(k == 0)
    def _():
        for acc, p in zip(acc_refs, parts):
            acc[...] = p

    @pl.when(k > 0)
    def _():
        for acc, p in zip(acc_refs, parts):
            acc[...] += p

    @pl.when(k == nk - 1)
    def _():
        finish([acc[...] for acc in acc_refs])


def _matmul(a, bs, *, n, tm, tn, tk, out_dtypes, epilogue, extras=(), prologue=None, name="matmul"):
    a_list = list(a) if isinstance(a, (list, tuple)) else [a]
    m = a_list[0].shape[0]
    kdim = sum(x.shape[1] for x in a_list)
    nk = kdim // tk
    assert len(a_list) == 1 or nk == 1
    grid = (m // tm, n // tn, nk)
    if len(a_list) == 1:
        in_specs = [pl.BlockSpec((tm, tk), lambda i, j, k: (i, k))]
    else:
        in_specs = [pl.BlockSpec((tm, x.shape[1]), lambda i, j, k: (i, 0)) for x in a_list]
    operands = list(a_list)
    for arr, layer, c0 in bs:
        if arr.ndim == 3:
            in_specs.append(pl.BlockSpec((None, tk, tn), lambda i, j, k, layer=layer, c0=c0: (layer, k, c0 + j)))
        else:
            in_specs.append(pl.BlockSpec((tk, tn), lambda i, j, k, c0=c0: (k, c0 + j)))
        operands.append(arr)
    for arr, shape, imap in extras:
        in_specs.append(pl.BlockSpec(shape, lambda i, j, k, imap=imap: imap(i, j)))
        operands.append(arr)
    out_shape = [jax.ShapeDtypeStruct((m, n), dt) for dt in out_dtypes]
    out_specs = [pl.BlockSpec((tm, tn), lambda i, j, k: (i, j)) for _ in out_dtypes]
    scratch = [pltpu.VMEM((tm, tn), F32) for _ in bs] if nk > 1 else []
    body = functools.partial(_mm_body, na=len(a_list), nb=len(bs), ne=len(extras), no=len(out_dtypes), nk=nk,
                             prologue=prologue, epilogue=epilogue)
    outs = pl.pallas_call(
        body, grid=grid, in_specs=in_specs, out_specs=out_specs, out_shape=out_shape,
        scratch_shapes=scratch, compiler_params=_params("parallel", "parallel", "arbitrary"), name=name,
    )(*operands)
    return outs


def _epi_identity(accs, extras):
    return [accs[0]]


def _epi_bias(accs, extras):
    return [accs[0] + extras[0]]


def _epi_swiglu(accs, extras):
    return [_silu(accs[0]) * accs[1]]


def _epi_glu(accs, extras):
    z, bias = extras
    return [z * jax.nn.sigmoid(accs[0] + bias)]


class _Mod:
    def __init__(self, arr, d, l, expanded):
        self.arr, self.d, self.l, self.expanded = arr, d, l, expanded

    def spec(self, piece, tm):
        d = self.d
        if self.expanded:
            return pl.BlockSpec((None, tm, d), lambda i, *_: (0, i, piece))
        per = self.l // tm
        return pl.BlockSpec((None, 1, d), lambda i, *_: (i // per, 0, piece))


def _modulate_body(x_ref, sc_ref, sh_ref, h_ref):
    h_ref[...] = (x_ref[...] * (1.0 + sc_ref[...]) + sh_ref[...]).astype(h_ref.dtype)


def _modulate(x, mod, tm):
    m, d = x.shape
    row = pl.BlockSpec((tm, d), lambda i: (i, 0))
    return pl.pallas_call(
        _modulate_body, grid=(m // tm,), in_specs=[row, mod.spec(1, tm), mod.spec(0, tm)],
        out_specs=row, out_shape=jax.ShapeDtypeStruct((m, d), BF16),
        compiler_params=_params("parallel"), name="modulate",
    )(x, mod.arr, mod.arr)


def _ln_body(*refs, alpha, has_next):
    if has_next:
        x_ref, f_ref, g_ref, w_ref, b_ref, sc_ref, sh_ref, xo_ref, ho_ref = refs
    else:
        x_ref, f_ref, g_ref, w_ref, b_ref, xo_ref = refs
    y = alpha * x_ref[...] + (1.0 + g_ref[...]) * f_ref[...]
    yc = y - jnp.mean(y, axis=-1, keepdims=True)
    var = jnp.mean(yc * yc, axis=-1, keepdims=True)
    xn = yc * lax.rsqrt(var + LN_EPS) * w_ref[...] + b_ref[...]
    xo_ref[...] = xn
    if has_next:
        ho_ref[...] = (xn * (1.0 + sc_ref[...]) + sh_ref[...]).astype(ho_ref.dtype)


def _deepnorm(x, f, mod, gate_piece, w, b, layer, alpha, tm, nxt=None):
    m, d = x.shape
    row = pl.BlockSpec((tm, d), lambda i: (i, 0))
    vec = pl.BlockSpec((None, 1, d), lambda i: (layer, 0, 0))
    in_specs = [row, row, mod.spec(gate_piece, tm), vec, vec]
    operands = [x, f, mod.arr, w, b]
    out_shape = [jax.ShapeDtypeStruct((m, d), F32)]
    out_specs = [row]
    if nxt is not None:
        nmod, sc_piece, sh_piece = nxt
        in_specs += [nmod.spec(sc_piece, tm), nmod.spec(sh_piece, tm)]
        operands += [nmod.arr, nmod.arr]
        out_shape.append(jax.ShapeDtypeStruct((m, d), BF16))
        out_specs.append(row)
    outs = pl.pallas_call(
        functools.partial(_ln_body, alpha=alpha, has_next=nxt is not None),
        grid=(m // tm,), in_specs=in_specs, out_specs=out_specs, out_shape=out_shape,
        compiler_params=_params("parallel"), name="deepnorm",
    )(*operands)
    return outs if nxt is not None else (outs[0], None)


def _s5_body(u_ref, bt_ref, ct_ref, sc_ref, d_ref, x0_ref, z_ref, xo_ref, xs_ref, st_ref, *, t_rows):
    t = pl.program_id(2)
    sw = st_ref.shape[-1] // 2

    @pl.when(t == 0)
    def _():
        st_ref[...] = x0_ref[...]

    u = u_ref[...]
    xs_ref[...] = jnp.dot(_bf(u), bt_ref[...], preferred_element_type=F32)

    def cmul_add(xr, xi, ar, ai, sr, si):
        return xr + ar * sr - ai * si, xi + ar * si + ai * sr

    def block(j, carry):
        cr, ci = carry
        off = pl.multiple_of(j * SUBLANES, SUBLANES)
        xr = xs_ref[pl.ds(off, SUBLANES), 0:sw]
        xi = xs_ref[pl.ds(off, SUBLANES), sw:2 * sw]
        for step, d in enumerate((1, 2, 4)):
            ar = sc_ref[16 * step:16 * step + 8, :]
            ai = sc_ref[16 * step + 8:16 * step + 16, :]
            xr, xi = cmul_add(xr, xi, ar, ai, pltpu.roll(xr, d, 0), pltpu.roll(xi, d, 0))
        xr, xi = cmul_add(xr, xi, sc_ref[48:56, :], sc_ref[56:64, :], cr, ci)
        xs_ref[pl.ds(off, SUBLANES), 0:sw] = xr
        xs_ref[pl.ds(off, SUBLANES), sw:2 * sw] = xi
        return xr[SUBLANES - 1:SUBLANES, :], xi[SUBLANES - 1:SUBLANES, :]

    st = st_ref[...]
    nblk = t_rows // SUBLANES
    cr, ci = lax.fori_loop(0, nblk, block, (st[:, 0:sw], st[:, sw:2 * sw]), unroll=2 if nblk % 2 == 0 else 1)
    st_ref[:, 0:sw] = cr
    st_ref[:, sw:2 * sw] = ci
    y = jnp.dot(_bf(xs_ref[...]), ct_ref[...], preferred_element_type=F32) + d_ref[...] * u
    z_ref[...] = 0.5 * y * (1.0 + jnp.tanh(math.sqrt(2.0 / math.pi) * (y + 0.044715 * (y * y * y))))

    @pl.when(t == pl.num_programs(2) - 1)
    def _():
        xo_ref[:, 0:sw] = cr
        xo_ref[:, sw:2 * sw] = ci


def _s5_tables(a_re, a_im, log_dt, b_re, b_im, c_re, c_im):
    g, p = a_re.shape
    hg = b_re.shape[-1]
    gb = g // S5_LANE_GROUPS
    lr_, li_ = a_re.astype(F32), a_im.astype(F32)
    dt = jnp.exp(log_dt.astype(F32))[:, None]
    mag = jnp.exp(lr_ * dt)
    ab_r, ab_i = mag * jnp.cos(li_ * dt), mag * jnp.sin(li_ * dt)
    den = lr_ * lr_ + li_ * li_
    cr_ = ((ab_r - 1.0) * lr_ + ab_i * li_) / den
    ci_ = (ab_i * lr_ - (ab_r - 1.0) * li_) / den
    br_, bi_ = b_re.astype(F32), b_im.astype(F32)
    bb_r = cr_[..., None] * br_ - ci_[..., None] * bi_
    bb_i = cr_[..., None] * bi_ + ci_[..., None] * br_
    eye = jnp.eye(S5_LANE_GROUPS, dtype=F32)

    def blockdiag_in(w):
        w = w.reshape(gb, S5_LANE_GROUPS, p, hg)
        return jnp.einsum('bgph,gk->bghkp', w, eye).reshape(gb, S5_LANE_GROUPS * hg, S5_LANE_GROUPS * p)

    def blockdiag_out(w):
        w = w.reshape(gb, S5_LANE_GROUPS, hg, p)
        return jnp.einsum('bgkp,gj->bgpjk', w, eye).reshape(gb, S5_LANE_GROUPS * p, S5_LANE_GROUPS * hg)

    bt = jnp.concatenate([blockdiag_in(bb_r), blockdiag_in(bb_i)], axis=2).astype(BF16)
    ct = jnp.concatenate([blockdiag_out(c_re.astype(F32)), blockdiag_out(-c_im.astype(F32))], axis=1).astype(BF16)
    sw = S5_LANE_GROUPS * p

    def rows8(z):
        return jnp.broadcast_to(z.reshape(gb, 1, sw), (gb, SUBLANES, sw))

    pows = [(ab_r, ab_i)]
    for _ in range(SUBLANES - 1):
        pr, pi = pows[-1]
        pows.append((pr * ab_r - pi * ab_i, pr * ab_i + pi * ab_r))
    pieces = []
    trow = jnp.arange(SUBLANES)[None, :, None]
    for d in (1, 2, 4):
        pieces += [jnp.where(trow >= d, rows8(pows[d - 1][part]), 0.0) for part in (0, 1)]
    for part in (0, 1):
        pieces.append(jnp.stack([pw[part] for pw in pows], axis=0).reshape(SUBLANES, gb, sw).transpose(1, 0, 2))
    sc = jnp.concatenate(pieces, axis=1)
    return bt, ct, sc


def _s5(proj, n, l, tables, d_skip, layer, x0, t_rows):
    bt, ct, sc = tables
    gb, cw, sw2 = bt.shape
    nt = l // t_rows
    grid = (n, gb, nt)
    z, xo = pl.pallas_call(
        functools.partial(_s5_body, t_rows=t_rows), grid=grid,
        in_specs=[
            pl.BlockSpec((t_rows, cw), lambda b, g, t: (b * nt + t, g)),
            pl.BlockSpec((None, cw, sw2), lambda b, g, t: (g, 0, 0)),
            pl.BlockSpec((None, sw2, cw), lambda b, g, t: (g, 0, 0)),
            pl.BlockSpec((None, 8 * SUBLANES, sw2 // 2), lambda b, g, t: (g, 0, 0)),
            pl.BlockSpec((None, 1, cw), lambda b, g, t: (layer, 0, g)),
            pl.BlockSpec((None, 1, sw2), lambda b, g, t: (b * gb + g, 0, 0)),
        ],
        out_specs=[
            pl.BlockSpec((t_rows, cw), lambda b, g, t: (b * nt + t, g)),
            pl.BlockSpec((None, 1, sw2), lambda b, g, t: (b * gb + g, 0, 0)),
        ],
        out_shape=[jax.ShapeDtypeStruct((n * l, gb * cw), F32),
                   jax.ShapeDtypeStruct((n * gb, 1, sw2), F32)],
        scratch_shapes=[pltpu.VMEM((t_rows, sw2), F32), pltpu.VMEM((1, sw2), F32)],
        compiler_params=_params("parallel", "parallel", "arbitrary"), name="s5_scan",
    )(proj, bt, ct, sc, d_skip, x0)
    return z, xo


def _strict_upper2(tk):
    rj = lax.broadcasted_iota(jnp.int32, (tk, tk), 0)
    cs = lax.broadcasted_iota(jnp.int32, (tk, tk), 1)
    upper = jnp.where(rj > cs, 1.0, 0.0).astype(BF16)
    return jnp.concatenate([upper, upper], axis=0)


def _sb_tile(q, kt, vt, carry, mask, scale, upper2):
    z = lax.dot_general(q, _bf(kt), _NT, preferred_element_type=F32) * scale
    ls = _log_sigmoid(z)
    lf = ls - z
    if mask is not None:
        lf = jnp.where(mask, lf, 0.0)
    hi = lf.astype(BF16)
    lo = (lf - hi.astype(F32)).astype(BF16)
    newer = jnp.dot(jnp.concatenate([hi, lo], axis=1), upper2, preferred_element_type=F32)
    w = jnp.exp(ls + carry + newer)
    if mask is not None:
        w = jnp.where(mask, w, 0.0)
    out = jnp.dot(w.astype(BF16), _bf(vt), preferred_element_type=F32)
    return out, carry + jnp.sum(lf, axis=1, keepdims=True)


def _sb_body(q_ref, kd_ref, vd_ref, kp_ref, vp_ref, o_ref, *, tk, prev_tiles, hb):
    tq = q_ref.shape[0]
    d = q_ref.shape[1] // hb
    scale = d ** -0.5
    r = lax.broadcasted_iota(jnp.int32, (tq, tq), 0)
    c = lax.broadcasted_iota(jnp.int32, (tq, tq), 1)
    upper2 = _strict_upper2(tk)
    upper2_diag = upper2 if tq == tk else _strict_upper2(tq)
    nprev = prev_tiles(pl.program_id(2))
    first_off = pl.multiple_of(jnp.maximum(nprev - 1, 0) * tk, tk)

    states = []
    for h in range(hb):
        cols = slice(h * d, (h + 1) * d)
        q = _bf(q_ref[:, cols])
        acc, carry = _sb_tile(q, kd_ref[:, cols], vd_ref[:, cols], jnp.zeros((tq, 1), F32), c < r, scale,
                              upper2_diag)
        out, carry = _sb_tile(q, kp_ref[pl.ds(first_off, tk), cols], vp_ref[pl.ds(first_off, tk), cols], carry,
                              nprev > 0, scale, upper2)
        states.append((q, acc + out, carry))

    def cond(state):
        j, _, _, cmax = state
        return jnp.logical_and(j >= 0, cmax > SB_EXIT)

    for h, (q, acc, carry) in enumerate(states):
        cols = slice(h * d, (h + 1) * d)

        def body(state, q=q, cols=cols):
            j, acc, carry, _ = state
            off = pl.multiple_of(j * tk, tk)
            out, carry = _sb_tile(q, kp_ref[pl.ds(off, tk), cols], vp_ref[pl.ds(off, tk), cols], carry, None,
                                  scale, upper2)
            return j - 1, acc + out, carry, jnp.max(carry)

        first = jnp.asarray(nprev - 2, jnp.int32)
        _, acc, _, _ = lax.while_loop(cond, body, (first, acc, carry, jnp.max(carry)))
        o_ref[:, cols] = acc.astype(o_ref.dtype)


def _stick_breaking(proj, n, l, heads, hd, qcol, kcol, vcol, prev=None):
    if prev is None:
        hb = 2
        tq = _pick(l, (256, 128, 64, 32, 16, 8))
        tk = tq
        nq = l // tq
        kp, vp = proj, proj
        kp_spec = pl.BlockSpec((l, hb * hd), lambda b, h, i: (b, kcol // hb + h))
        vp_spec = pl.BlockSpec((l, hb * hd), lambda b, h, i: (b, vcol // hb + h))
        prev_tiles = lambda i: i
    else:
        hb = 8
        kc, vc, layer = prev
        past = kc.shape[1] // n
        tq, nq = l, 1
        tk = _pick(past, (256, 128, 64, 32, 16, 8))
        kp, vp = kc, vc
        kp_spec = pl.BlockSpec((None, past, hb * hd), lambda b, h, i: (layer, b, h))
        vp_spec = pl.BlockSpec((None, past, hb * hd), lambda b, h, i: (layer, b, h))
        prev_tiles = lambda i: past // tk
    assert heads % hb == 0 and qcol % hb == 0 and kcol % hb == 0 and vcol % hb == 0
    w = hb * hd
    return pl.pallas_call(
        functools.partial(_sb_body, tk=tk, prev_tiles=prev_tiles, hb=hb), grid=(n, heads // hb, nq),
        in_specs=[
            pl.BlockSpec((tq, w), lambda b, h, i: (b * nq + i, qcol // hb + h)),
            pl.BlockSpec((tq, w), lambda b, h, i: (b * nq + i, kcol // hb + h)),
            pl.BlockSpec((tq, w), lambda b, h, i: (b * nq + i, vcol // hb + h)),
            kp_spec, vp_spec,
        ],
        out_specs=pl.BlockSpec((tq, w), lambda b, h, i: (b * nq + i, h)),
        out_shape=jax.ShapeDtypeStruct((n * l, heads * hd), BF16),
        compiler_params=_params("parallel", "parallel", "arbitrary"), name="stick_breaking",
    )(proj, proj, proj, kp, vp)


def _gla_body(q_ref, k_ref, v_ref, g_ref, lr_ref, wlr_ref, blr_ref, nw_ref, s0_ref, y_ref, so_ref, st_ref, *, heads):
    t = pl.program_id(1)
    c = q_ref.shape[0]

    @pl.when(t == 0)
    def _():
        st_ref[...] = s0_ref[...]

    gate_in = jnp.dot(_bf(lr_ref[...]), _bf(wlr_ref[...]), preferred_element_type=F32) + blr_ref[...]
    logg = _log_sigmoid(gate_in) * (1.0 / GLA_GATE_NORM)
    rt = lax.broadcasted_iota(jnp.int32, (c, c), 0)
    cs = lax.broadcasted_iota(jnp.int32, (c, c), 1)
    lower = jnp.where(cs <= rt, 1.0, 0.0).astype(BF16)
    hi = logg.astype(BF16)
    lo = (logg - hi.astype(F32)).astype(BF16)
    b_all = jnp.dot(lower, hi, preferred_element_type=F32) + jnp.dot(lower, lo, preferred_element_type=F32)
    last = t == pl.num_programs(1) - 1
    for hh in range(heads):
        _gla_head(hh, b_all, q_ref, k_ref, v_ref, g_ref, nw_ref, y_ref, so_ref, st_ref, last, heads)


def _gla_head(hh, b_all, q_ref, k_ref, v_ref, g_ref, nw_ref, y_ref, so_ref, st_ref, last, heads):
    c = q_ref.shape[0]
    dk = q_ref.shape[1] // heads
    dv = v_ref.shape[1] // heads
    q = q_ref[:, hh * dk:(hh + 1) * dk] * (dk ** -0.5)
    k = k_ref[:, hh * dk:(hh + 1) * dk]
    vb = _bf(v_ref[:, hh * dv:(hh + 1) * dv])
    b = b_all[:, hh * dk:(hh + 1) * dk]
    st = st_ref[hh]
    o_inter = lax.dot_general(_bf(q * jnp.exp(b)), _bf(st), _NT, preferred_element_type=F32)
    row = lax.broadcasted_iota(jnp.int32, (GLA_SUB, 1), 0)
    lane = lax.broadcasted_iota(jnp.int32, (GLA_SUB, GLA_SUB), 1)
    outs = []
    for blk in range(c // GLA_SUB):
        r0, r1 = blk * GLA_SUB, (blk + 1) * GLA_SUB
        bi, qi, ki = b[r0:r1], q[r0:r1], k[r0:r1]
        att = jnp.zeros((GLA_SUB, GLA_SUB), F32)
        for s in range(GLA_SUB):
            dec = jnp.exp(jnp.where(row >= s, bi - bi[s:s + 1, :], NEG))
            col = jnp.sum(qi * dec * ki[s:s + 1, :], axis=1, keepdims=True)
            att = jnp.where(lane == s, col, att)
        oi = jnp.dot(_bf(att), vb[r0:r1], preferred_element_type=F32)
        if blk > 0:
            bref = b[r0 - 1:r0, :]
            qt = qi * jnp.exp(bi - bref)
            kt = k[0:r0] * jnp.exp(bref - b[0:r0])
            a_off = lax.dot_general(_bf(qt), _bf(kt), _NT, preferred_element_type=F32)
            oi = oi + jnp.dot(_bf(a_off), vb[0:r0], preferred_element_type=F32)
        outs.append(oi)
    o = (outs[0] if len(outs) == 1 else jnp.concatenate(outs, axis=0)) + o_inter
    o = o * lax.rsqrt(jnp.mean(o * o, axis=-1, keepdims=True) + RMS_EPS) * nw_ref[...]
    y_ref[:, hh * dv:(hh + 1) * dv] = (o * _silu(g_ref[:, hh * dv:(hh + 1) * dv])).astype(y_ref.dtype)

    bl = b[c - 1:c, :]
    kh = k * jnp.exp(bl - b)
    new = st * jnp.exp(bl) + lax.dot_general(vb, _bf(kh), _TN, preferred_element_type=F32)
    st_ref[hh] = new

    @pl.when(last)
    def _():
        so_ref[hh] = new


def _gla(proj, lr, n, l, heads, dk, dv, w_lr, b_lr, norm_w, layer, s0t):
    c = min(CHUNK, l)
    nc = l // c
    wk, wv = heads * dk, heads * dv
    assert (2 * wk) % wv == 0
    v_blk = (2 * wk) // wv
    rank_pad = lr.shape[1]
    y, so = pl.pallas_call(
        functools.partial(_gla_body, heads=heads), grid=(n, nc),
        in_specs=[
            pl.BlockSpec((c, wk), lambda b, t: (b * nc + t, 0)),
            pl.BlockSpec((c, wk), lambda b, t: (b * nc + t, 1)),
            pl.BlockSpec((c, wv), lambda b, t: (b * nc + t, v_blk)),
            pl.BlockSpec((c, wv), lambda b, t: (b * nc + t, v_blk + 1)),
            pl.BlockSpec((c, rank_pad), lambda b, t: (b * nc + t, 0)),
            pl.BlockSpec((None, rank_pad, wk), lambda b, t: (layer, 0, 0)),
            pl.BlockSpec((None, 1, wk), lambda b, t: (layer, 0, 0)),
            pl.BlockSpec((None, 1, dv), lambda b, t: (layer, 0, 0)),
            pl.BlockSpec((heads, dv, dk), lambda b, t: (b, 0, 0)),
        ],
        out_specs=[
            pl.BlockSpec((c, wv), lambda b, t: (b * nc + t, 0)),
            pl.BlockSpec((heads, dv, dk), lambda b, t: (b, 0, 0)),
        ],
        out_shape=[jax.ShapeDtypeStruct((n * l, wv), BF16),
                   jax.ShapeDtypeStruct((n * heads, dv, dk), F32)],
        scratch_shapes=[pltpu.VMEM((heads, dv, dk), F32)],
        compiler_params=_params("parallel", "arbitrary"), name="gla",
    )(proj, proj, proj, proj, lr, w_lr, b_lr, norm_w, s0t)
    return y, so


def _band_body(q_ref, k_ref, v_ref, bias_ref, o_ref, *, window, past, hb):
    tq = q_ref.shape[0]
    d = q_ref.shape[1] // hb
    if past is not None:
        qs = pl.program_id(2) * tq
        starts = [pl.multiple_of(jnp.maximum(qs - past + blk * tq, 0), tq) for blk in range(window // tq)]
        exists = lax.broadcasted_iota(jnp.int32, (tq, window), 1) >= past - qs
    for h in range(hb):
        cols = slice(h * d, (h + 1) * d)
        q = _bf(q_ref[:, cols])
        if past is None:
            kw, vw = _bf(k_ref[:, cols]), _bf(v_ref[:, cols])
        else:
            kw = jnp.concatenate([_bf(k_ref[pl.ds(st, tq), cols]) for st in starts], axis=0)
            vw = jnp.concatenate([_bf(v_ref[pl.ds(st, tq), cols]) for st in starts], axis=0)
        s = lax.dot_general(q, kw, _NT, preferred_element_type=F32) * (d ** -0.5) + bias_ref[h]
        if past is not None:
            s = jnp.where(exists, s, NEG)
        p = jnp.exp(s - jnp.max(s, axis=-1, keepdims=True))
        o = jnp.dot(p.astype(BF16), vw, preferred_element_type=F32) / jnp.sum(p, axis=-1, keepdims=True)
        o_ref[:, cols] = o.astype(o_ref.dtype)


def _toeplitz_bias(table, tq, window, offset, static_mask=None):
    rows, heads = table.shape
    clip = (rows - 1) // 2
    period = tq + window
    m = np.arange(period)
    m = np.where(m >= window, m - period, m)
    idx = np.clip(offset - m, -clip, clip) + clip
    vec = table.astype(F32)[idx, :].T
    flat = jnp.tile(vec, (1, tq))[:, :tq * (period - 1)]
    bias = flat.reshape(heads, tq, period - 1)[:, :, :window]
    if static_mask is not None:
        bias = jnp.where(static_mask[None], bias, NEG)
    return bias


def _band(q_arr, qcol, k_arr, v_arr, kcol, vcol, n, l, heads, hd, bias, tq, window, past, out_rows):
    nq = l // tq
    lk = k_arr.shape[0] // n
    hb = 2 if past is not None else 8
    assert heads % hb == 0 and qcol % hb == 0 and kcol % hb == 0 and vcol % hb == 0
    w = hb * hd
    return pl.pallas_call(
        functools.partial(_band_body, window=window, past=past, hb=hb), grid=(n, heads // hb, nq),
        in_specs=[
            pl.BlockSpec((tq, w), lambda b, h, i: (b * nq + i, qcol // hb + h)),
            pl.BlockSpec((lk, w), lambda b, h, i: (b, kcol // hb + h)),
            pl.BlockSpec((lk, w), lambda b, h, i: (b, vcol // hb + h)),
            pl.BlockSpec((hb, tq, window), lambda b, h, i: (h, 0, 0)),
        ],
        out_specs=pl.BlockSpec((tq, w), lambda b, h, i: (b * nq + i, h)),
        out_shape=jax.ShapeDtypeStruct((out_rows, heads * hd), BF16),
        compiler_params=_params("parallel", "parallel", "arbitrary"), name="band_attention",
    )(q_arr, k_arr, v_arr, bias)


def _kv_rows_body(*refs, nl, heads, hd):
    k_refs, v_refs = refs[0:nl], refs[nl:2 * nl]
    ko_ref, vo_ref = refs[2 * nl], refs[2 * nl + 1]
    layer = pl.program_id(0)
    for li in range(nl):
        @pl.when(layer == li)
        def _(li=li):
            for h in range(heads):
                ko_ref[:, h, :] = k_refs[li][:, h * hd:(h + 1) * hd]
                vo_ref[:, h, :] = v_refs[li][:, h * hd:(h + 1) * hd]


def _kv_rows(projs, kblk, vblk, heads, hd):
    nl = len(projs)
    m = projs[0].shape[0]
    w = heads * hd
    tm = _pick(m, (256, 128, 64, 32, 16, 8))
    nt = m // tm

    def in_spec(li, blk):
        return pl.BlockSpec((tm, w), lambda layer, i: (jnp.clip(i + (layer - li) * nt, 0, nt - 1), blk))

    out_spec = pl.BlockSpec((None, tm, heads, hd), lambda layer, i: (layer, i, 0, 0))
    out_sds = jax.ShapeDtypeStruct((nl, m, heads, hd), F32)
    return pl.pallas_call(
        functools.partial(_kv_rows_body, nl=nl, heads=heads, hd=hd), grid=(nl, nt),
        in_specs=[in_spec(li, kblk) for li in range(nl)] + [in_spec(li, vblk) for li in range(nl)],
        out_specs=[out_spec, out_spec], out_shape=[out_sds, out_sds],
        compiler_params=_params("arbitrary", "arbitrary"), name="kv_rows",
    )(*projs, *projs)


def _trunk(x, n, l, mods, st, p, dims):
    m, d = x.shape
    depth = p['w_mod'].shape[0]
    alpha = (2 * depth) ** 0.25
    expanded = l % 256 != 0
    tm_big = m if expanded else _pick(l, (1024, 512, 256))
    tm_row = m if expanded else 256
    d_ff = p['ffn_w_gate'].shape[2]
    g_, p_, hg = dims['s5']
    mix_a = g_ * hg
    sb_heads, sb_hd = dims['sb']
    gla_heads, gla_dk, gla_dv = dims['gla']
    ca_heads, ca_hd = dims['ca']
    sb_w = sb_heads * sb_hd
    ca_w = ca_heads * ca_hd
    gla_w = gla_heads * gla_dv

    def mod_of(layer):
        if expanded:
            arr = jnp.repeat(mods[layer], l, axis=0)[None]
        else:
            arr = mods[layer][:, None, :]
        return _Mod(arr, d, l, expanded)

    outs = {k: [] for k in ('s5_re', 's5_im', 'gla', 'band_k', 'band_v')}
    projs_ab = []
    mod = mod_of(0)
    h = _modulate(x, mod, tm_row)
    for layer in range(depth):
        i = layer // 2
        if layer % 2 == 0:
            in_ab = p['ab_w_in'].shape[2]
            proj, = _matmul(h, [(p['ab_w_in'], i, 0)], n=in_ab, tm=tm_big, tn=256, tk=d, out_dtypes=[F32],
                            epilogue=_epi_identity, name="ab_in")
            gb = g_ // S5_LANE_GROUPS
            sw = S5_LANE_GROUPS * p_
            x0 = jnp.concatenate([st['s5_re'][i].reshape(n * gb, 1, sw), st['s5_im'][i].reshape(n * gb, 1, sw)], axis=2)
            z, xo = _s5(proj, n, l, p['s5_tables'][i], p['s5_d'], i, x0, _pick(l, (256, 128, 64, 32, 16, 8)))
            outs['s5_re'].append(xo[:, 0, :sw].reshape(n, g_, p_))
            outs['s5_im'].append(xo[:, 0, sw:].reshape(n, g_, p_))
            tn_glu = 256
            ya, = _matmul(z, [(p['s5_w_glu'], i, 0)], n=mix_a, tm=tm_big, tn=tn_glu, tk=mix_a, out_dtypes=[BF16],
                          epilogue=_epi_glu, name="s5_glu",
                          extras=[(z, (tm_big, tn_glu), lambda a, b: (a, b)),
                                  (p['s5_b_glu'], (None, 1, tn_glu), lambda a, b, i=i: (i, 0, b))])
            qc, kc, vc = mix_a // sb_hd, (mix_a + sb_w) // sb_hd, (mix_a + 2 * sb_w) // sb_hd
            prev = None if st['sb_k'] is None else (st['sb_k'], st['sb_v'], i)
            yb = _stick_breaking(proj, n, l, sb_heads, sb_hd, qc, kc, vc, prev)
            projs_ab.append(proj)
            merged = [ya, yb]
            w_out = p['ab_w_out']
        else:
            n_gla = 2 * gla_heads * gla_dk + 2 * gla_w
            proj, = _matmul(h, [(p['cd_w_in'], i, 0)], n=n_gla, tm=tm_big, tn=256, tk=d, out_dtypes=[F32],
                            epilogue=_epi_identity, name="cd_in_gla")
            projb, = _matmul(h, [(p['cd_w_band'][i], None, 0)], n=3 * ca_w, tm=tm_big, tn=256, tk=d,
                             out_dtypes=[F32], epilogue=_epi_identity, name="cd_in_band")
            lr, = _matmul(h, [(p['cd_w_lrin'][i], None, 0)], n=LANES, tm=tm_big, tn=LANES, tk=d, out_dtypes=[F32],
                          epilogue=_epi_identity, name="cd_lr")
            s0t = jnp.swapaxes(st['gla'][i], -1, -2).reshape(n * gla_heads, gla_dv, gla_dk)
            yc, so = _gla(proj, lr, n, l, gla_heads, gla_dk, gla_dv, p['gla_w_lr_pad'], p['gla_b_lr'],
                          p['gla_norm_w'], i, s0t)
            outs['gla'].append(jnp.swapaxes(so.reshape(n, gla_heads, gla_dv, gla_dk), -1, -2))
            ka = projb[:, ca_w:2 * ca_w]
            va = projb[:, 2 * ca_w:3 * ca_w]
            table = p['ca_rel_bias'][i]
            if st['band_k'] is None:
                tq = _pick(l, (256, 128, 64))
                window = tq + BAND_PAST
                qchunk = np.arange(tq)[:, None] // CHUNK
                kchunk = np.arange(window)[None, :] // CHUNK
                static_mask = (kchunk >= qchunk) & (kchunk <= qchunk + BAND_PREV)
                bias = _toeplitz_bias(table, tq, window, BAND_PAST, static_mask)
                yd = _band(projb, 0, projb, projb, ca_heads, 2 * ca_heads, n, l, ca_heads, ca_hd, bias, tq, window,
                           BAND_PAST, m)
                keep = min(BAND_PAST, l)
                outs['band_k'].append(ka.reshape(n, l, ca_heads, ca_hd)[:, l - keep:])
                outs['band_v'].append(va.reshape(n, l, ca_heads, ca_hd)[:, l - keep:])
            else:
                r = st['band_k'].shape[2]
                kcat = jnp.concatenate([st['band_k'][i].reshape(n, r, ca_w), ka.reshape(n, l, ca_w)], axis=1)
                vcat = jnp.concatenate([st['band_v'][i].reshape(n, r, ca_w), va.reshape(n, l, ca_w)], axis=1)
                bias = _toeplitz_bias(table, l, r + l, r)
                yd = _band(projb, 0, kcat.reshape(n * (r + l), ca_w), vcat.reshape(n * (r + l), ca_w), 0, 0,
                           n, l, ca_heads, ca_hd, bias, l, r + l, None, m)
                outs['band_k'].append(ka.reshape(n, l, ca_heads, ca_hd))
                outs['band_v'].append(va.reshape(n, l, ca_heads, ca_hd))
            merged = [yc, yd]
            w_out = p['cd_w_out']
        f, = _matmul(merged, [(w_out, i, 0)], n=d, tm=tm_big, tn=256, tk=sum(y.shape[1] for y in merged),
                     out_dtypes=[F32], epilogue=_epi_identity, name="mix_out")
        x, h = _deepnorm(x, f, mod, 2, p['ln1_w'], p['ln1_b'], layer, alpha, tm_row, nxt=(mod, 4, 3))
        tn_ff = _pick(d_ff, (256, 128))
        act, = _matmul(h, [(p['ffn_w_gate'], layer, 0), (p['ffn_w_up'], layer, 0)], n=d_ff, tm=tm_big, tn=tn_ff,
                       tk=d, out_dtypes=[BF16], epilogue=_epi_swiglu, name="ffn_in")
        ff, = _matmul(act, [(p['ffn_w_down'], layer, 0)], n=d, tm=min(tm_big, 512), tn=256, tk=d_ff,
                      out_dtypes=[F32], epilogue=_epi_identity, name="ffn_out")
        if layer + 1 < depth:
            nmod = mod_of(layer + 1)
            x, h = _deepnorm(x, ff, mod, 5, p['ln2_w'], p['ln2_b'], layer, alpha, tm_row, nxt=(nmod, 1, 0))
            mod = nmod
        else:
            x, _ = _deepnorm(x, ff, mod, 5, p['ln2_w'], p['ln2_b'], layer, alpha, tm_row)
    outs = {k: jnp.stack(v) for k, v in outs.items()}
    assert mix_a % sb_w == 0
    sb_k, sb_v = _kv_rows(projs_ab, mix_a // sb_w + 1, mix_a // sb_w + 2, sb_heads, sb_hd)
    outs['sb_k'] = sb_k.reshape(len(projs_ab), n, l, sb_heads, sb_hd)
    outs['sb_v'] = sb_v.reshape(len(projs_ab), n, l, sb_heads, sb_hd)
    return x, outs


def kernel(x_prompt, x_sample, state_s5_re, state_s5_im, cache_sb_k, cache_sb_v, state_gla, cache_band_k, cache_band_v, c_prompt, c_sample, w_mod, b_mod, ln1_w, ln1_b, ln2_w, ln2_b, ab_w_in, ab_w_out, s5_a_re, s5_a_im, s5_log_dt, s5_b_re, s5_b_im, s5_c_re, s5_c_im, s5_d, s5_w_glu, s5_b_glu, cd_w_in, cd_w_out, gla_w_lr, gla_b_lr, gla_norm_w, ca_rel_bias, ffn_w_gate, ffn_w_up, ffn_w_down):
    nb, seq, d = x_prompt.shape
    db, dseq, _ = x_sample.shape
    depth = w_mod.shape[0]
    n_even, g_, p_ = s5_a_re.shape
    hg = s5_b_re.shape[-1]
    _, _, past, sb_heads, sb_hd = cache_sb_k.shape
    n_odd, _, gla_heads, gla_dk, gla_dv = state_gla.shape
    _, _, band_rows, ca_heads, ca_hd = cache_band_k.shape
    rank = gla_w_lr.shape[1]
    dims = {'s5': (g_, p_, hg), 'sb': (sb_heads, sb_hd), 'gla': (gla_heads, gla_dk, gla_dv), 'ca': (ca_heads, ca_hd)}

    lr0 = 2 * gla_heads * gla_dk + 2 * gla_heads * gla_dv
    p = dict(
        w_mod=w_mod, ln1_w=ln1_w[:, None, :], ln1_b=ln1_b[:, None, :], ln2_w=ln2_w[:, None, :], ln2_b=ln2_b[:, None, :],
        ab_w_in=ab_w_in, ab_w_out=ab_w_out, s5_d=s5_d[:, None, :], s5_w_glu=s5_w_glu, s5_b_glu=s5_b_glu[:, None, :],
        cd_w_out=cd_w_out, gla_b_lr=gla_b_lr[:, None, :], gla_norm_w=gla_norm_w[:, None, :], ca_rel_bias=ca_rel_bias,
        ffn_w_gate=ffn_w_gate, ffn_w_up=ffn_w_up, ffn_w_down=ffn_w_down.astype(BF16), cd_w_in=cd_w_in,
        cd_w_band=[cd_w_in[i, :, lr0 + rank:] for i in range(n_odd)],
        cd_w_lrin=[jnp.pad(cd_w_in[i, :, lr0:lr0 + rank], ((0, 0), (0, LANES - rank))).astype(BF16)
                   for i in range(n_odd)],
        gla_w_lr_pad=jnp.pad(gla_w_lr, ((0, 0), (0, LANES - rank), (0, 0))),
        s5_tables=[_s5_tables(s5_a_re[i], s5_a_im[i], s5_log_dt[i], s5_b_re[i], s5_b_im[i], s5_c_re[i], s5_c_im[i])
                   for i in range(n_even)],
    )

    rows = nb + db
    rows_pad = -(-rows // 16) * 16
    c_all = jnp.concatenate([c_prompt, c_sample, jnp.zeros((rows_pad - rows, d), F32)], axis=0)
    mods = []
    for layer in range(depth):
        tn_mod = _pick(6 * d, (512, 256, 128))
        mod_l, = _matmul(c_all, [(w_mod, layer, 0)], n=6 * d, tm=rows_pad, tn=tn_mod, tk=d, out_dtypes=[F32],
                         epilogue=_epi_bias, prologue=_silu, name="adaln_mod",
                         extras=[(b_mod[:, None, :], (None, 1, tn_mod), lambda a, b, layer=layer: (layer, 0, b))])
        mods.append(mod_l)
    mods = jnp.stack(mods)
    mods_p, mods_s = mods[:, :nb], mods[:, nb:rows]

    zeros_s5 = jnp.zeros((n_even, nb, g_, p_), F32)
    st_p = dict(s5_re=zeros_s5, s5_im=zeros_s5, sb_k=None, sb_v=None,
                gla=jnp.zeros((n_odd, nb, gla_heads, gla_dk, gla_dv), F32), band_k=None, band_v=None)
    y_p, o_p = _trunk(x_prompt.reshape(nb * seq, d), nb, seq, mods_p, st_p, p, dims)
    st_s = dict(s5_re=state_s5_re, s5_im=state_s5_im,
                sb_k=cache_sb_k.reshape(n_even, db * past, sb_heads * sb_hd),
                sb_v=cache_sb_v.reshape(n_even, db * past, sb_heads * sb_hd),
                gla=state_gla, band_k=cache_band_k, band_v=cache_band_v)
    y_s, o_s = _trunk(x_sample.reshape(db * dseq, d), db, dseq, mods_s, st_s, p, dims)
    return (y_p.reshape(nb, seq, d), y_s.reshape(db, dseq, d),
            o_p['s5_re'], o_p['s5_im'], o_p['sb_k'], o_p['sb_v'], o_p['gla'], o_p['band_k'], o_p['band_v'],
            o_s['s5_re'], o_s['s5_im'], o_s['sb_k'], o_s['sb_v'], o_s['gla'], o_s['band_k'], o_s['band_v'])
```

```python
import functools
import math

import numpy as np
import jax
import jax.numpy as jnp
from jax import lax
from jax.experimental import pallas as pl
from jax.experimental.pallas import tpu as pltpu

F32 = jnp.float32
BF16 = jnp.bfloat16

CHUNK = 64
BAND_PREV = 8
BAND_PAST = BAND_PREV * CHUNK
GLA_GATE_NORM = 16.0
LN_EPS = 1e-5
RMS_EPS = 1e-6

LANES = 128
SUBLANES = 8
V7X_VMEM_BYTES = 64 * 1024 * 1024
VMEM_LIMIT = V7X_VMEM_BYTES - 8 * 1024 * 1024

GLA_SUB = 16
S5_LANE_GROUPS = 8
NEG = -1e30
SB_EXIT = -104.0

_NT = (((1,), (1,)), ((), ()))
_TN = (((0,), (0,)), ((), ()))


def _pick(dim, candidates):
    for c in candidates:
        if c <= dim and dim % c == 0:
            return c
    return dim


def _params(*sem):
    return pltpu.CompilerParams(dimension_semantics=sem, vmem_limit_bytes=VMEM_LIMIT)


def _bf(x):
    return x if x.dtype == BF16 else x.astype(BF16)


def _log_sigmoid(z):
    return jnp.minimum(z, 0.0) - jnp.log1p(jnp.exp(-jnp.abs(z)))


def _silu(x):
    return x * jax.nn.sigmoid(x)


def _mm_body(*refs, na, nb, ne, no, nk, prologue, epilogue):
    a_refs, refs = refs[:na], refs[na:]
    b_refs = refs[0:nb]
    e_refs = refs[nb:nb + ne]
    o_refs = refs[nb + ne:nb + ne + no]
    acc_refs = refs[nb + ne + no:]
    a_parts = [a_ref[...] for a_ref in a_refs]
    if prologue is not None:
        a_parts = [prologue(a) for a in a_parts]
    a_parts = [_bf(a) for a in a_parts]
    a = a_parts[0] if na == 1 else jnp.concatenate(a_parts, axis=1)
    parts = [jnp.dot(a, _bf(b_ref[...]), preferred_element_type=F32) for b_ref in b_refs]

    def finish(accs):
        outs = epilogue(accs, [e[...] for e in e_refs])
        for o_ref, o in zip(o_refs, outs):
            o_ref[...] = o.astype(o_ref.dtype)

    if nk == 1:
        finish(parts)
        return
    k = pl.program_id(2)

    @pl.when(k == 0)
    def _():
        for acc, p in zip(acc_refs, parts):
            acc[...] = p

    @pl.when(k > 0)
    def _():
        for acc, p in zip(acc_refs, parts):
            acc[...] += p

    @pl.when(k == nk - 1)
    def _():
        finish([acc[...] for acc in acc_refs])


def _matmul(a, bs, *, n, tm, tn, tk, out_dtypes, epilogue, extras=(), prologue=None, name="matmul"):
    a_list = list(a) if isinstance(a, (list, tuple)) else [a]
    m = a_list[0].shape[0]
    kdim = sum(x.shape[1] for x in a_list)
    nk = kdim // tk
    assert len(a_list) == 1 or nk == 1
    grid = (m // tm, n // tn, nk)
    if len(a_list) == 1:
        in_specs = [pl.BlockSpec((tm, tk), lambda i, j, k: (i, k))]
    else:
        in_specs = [pl.BlockSpec((tm, x.shape[1]), lambda i, j, k: (i, 0)) for x in a_list]
    operands = list(a_list)
    for arr, layer, c0 in bs:
        if arr.ndim == 3:
            in_specs.append(pl.BlockSpec((None, tk, tn), lambda i, j, k, layer=layer, c0=c0: (layer, k, c0 + j)))
        else:
            in_specs.append(pl.BlockSpec((tk, tn), lambda i, j, k, c0=c0: (k, c0 + j)))
        operands.append(arr)
    for arr, shape, imap in extras:
        in_specs.append(pl.BlockSpec(shape, lambda i, j, k, imap=imap: imap(i, j)))
        operands.append(arr)
    out_shape = [jax.ShapeDtypeStruct((m, n), dt) for dt in out_dtypes]
    out_specs = [pl.BlockSpec((tm, tn), lambda i, j, k: (i, j)) for _ in out_dtypes]
    scratch = [pltpu.VMEM((tm, tn), F32) for _ in bs] if nk > 1 else []
    body = functools.partial(_mm_body, na=len(a_list), nb=len(bs), ne=len(extras), no=len(out_dtypes), nk=nk,
                             prologue=prologue, epilogue=epilogue)
    outs = pl.pallas_call(
        body, grid=grid, in_specs=in_specs, out_specs=out_specs, out_shape=out_shape,
        scratch_shapes=scratch, compiler_params=_params("parallel", "parallel", "arbitrary"), name=name,
    )(*operands)
    return outs


def _epi_identity(accs, extras):
    return [accs[0]]


def _epi_bias(accs, extras):
    return [accs[0] + extras[0]]


def _epi_swiglu(accs, extras):
    return [_silu(accs[0]) * accs[1]]


def _epi_glu(accs, extras):
    z, bias = extras
    return [z * jax.nn.sigmoid(accs[0] + bias)]


class _Mod:
    def __init__(self, arr, d, l, expanded):
        self.arr, self.d, self.l, self.expanded = arr, d, l, expanded

    def spec(self, piece, tm):
        d = self.d
        if self.expanded:
            return pl.BlockSpec((None, tm, d), lambda i, *_: (0, i, piece))
        per = self.l // tm
        return pl.BlockSpec((None, 1, d), lambda i, *_: (i // per, 0, piece))


def _modulate_body(x_ref, sc_ref, sh_ref, h_ref):
    h_ref[...] = (x_ref[...] * (1.0 + sc_ref[...]) + sh_ref[...]).astype(h_ref.dtype)


def _modulate(x, mod, tm):
    m, d = x.shape
    row = pl.BlockSpec((tm, d), lambda i: (i, 0))
    return pl.pallas_call(
        _modulate_body, grid=(m // tm,), in_specs=[row, mod.spec(1, tm), mod.spec(0, tm)],
        out_specs=row, out_shape=jax.ShapeDtypeStruct((m, d), BF16),
        compiler_params=_params("parallel"), name="modulate",
    )(x, mod.arr, mod.arr)


def _ln_body(*refs, alpha, has_next):
    if has_next:
        x_ref, f_ref, g_ref, w_ref, b_ref, sc_ref, sh_ref, xo_ref, ho_ref = refs
    else:
        x_ref, f_ref, g_ref, w_ref, b_ref, xo_ref = refs
    y = alpha * x_ref[...] + (1.0 + g_ref[...]) * f_ref[...]
    yc = y - jnp.mean(y, axis=-1, keepdims=True)
    var = jnp.mean(yc * yc, axis=-1, keepdims=True)
    xn = yc * lax.rsqrt(var + LN_EPS) * w_ref[...] + b_ref[...]
    xo_ref[...] = xn
    if has_next:
        ho_ref[...] = (xn * (1.0 + sc_ref[...]) + sh_ref[...]).astype(ho_ref.dtype)


def _deepnorm(x, f, mod, gate_piece, w, b, layer, alpha, tm, nxt=None):
    m, d = x.shape
    row = pl.BlockSpec((tm, d), lambda i: (i, 0))
    vec = pl.BlockSpec((None, 1, d), lambda i: (layer, 0, 0))
    in_specs = [row, row, mod.spec(gate_piece, tm), vec, vec]
    operands = [x, f, mod.arr, w, b]
    out_shape = [jax.ShapeDtypeStruct((m, d), F32)]
    out_specs = [row]
    if nxt is not None:
        nmod, sc_piece, sh_piece = nxt
        in_specs += [nmod.spec(sc_piece, tm), nmod.spec(sh_piece, tm)]
        operands += [nmod.arr, nmod.arr]
        out_shape.append(jax.ShapeDtypeStruct((m, d), BF16))
        out_specs.append(row)
    outs = pl.pallas_call(
        functools.partial(_ln_body, alpha=alpha, has_next=nxt is not None),
        grid=(m // tm,), in_specs=in_specs, out_specs=out_specs, out_shape=out_shape,
        compiler_params=_params("parallel"), name="deepnorm",
    )(*operands)
    return outs if nxt is not None else (outs[0], None)


def _s5_body(u_ref, bt_ref, ct_ref, sc_ref, d_ref, x0_ref, z_ref, xo_ref, xs_ref, st_ref, *, t_rows):
    t = pl.program_id(2)
    sw = st_ref.shape[-1] // 2

    @pl.when(t == 0)
    def _():
        st_ref[...] = x0_ref[...]

    u = u_ref[...]
    xs_ref[...] = jnp.dot(_bf(u), bt_ref[...], preferred_element_type=F32)

    def cmul_add(xr, xi, ar, ai, sr, si):
        return xr + ar * sr - ai * si, xi + ar * si + ai * sr

    def block(j, carry):
        cr, ci = carry
        off = pl.multiple_of(j * SUBLANES, SUBLANES)
        xr = xs_ref[pl.ds(off, SUBLANES), 0:sw]
        xi = xs_ref[pl.ds(off, SUBLANES), sw:2 * sw]
        for step, d in enumerate((1, 2, 4)):
            ar = sc_ref[16 * step:16 * step + 8, :]
            ai = sc_ref[16 * step + 8:16 * step + 16, :]
            xr, xi = cmul_add(xr, xi, ar, ai, pltpu.roll(xr, d, 0), pltpu.roll(xi, d, 0))
        xr, xi = cmul_add(xr, xi, sc_ref[48:56, :], sc_ref[56:64, :], cr, ci)
        xs_ref[pl.ds(off, SUBLANES), 0:sw] = xr
        xs_ref[pl.ds(off, SUBLANES), sw:2 * sw] = xi
        return xr[SUBLANES - 1:SUBLANES, :], xi[SUBLANES - 1:SUBLANES, :]

    st = st_ref[...]
    nblk = t_rows // SUBLANES
    cr, ci = lax.fori_loop(0, nblk, block, (st[:, 0:sw], st[:, sw:2 * sw]), unroll=2 if nblk % 2 == 0 else 1)
    st_ref[:, 0:sw] = cr
    st_ref[:, sw:2 * sw] = ci
    y = jnp.dot(_bf(xs_ref[...]), ct_ref[...], preferred_element_type=F32) + d_ref[...] * u
    z_ref[...] = 0.5 * y * (1.0 + jnp.tanh(math.sqrt(2.0 / math.pi) * (y + 0.044715 * (y * y * y))))

    @pl.when(t == pl.num_programs(2) - 1)
    def _():
        xo_ref[:, 0:sw] = cr
        xo_ref[:, sw:2 * sw] = ci


def _s5_tables(a_re, a_im, log_dt, b_re, b_im, c_re, c_im):
    g, p = a_re.shape
    hg = b_re.shape[-1]
    gb = g // S5_LANE_GROUPS
    lr_, li_ = a_re.astype(F32), a_im.astype(F32)
    dt = jnp.exp(log_dt.astype(F32))[:, None]
    mag = jnp.exp(lr_ * dt)
    ab_r, ab_i = mag * jnp.cos(li_ * dt), mag * jnp.sin(li_ * dt)
    den = lr_ * lr_ + li_ * li_
    cr_ = ((ab_r - 1.0) * lr_ + ab_i * li_) / den
    ci_ = (ab_i * lr_ - (ab_r - 1.0) * li_) / den
    br_, bi_ = b_re.astype(F32), b_im.astype(F32)
    bb_r = cr_[..., None] * br_ - ci_[..., None] * bi_
    bb_i = cr_[..., None] * bi_ + ci_[..., None] * br_
    eye = jnp.eye(S5_LANE_GROUPS, dtype=F32)

    def blockdiag_in(w):
        w = w.reshape(gb, S5_LANE_GROUPS, p, hg)
        return jnp.einsum('bgph,gk->bghkp', w, eye).reshape(gb, S5_LANE_GROUPS * hg, S5_LANE_GROUPS * p)

    def blockdiag_out(w):
        w = w.reshape(gb, S5_LANE_GROUPS, hg, p)
        return jnp.einsum('bgkp,gj->bgpjk', w, eye).reshape(gb, S5_LANE_GROUPS * p, S5_LANE_GROUPS * hg)

    bt = jnp.concatenate([blockdiag_in(bb_r), blockdiag_in(bb_i)], axis=2).astype(BF16)
    ct = jnp.concatenate([blockdiag_out(c_re.astype(F32)), blockdiag_out(-c_im.astype(F32))], axis=1).astype(BF16)
    sw = S5_LANE_GROUPS * p

    def rows8(z):
        return jnp.broadcast_to(z.reshape(gb, 1, sw), (gb, SUBLANES, sw))

    pows = [(ab_r, ab_i)]
    for _ in range(SUBLANES - 1):
        pr, pi = pows[-1]
        pows.append((pr * ab_r - pi * ab_i, pr * ab_i + pi * ab_r))
    pieces = []
    trow = jnp.arange(SUBLANES)[None, :, None]
    for d in (1, 2, 4):
        pieces += [jnp.where(trow >= d, rows8(pows[d - 1][part]), 0.0) for part in (0, 1)]
    for part in (0, 1):
        pieces.append(jnp.stack([pw[part] for pw in pows], axis=0).reshape(SUBLANES, gb, sw).transpose(1, 0, 2))
    sc = jnp.concatenate(pieces, axis=1)
    return bt, ct, sc


def _s5(proj, n, l, tables, d_skip, layer, x0, t_rows):
    bt, ct, sc = tables
    gb, cw, sw2 = bt.shape
    nt = l // t_rows
    grid = (n, gb, nt)
    z, xo = pl.pallas_call(
        functools.partial(_s5_body, t_rows=t_rows), grid=grid,
        in_specs=[
            pl.BlockSpec((t_rows, cw), lambda b, g, t: (b * nt + t, g)),
            pl.BlockSpec((None, cw, sw2), lambda b, g, t: (g, 0, 0)),
            pl.BlockSpec((None, sw2, cw), lambda b, g, t: (g, 0, 0)),
            pl.BlockSpec((None, 8 * SUBLANES, sw2 // 2), lambda b, g, t: (g, 0, 0)),
            pl.BlockSpec((None, 1, cw), lambda b, g, t: (layer, 0, g)),
            pl.BlockSpec((None, 1, sw2), lambda b, g, t: (b * gb + g, 0, 0)),
        ],
        out_specs=[
            pl.BlockSpec((t_rows, cw), lambda b, g, t: (b * nt + t, g)),
            pl.BlockSpec((None, 1, sw2), lambda b, g, t: (b * gb + g, 0, 0)),
        ],
        out_shape=[jax.ShapeDtypeStruct((n * l, gb * cw), F32),
                   jax.ShapeDtypeStruct((n * gb, 1, sw2), F32)],
        scratch_shapes=[pltpu.VMEM((t_rows, sw2), F32), pltpu.VMEM((1, sw2), F32)],
        compiler_params=_params("parallel", "parallel", "arbitrary"), name="s5_scan",
    )(proj, bt, ct, sc, d_skip, x0)
    return z, xo


def _strict_upper2(tk):
    rj = lax.broadcasted_iota(jnp.int32, (tk, tk), 0)
    cs = lax.broadcasted_iota(jnp.int32, (tk, tk), 1)
    upper = jnp.where(rj > cs, 1.0, 0.0).astype(BF16)
    return jnp.concatenate([upper, upper], axis=0)


def _sb_tile(q, kt, vt, carry, mask, scale, upper2):
    z = lax.dot_general(q, _bf(kt), _NT, preferred_element_type=F32) * scale
    ls = _log_sigmoid(z)
    lf = ls - z
    if mask is not None:
        lf = jnp.where(mask, lf, 0.0)
    hi = lf.astype(BF16)
    lo = (lf - hi.astype(F32)).astype(BF16)
    newer = jnp.dot(jnp.concatenate([hi, lo], axis=1), upper2, preferred_element_type=F32)
    w = jnp.exp(ls + carry + newer)
    if mask is not None:
        w = jnp.where(mask, w, 0.0)
    out = jnp.dot(w.astype(BF16), _bf(vt), preferred_element_type=F32)
    return out, carry + jnp.sum(lf, axis=1, keepdims=True)


def _sb_body(q_ref, kd_ref, vd_ref, kp_ref, vp_ref, o_ref, *, tk, prev_tiles, hb):
    tq = q_ref.shape[0]
    d = q_ref.shape[1] // hb
    scale = d ** -0.5
    r = lax.broadcasted_iota(jnp.int32, (tq, tq), 0)
    c = lax.broadcasted_iota(jnp.int32, (tq, tq), 1)
    upper2 = _strict_upper2(tk)
    upper2_diag = upper2 if tq == tk else _strict_upper2(tq)
    nprev = prev_tiles(pl.program_id(2))
    first_off = pl.multiple_of(jnp.maximum(nprev - 1, 0) * tk, tk)

    def earlier(ref, off, h):
        if len(ref.shape) == 3:
            return ref[pl.ds(off, tk), h, :]
        return ref[pl.ds(off, tk), h * d:(h + 1) * d]

    states = []
    for h in range(hb):
        cols = slice(h * d, (h + 1) * d)
        q = _bf(q_ref[:, cols])
        acc, carry = _sb_tile(q, kd_ref[:, cols], vd_ref[:, cols], jnp.zeros((tq, 1), F32), c < r, scale,
                              upper2_diag)
        out, carry = _sb_tile(q, earlier(kp_ref, first_off, h), earlier(vp_ref, first_off, h), carry,
                              nprev > 0, scale, upper2)
        states.append((q, acc + out, carry))

    def cond(state):
        j, _, _, cmax = state
        return jnp.logical_and(j >= 0, cmax > SB_EXIT)

    for h, (q, acc, carry) in enumerate(states):
        cols = slice(h * d, (h + 1) * d)

        def body(state, q=q, h=h):
            j, acc, carry, _ = state
            off = pl.multiple_of(j * tk, tk)
            out, carry = _sb_tile(q, earlier(kp_ref, off, h), earlier(vp_ref, off, h), carry, None, scale, upper2)
            return j - 1, acc + out, carry, jnp.max(carry)

        first = jnp.asarray(nprev - 2, jnp.int32)
        _, acc, _, _ = lax.while_loop(cond, body, (first, acc, carry, jnp.max(carry)))
        o_ref[:, cols] = acc.astype(o_ref.dtype)


def _stick_breaking(proj, n, l, heads, hd, qcol, kcol, vcol, prev=None):
    if prev is None:
        hb = 2
        tq = _pick(l, (256, 128, 64, 32, 16, 8))
        tk = tq
        nq = l // tq
        kp, vp = proj, proj
        kp_spec = pl.BlockSpec((l, hb * hd), lambda b, h, i: (b, kcol // hb + h))
        vp_spec = pl.BlockSpec((l, hb * hd), lambda b, h, i: (b, vcol // hb + h))
        prev_tiles = lambda i: i
    else:
        hb = heads
        kc, vc, layer = prev
        past = kc.shape[1]
        tq, nq = l, 1
        tk = _pick(past, (256, 128, 64, 32, 16, 8))
        kp, vp = kc, vc
        kp_spec = pl.BlockSpec((None, past, heads, hd), lambda b, h, i: (layer * n + b, 0, 0, 0))
        vp_spec = pl.BlockSpec((None, past, heads, hd), lambda b, h, i: (layer * n + b, 0, 0, 0))
        prev_tiles = lambda i: past // tk
    assert heads % hb == 0 and qcol % hb == 0 and kcol % hb == 0 and vcol % hb == 0
    w = hb * hd
    return pl.pallas_call(
        functools.partial(_sb_body, tk=tk, prev_tiles=prev_tiles, hb=hb), grid=(n, heads // hb, nq),
        in_specs=[
            pl.BlockSpec((tq, w), lambda b, h, i: (b * nq + i, qcol // hb + h)),
            pl.BlockSpec((tq, w), lambda b, h, i: (b * nq + i, kcol // hb + h)),
            pl.BlockSpec((tq, w), lambda b, h, i: (b * nq + i, vcol // hb + h)),
            kp_spec, vp_spec,
        ],
        out_specs=pl.BlockSpec((tq, w), lambda b, h, i: (b * nq + i, h)),
        out_shape=jax.ShapeDtypeStruct((n * l, heads * hd), BF16),
        compiler_params=_params("parallel", "parallel", "arbitrary"), name="stick_breaking",
    )(proj, proj, proj, kp, vp)


def _gla_body(q_ref, k_ref, v_ref, g_ref, lr_ref, wlr_ref, blr_ref, nw_ref, s0_ref, y_ref, so_ref, st_ref, *, heads):
    t = pl.program_id(1)
    c = q_ref.shape[0]

    @pl.when(t == 0)
    def _():
        st_ref[...] = s0_ref[...]

    gate_in = jnp.dot(_bf(lr_ref[...]), _bf(wlr_ref[...]), preferred_element_type=F32) + blr_ref[...]
    logg = _log_sigmoid(gate_in) * (1.0 / GLA_GATE_NORM)
    rt = lax.broadcasted_iota(jnp.int32, (c, c), 0)
    cs = lax.broadcasted_iota(jnp.int32, (c, c), 1)
    lower = jnp.where(cs <= rt, 1.0, 0.0).astype(BF16)
    hi = logg.astype(BF16)
    lo = (logg - hi.astype(F32)).astype(BF16)
    b_all = jnp.dot(lower, hi, preferred_element_type=F32) + jnp.dot(lower, lo, preferred_element_type=F32)
    last = t == pl.num_programs(1) - 1
    for hh in range(heads):
        _gla_head(hh, b_all, q_ref, k_ref, v_ref, g_ref, nw_ref, y_ref, so_ref, st_ref, last, heads)


def _gla_head(hh, b_all, q_ref, k_ref, v_ref, g_ref, nw_ref, y_ref, so_ref, st_ref, last, heads):
    c = q_ref.shape[0]
    dk = q_ref.shape[1] // heads
    dv = v_ref.shape[1] // heads
    q = q_ref[:, hh * dk:(hh + 1) * dk] * (dk ** -0.5)
    k = k_ref[:, hh * dk:(hh + 1) * dk]
    vb = _bf(v_ref[:, hh * dv:(hh + 1) * dv])
    b = b_all[:, hh * dk:(hh + 1) * dk]
    st = st_ref[hh]
    o_inter = lax.dot_general(_bf(q * jnp.exp(b)), _bf(st), _NT, preferred_element_type=F32)
    row = lax.broadcasted_iota(jnp.int32, (GLA_SUB, 1), 0)
    lane = lax.broadcasted_iota(jnp.int32, (GLA_SUB, GLA_SUB), 1)
    outs = []
    for blk in range(c // GLA_SUB):
        r0, r1 = blk * GLA_SUB, (blk + 1) * GLA_SUB
        bi, qi, ki = b[r0:r1], q[r0:r1], k[r0:r1]
        att = jnp.zeros((GLA_SUB, GLA_SUB), F32)
        for s in range(GLA_SUB):
            dec = jnp.exp(jnp.where(row >= s, bi - bi[s:s + 1, :], NEG))
            col = jnp.sum(qi * dec * ki[s:s + 1, :], axis=1, keepdims=True)
            att = jnp.where(lane == s, col, att)
        oi = jnp.dot(_bf(att), vb[r0:r1], preferred_element_type=F32)
        if blk > 0:
            bref = b[r0 - 1:r0, :]
            qt = qi * jnp.exp(bi - bref)
            kt = k[0:r0] * jnp.exp(bref - b[0:r0])
            a_off = lax.dot_general(_bf(qt), _bf(kt), _NT, preferred_element_type=F32)
            oi = oi + jnp.dot(_bf(a_off), vb[0:r0], preferred_element_type=F32)
        outs.append(oi)
    o = (outs[0] if len(outs) == 1 else jnp.concatenate(outs, axis=0)) + o_inter
    o = o * lax.rsqrt(jnp.mean(o * o, axis=-1, keepdims=True) + RMS_EPS) * nw_ref[...]
    y_ref[:, hh * dv:(hh + 1) * dv] = (o * _silu(g_ref[:, hh * dv:(hh + 1) * dv])).astype(y_ref.dtype)

    bl = b[c - 1:c, :]
    kh = k * jnp.exp(bl - b)
    new = st * jnp.exp(bl) + lax.dot_general(vb, _bf(kh), _TN, preferred_element_type=F32)
    st_ref[hh] = new

    @pl.when(last)
    def _():
        so_ref[hh] = new


def _gla(proj, lr, n, l, heads, dk, dv, w_lr, b_lr, norm_w, layer, s0t):
    c = min(CHUNK, l)
    nc = l // c
    wk, wv = heads * dk, heads * dv
    assert (2 * wk) % wv == 0
    v_blk = (2 * wk) // wv
    rank_pad = lr.shape[1]
    y, so = pl.pallas_call(
        functools.partial(_gla_body, heads=heads), grid=(n, nc),
        in_specs=[
            pl.BlockSpec((c, wk), lambda b, t: (b * nc + t, 0)),
            pl.BlockSpec((c, wk), lambda b, t: (b * nc + t, 1)),
            pl.BlockSpec((c, wv), lambda b, t: (b * nc + t, v_blk)),
            pl.BlockSpec((c, wv), lambda b, t: (b * nc + t, v_blk + 1)),
            pl.BlockSpec((c, rank_pad), lambda b, t: (b * nc + t, 0)),
            pl.BlockSpec((None, rank_pad, wk), lambda b, t: (layer, 0, 0)),
            pl.BlockSpec((None, 1, wk), lambda b, t: (layer, 0, 0)),
            pl.BlockSpec((None, 1, dv), lambda b, t: (layer, 0, 0)),
            pl.BlockSpec((heads, dv, dk), lambda b, t: (b, 0, 0)),
        ],
        out_specs=[
            pl.BlockSpec((c, wv), lambda b, t: (b * nc + t, 0)),
            pl.BlockSpec((heads, dv, dk), lambda b, t: (b, 0, 0)),
        ],
        out_shape=[jax.ShapeDtypeStruct((n * l, wv), BF16),
                   jax.ShapeDtypeStruct((n * heads, dv, dk), F32)],
        scratch_shapes=[pltpu.VMEM((heads, dv, dk), F32)],
        compiler_params=_params("parallel", "arbitrary"), name="gla",
    )(proj, proj, proj, proj, lr, w_lr, b_lr, norm_w, s0t)
    return y, so


def _band_body(q_ref, k_ref, v_ref, bias_ref, o_ref, *, window, past, hb):
    tq = q_ref.shape[0]
    d = q_ref.shape[1] // hb
    qs = pl.program_id(2) * tq
    starts = [pl.multiple_of(jnp.maximum(qs - past + blk * tq, 0), tq) for blk in range(window // tq)]
    exists = lax.broadcasted_iota(jnp.int32, (tq, window), 1) >= past - qs
    for h in range(hb):
        cols = slice(h * d, (h + 1) * d)
        q = _bf(q_ref[:, cols])
        kw = jnp.concatenate([_bf(k_ref[pl.ds(st, tq), cols]) for st in starts], axis=0)
        vw = jnp.concatenate([_bf(v_ref[pl.ds(st, tq), cols]) for st in starts], axis=0)
        s = lax.dot_general(q, kw, _NT, preferred_element_type=F32) * (d ** -0.5) + bias_ref[h]
        s = jnp.where(exists, s, NEG)
        p = jnp.exp(s - jnp.max(s, axis=-1, keepdims=True))
        o = jnp.dot(p.astype(BF16), vw, preferred_element_type=F32) / jnp.sum(p, axis=-1, keepdims=True)
        o_ref[:, cols] = o.astype(o_ref.dtype)


def _toeplitz_bias(table, tq, window, offset, static_mask=None):
    rows, heads = table.shape
    clip = (rows - 1) // 2
    period = tq + window
    m = np.arange(period)
    m = np.where(m >= window, m - period, m)
    idx = np.clip(offset - m, -clip, clip) + clip
    vec = table.astype(F32)[idx, :].T
    flat = jnp.tile(vec, (1, tq))[:, :tq * (period - 1)]
    bias = flat.reshape(heads, tq, period - 1)[:, :, :window]
    if static_mask is not None:
        bias = jnp.where(static_mask[None], bias, NEG)
    return bias


def _band(q_arr, qcol, k_arr, v_arr, kcol, vcol, n, l, heads, hd, bias, tq, window, past, out_rows):
    nq = l // tq
    lk = k_arr.shape[0] // n
    hb = 2
    assert heads % hb == 0 and qcol % hb == 0 and kcol % hb == 0 and vcol % hb == 0
    w = hb * hd
    return pl.pallas_call(
        functools.partial(_band_body, window=window, past=past, hb=hb), grid=(n, heads // hb, nq),
        in_specs=[
            pl.BlockSpec((tq, w), lambda b, h, i: (b * nq + i, qcol // hb + h)),
            pl.BlockSpec((lk, w), lambda b, h, i: (b, kcol // hb + h)),
            pl.BlockSpec((lk, w), lambda b, h, i: (b, vcol // hb + h)),
            pl.BlockSpec((hb, tq, window), lambda b, h, i: (h, 0, 0)),
        ],
        out_specs=pl.BlockSpec((tq, w), lambda b, h, i: (b * nq + i, h)),
        out_shape=jax.ShapeDtypeStruct((out_rows, heads * hd), BF16),
        compiler_params=_params("parallel", "parallel", "arbitrary"), name="band_attention",
    )(q_arr, k_arr, v_arr, bias)


def _band_decode_body(q_ref, kn_ref, vn_ref, kc_ref, vc_ref, bc_ref, bn_ref, o_ref, *, heads):
    d = q_ref.shape[1] // heads
    scale = d ** -0.5
    for h in range(heads):
        cols = slice(h * d, (h + 1) * d)
        q = _bf(q_ref[:, cols])
        sc = lax.dot_general(q, _bf(kc_ref[:, h, :]), _NT, preferred_element_type=F32) * scale + bc_ref[h]
        sn = lax.dot_general(q, _bf(kn_ref[:, cols]), _NT, preferred_element_type=F32) * scale + bn_ref[h]
        m = jnp.maximum(jnp.max(sc, axis=-1, keepdims=True), jnp.max(sn, axis=-1, keepdims=True))
        pc, pn = jnp.exp(sc - m), jnp.exp(sn - m)
        o = (jnp.dot(pc.astype(BF16), _bf(vc_ref[:, h, :]), preferred_element_type=F32)
             + jnp.dot(pn.astype(BF16), _bf(vn_ref[:, cols]), preferred_element_type=F32))
        o = o / (jnp.sum(pc, axis=-1, keepdims=True) + jnp.sum(pn, axis=-1, keepdims=True))
        o_ref[:, cols] = o.astype(o_ref.dtype)


def _band_decode(projb, n, l, heads, hd, k_cache, v_cache, layer, bias_c, bias_n):
    w = heads * hd
    r = k_cache.shape[1]
    new = lambda blk: pl.BlockSpec((l, w), lambda b: (b, blk))
    cache = pl.BlockSpec((None, r, heads, hd), lambda b: (layer * n + b, 0, 0, 0))
    return pl.pallas_call(
        functools.partial(_band_decode_body, heads=heads), grid=(n,),
        in_specs=[new(0), new(1), new(2), cache, cache,
                  pl.BlockSpec((heads, l, r), lambda b: (0, 0, 0)),
                  pl.BlockSpec((heads, l, l), lambda b: (0, 0, 0))],
        out_specs=pl.BlockSpec((l, w), lambda b: (b, 0)),
        out_shape=jax.ShapeDtypeStruct((n * l, w), BF16),
        compiler_params=_params("parallel"), name="band_decode",
    )(projb, projb, projb, k_cache, v_cache, bias_c, bias_n)


def _kv_rows_body(*refs, nl, heads, hd):
    k_refs, v_refs = refs[0:nl], refs[nl:2 * nl]
    ko_ref, vo_ref = refs[2 * nl], refs[2 * nl + 1]
    layer = pl.program_id(0)
    for li in range(nl):
        @pl.when(layer == li)
        def _(li=li):
            for h in range(heads):
                ko_ref[:, h, :] = k_refs[li][:, h * hd:(h + 1) * hd]
                vo_ref[:, h, :] = v_refs[li][:, h * hd:(h + 1) * hd]


def _kv_rows(projs, kblk, vblk, heads, hd):
    nl = len(projs)
    m = projs[0].shape[0]
    w = heads * hd
    tm = _pick(m, (256, 128, 64, 32, 16, 8))
    nt = m // tm

    def in_spec(li, blk):
        return pl.BlockSpec((tm, w), lambda layer, i: (jnp.clip(i + (layer - li) * nt, 0, nt - 1), blk))

    out_spec = pl.BlockSpec((None, tm, heads, hd), lambda layer, i: (layer, i, 0, 0))
    out_sds = jax.ShapeDtypeStruct((nl, m, heads, hd), F32)
    return pl.pallas_call(
        functools.partial(_kv_rows_body, nl=nl, heads=heads, hd=hd), grid=(nl, nt),
        in_specs=[in_spec(li, kblk) for li in range(nl)] + [in_spec(li, vblk) for li in range(nl)],
        out_specs=[out_spec, out_spec], out_shape=[out_sds, out_sds],
        compiler_params=_params("arbitrary", "arbitrary"), name="kv_rows",
    )(*projs, *projs)


def _trunk(x, n, l, mods, st, p, dims):
    m, d = x.shape
    depth = p['w_mod'].shape[0]
    alpha = (2 * depth) ** 0.25
    expanded = l % 256 != 0
    tm_big = m if expanded else _pick(l, (1024, 512, 256))
    tm_row = m if expanded else 256
    d_ff = p['ffn_w_gate'].shape[2]
    g_, p_, hg = dims['s5']
    mix_a = g_ * hg
    sb_heads, sb_hd = dims['sb']
    gla_heads, gla_dk, gla_dv = dims['gla']
    ca_heads, ca_hd = dims['ca']
    sb_w = sb_heads * sb_hd
    ca_w = ca_heads * ca_hd
    gla_w = gla_heads * gla_dv

    def mod_of(layer):
        if expanded:
            arr = jnp.repeat(mods[layer], l, axis=0)[None]
        else:
            arr = mods[layer][:, None, :]
        return _Mod(arr, d, l, expanded)

    outs = {k: [] for k in ('s5_re', 's5_im', 'gla', 'band_k', 'band_v')}
    projs_ab = []
    mod = mod_of(0)
    h = _modulate(x, mod, tm_row)
    for layer in range(depth):
        i = layer // 2
        if layer % 2 == 0:
            in_ab = p['ab_w_in'].shape[2]
            proj, = _matmul(h, [(p['ab_w_in'], i, 0)], n=in_ab, tm=tm_big, tn=256, tk=d, out_dtypes=[F32],
                            epilogue=_epi_identity, name="ab_in")
            gb = g_ // S5_LANE_GROUPS
            sw = S5_LANE_GROUPS * p_
            x0 = jnp.concatenate([st['s5_re'][i].reshape(n * gb, 1, sw), st['s5_im'][i].reshape(n * gb, 1, sw)], axis=2)
            z, xo = _s5(proj, n, l, p['s5_tables'][i], p['s5_d'], i, x0, _pick(l, (256, 128, 64, 32, 16, 8)))
            outs['s5_re'].append(xo[:, 0, :sw].reshape(n, g_, p_))
            outs['s5_im'].append(xo[:, 0, sw:].reshape(n, g_, p_))
            tn_glu = 256
            ya, = _matmul(z, [(p['s5_w_glu'], i, 0)], n=mix_a, tm=tm_big, tn=tn_glu, tk=mix_a, out_dtypes=[BF16],
                          epilogue=_epi_glu, name="s5_glu",
                          extras=[(z, (tm_big, tn_glu), lambda a, b: (a, b)),
                                  (p['s5_b_glu'], (None, 1, tn_glu), lambda a, b, i=i: (i, 0, b))])
            qc, kc, vc = mix_a // sb_hd, (mix_a + sb_w) // sb_hd, (mix_a + 2 * sb_w) // sb_hd
            prev = None if st['sb_k'] is None else (st['sb_k'], st['sb_v'], i)
            yb = _stick_breaking(proj, n, l, sb_heads, sb_hd, qc, kc, vc, prev)
            projs_ab.append(proj)
            merged = [ya, yb]
            w_out = p['ab_w_out']
        else:
            n_gla = 2 * gla_heads * gla_dk + 2 * gla_w
            proj, = _matmul(h, [(p['cd_w_gla'][i], None, 0)], n=n_gla, tm=tm_big, tn=256, tk=d, out_dtypes=[F32],
                            epilogue=_epi_identity, name="cd_in_gla")
            projb, = _matmul(h, [(p['cd_w_band'][i], None, 0)], n=3 * ca_w, tm=tm_big, tn=256, tk=d,
                             out_dtypes=[F32], epilogue=_epi_identity, name="cd_in_band")
            lr, = _matmul(h, [(p['cd_w_lrin'][i], None, 0)], n=LANES, tm=tm_big, tn=LANES, tk=d, out_dtypes=[F32],
                          epilogue=_epi_identity, name="cd_lr")
            s0t = jnp.swapaxes(st['gla'][i], -1, -2).reshape(n * gla_heads, gla_dv, gla_dk)
            yc, so = _gla(proj, lr, n, l, gla_heads, gla_dk, gla_dv, p['gla_w_lr_pad'], p['gla_b_lr'],
                          p['gla_norm_w'], i, s0t)
            outs['gla'].append(jnp.swapaxes(so.reshape(n, gla_heads, gla_dv, gla_dk), -1, -2))
            ka = projb[:, ca_w:2 * ca_w]
            va = projb[:, 2 * ca_w:3 * ca_w]
            table = p['ca_rel_bias'][i]
            if st['band_k'] is None:
                tq = _pick(l, (256, 128, 64))
                window = tq + BAND_PAST
                qchunk = np.arange(tq)[:, None] // CHUNK
                kchunk = np.arange(window)[None, :] // CHUNK
                static_mask = (kchunk >= qchunk) & (kchunk <= qchunk + BAND_PREV)
                bias = _toeplitz_bias(table, tq, window, BAND_PAST, static_mask)
                yd = _band(projb, 0, projb, projb, ca_heads, 2 * ca_heads, n, l, ca_heads, ca_hd, bias, tq, window,
                           BAND_PAST, m)
                keep = min(BAND_PAST, l)
                outs['band_k'].append(ka.reshape(n, l, ca_heads, ca_hd)[:, l - keep:])
                outs['band_v'].append(va.reshape(n, l, ca_heads, ca_hd)[:, l - keep:])
            else:
                r = st['band_k'].shape[1]
                bias = _toeplitz_bias(table, l, r + l, r)
                yd = _band_decode(projb, n, l, ca_heads, ca_hd, st['band_k'], st['band_v'], i,
                                  bias[:, :, :r], bias[:, :, r:])
                outs['band_k'].append(ka.reshape(n, l, ca_heads, ca_hd))
                outs['band_v'].append(va.reshape(n, l, ca_heads, ca_hd))
            merged = [yc, yd]
            w_out = p['cd_w_out']
        f, = _matmul(merged, [(w_out, i, 0)], n=d, tm=tm_big, tn=256, tk=sum(y.shape[1] for y in merged),
                     out_dtypes=[F32], epilogue=_epi_identity, name="mix_out")
        x, h = _deepnorm(x, f, mod, 2, p['ln1_w'], p['ln1_b'], layer, alpha, tm_row, nxt=(mod, 4, 3))
        tn_ff = _pick(d_ff, (256, 128))
        act, = _matmul(h, [(p['ffn_w_gate'], layer, 0), (p['ffn_w_up'], layer, 0)], n=d_ff, tm=tm_big, tn=tn_ff,
                       tk=d, out_dtypes=[BF16], epilogue=_epi_swiglu, name="ffn_in")
        ff, = _matmul(act, [(p['ffn_w_down'], layer, 0)], n=d, tm=min(tm_big, 512), tn=256, tk=d_ff,
                      out_dtypes=[F32], epilogue=_epi_identity, name="ffn_out")
        if layer + 1 < depth:
            nmod = mod_of(layer + 1)
            x, h = _deepnorm(x, ff, mod, 5, p['ln2_w'], p['ln2_b'], layer, alpha, tm_row, nxt=(nmod, 1, 0))
            mod = nmod
        else:
            x, _ = _deepnorm(x, ff, mod, 5, p['ln2_w'], p['ln2_b'], layer, alpha, tm_row)
    outs = {k: jnp.stack(v) for k, v in outs.items()}
    assert mix_a % sb_w == 0
    sb_k, sb_v = _kv_rows(projs_ab, mix_a // sb_w + 1, mix_a // sb_w + 2, sb_heads, sb_hd)
    outs['sb_k'] = sb_k.reshape(len(projs_ab), n, l, sb_heads, sb_hd)
    outs['sb_v'] = sb_v.reshape(len(projs_ab), n, l, sb_heads, sb_hd)
    return x, outs


def kernel(x_prompt, x_sample, state_s5_re, state_s5_im, cache_sb_k, cache_sb_v, state_gla, cache_band_k, cache_band_v, c_prompt, c_sample, w_mod, b_mod, ln1_w, ln1_b, ln2_w, ln2_b, ab_w_in, ab_w_out, s5_a_re, s5_a_im, s5_log_dt, s5_b_re, s5_b_im, s5_c_re, s5_c_im, s5_d, s5_w_glu, s5_b_glu, cd_w_in, cd_w_out, gla_w_lr, gla_b_lr, gla_norm_w, ca_rel_bias, ffn_w_gate, ffn_w_up, ffn_w_down):
    nb, seq, d = x_prompt.shape
    db, dseq, _ = x_sample.shape
    depth = w_mod.shape[0]
    n_even, g_, p_ = s5_a_re.shape
    hg = s5_b_re.shape[-1]
    _, _, past, sb_heads, sb_hd = cache_sb_k.shape
    n_odd, _, gla_heads, gla_dk, gla_dv = state_gla.shape
    _, _, band_rows, ca_heads, ca_hd = cache_band_k.shape
    rank = gla_w_lr.shape[1]
    dims = {'s5': (g_, p_, hg), 'sb': (sb_heads, sb_hd), 'gla': (gla_heads, gla_dk, gla_dv), 'ca': (ca_heads, ca_hd)}

    lr0 = 2 * gla_heads * gla_dk + 2 * gla_heads * gla_dv
    p = dict(
        w_mod=w_mod, ln1_w=ln1_w[:, None, :], ln1_b=ln1_b[:, None, :], ln2_w=ln2_w[:, None, :], ln2_b=ln2_b[:, None, :],
        ab_w_in=ab_w_in, ab_w_out=ab_w_out, s5_d=s5_d[:, None, :], s5_w_glu=s5_w_glu, s5_b_glu=s5_b_glu[:, None, :],
        cd_w_out=cd_w_out, gla_b_lr=gla_b_lr[:, None, :], gla_norm_w=gla_norm_w[:, None, :], ca_rel_bias=ca_rel_bias,
        ffn_w_gate=ffn_w_gate, ffn_w_up=ffn_w_up, ffn_w_down=ffn_w_down.astype(BF16),
        cd_w_gla=[cd_w_in[i, :, :lr0] for i in range(n_odd)],
        cd_w_band=[cd_w_in[i, :, lr0 + rank:] for i in range(n_odd)],
        cd_w_lrin=[jnp.pad(cd_w_in[i, :, lr0:lr0 + rank], ((0, 0), (0, LANES - rank))) for i in range(n_odd)],
        gla_w_lr_pad=jnp.pad(gla_w_lr, ((0, 0), (0, LANES - rank), (0, 0))),
        s5_tables=[_s5_tables(s5_a_re[i], s5_a_im[i], s5_log_dt[i], s5_b_re[i], s5_b_im[i], s5_c_re[i], s5_c_im[i])
                   for i in range(n_even)],
    )

    rows = nb + db
    rows_pad = -(-rows // 16) * 16
    c_all = jnp.concatenate([c_prompt, c_sample, jnp.zeros((rows_pad - rows, d), F32)], axis=0)
    mods = []
    for layer in range(depth):
        tn_mod = _pick(6 * d, (512, 256, 128))
        mod_l, = _matmul(c_all, [(w_mod, layer, 0)], n=6 * d, tm=rows_pad, tn=tn_mod, tk=d, out_dtypes=[F32],
                         epilogue=_epi_bias, prologue=_silu, name="adaln_mod",
                         extras=[(b_mod[:, None, :], (None, 1, tn_mod), lambda a, b, layer=layer: (layer, 0, b))])
        mods.append(mod_l)
    mods = jnp.stack(mods)
    mods_p, mods_s = mods[:, :nb], mods[:, nb:rows]

    zeros_s5 = jnp.zeros((n_even, nb, g_, p_), F32)
    st_p = dict(s5_re=zeros_s5, s5_im=zeros_s5, sb_k=None, sb_v=None,
                gla=jnp.zeros((n_odd, nb, gla_heads, gla_dk, gla_dv), F32), band_k=None, band_v=None)
    y_p, o_p = _trunk(x_prompt.reshape(nb * seq, d), nb, seq, mods_p, st_p, p, dims)
    st_s = dict(s5_re=state_s5_re, s5_im=state_s5_im,
                sb_k=cache_sb_k.reshape(n_even * db, past, sb_heads, sb_hd),
                sb_v=cache_sb_v.reshape(n_even * db, past, sb_heads, sb_hd),
                gla=state_gla,
                band_k=cache_band_k.reshape(n_odd * db, band_rows, ca_heads, ca_hd),
                band_v=cache_band_v.reshape(n_odd * db, band_rows, ca_heads, ca_hd))
    y_s, o_s = _trunk(x_sample.reshape(db * dseq, d), db, dseq, mods_s, st_s, p, dims)
    return (y_p.reshape(nb, seq, d), y_s.reshape(db, dseq, d),
            o_p['s5_re'], o_p['s5_im'], o_p['sb_k'], o_p['sb_v'], o_p['gla'], o_p['band_k'], o_p['band_v'],
            o_s['s5_re'], o_s['s5_im'], o_s['sb_k'], o_s['sb_v'], o_s['gla'], o_s['band_k'], o_s['band_v'])
```

```python
import functools
import math

import numpy as np
import jax
import jax.numpy as jnp
from jax import lax
from jax.experimental import pallas as pl
from jax.experimental.pallas import tpu as pltpu

F32 = jnp.float32
BF16 = jnp.bfloat16

CHUNK = 64
BAND_PREV = 8
BAND_PAST = BAND_PREV * CHUNK
GLA_GATE_NORM = 16.0
LN_EPS = 1e-5
RMS_EPS = 1e-6

LANES = 128
SUBLANES = 8
V7X_VMEM_BYTES = 64 * 1024 * 1024
VMEM_LIMIT = V7X_VMEM_BYTES - 8 * 1024 * 1024

GLA_SUB = 16
S5_LANE_GROUPS = 8
NEG = -1e30
SB_EXIT = -104.0

_NT = (((1,), (1,)), ((), ()))
_TN = (((0,), (0,)), ((), ()))


def _pick(dim, candidates):
    for c in candidates:
        if c <= dim and dim % c == 0:
            return c
    return dim


def _params(*sem):
    return pltpu.CompilerParams(dimension_semantics=sem, vmem_limit_bytes=VMEM_LIMIT)


def _bf(x):
    return x if x.dtype == BF16 else x.astype(BF16)


def _log_sigmoid(z):
    return jnp.minimum(z, 0.0) - jnp.log1p(jnp.exp(-jnp.abs(z)))


def _silu(x):
    return x * jax.nn.sigmoid(x)


def _mm_body(*refs, na, nb, ne, no, nk, prologue, epilogue):
    a_refs, refs = refs[:na], refs[na:]
    b_refs = refs[0:nb]
    e_refs = refs[nb:nb + ne]
    o_refs = refs[nb + ne:nb + ne + no]
    acc_refs = refs[nb + ne + no:]
    a_parts = [a_ref[...] for a_ref in a_refs]
    if prologue is not None:
        a_parts = [prologue(a) for a in a_parts]
    a_parts = [_bf(a) for a in a_parts]
    a = a_parts[0] if na == 1 else jnp.concatenate(a_parts, axis=1)
    parts = [jnp.dot(a, _bf(b_ref[...]), preferred_element_type=F32) for b_ref in b_refs]

    def finish(accs):
        outs = epilogue(accs, [e[...] for e in e_refs])
        for o_ref, o in zip(o_refs, outs):
            o_ref[...] = o.astype(o_ref.dtype)

    if nk == 1:
        finish(parts)
        return
    k = pl.program_id(2)

    @pl.when(k == 0)
    def _():
        for acc, p in zip(acc_refs, parts):
            acc[...] = p

    @pl.when(k > 0)
    def _():
        for acc, p in zip(acc_refs, parts):
            acc[...] += p

    @pl.when(k == nk - 1)
    def _():
        finish([acc[...] for acc in acc_refs])


def _matmul(a, bs, *, n, tm, tn, tk, out_dtypes, epilogue, extras=(), prologue=None, name="matmul"):
    a_list = list(a) if isinstance(a, (list, tuple)) else [a]
    m = a_list[0].shape[0]
    kdim = sum(x.shape[1] for x in a_list)
    nk = kdim // tk
    assert len(a_list) == 1 or nk == 1
    grid = (m // tm, n // tn, nk)
    if len(a_list) == 1:
        in_specs = [pl.BlockSpec((tm, tk), lambda i, j, k: (i, k))]
    else:
        in_specs = [pl.BlockSpec((tm, x.shape[1]), lambda i, j, k: (i, 0)) for x in a_list]
    operands = list(a_list)
    for arr, layer, c0 in bs:
        if arr.ndim == 3:
            in_specs.append(pl.BlockSpec((None, tk, tn), lambda i, j, k, layer=layer, c0=c0: (layer, k, c0 + j)))
        else:
            in_specs.append(pl.BlockSpec((tk, tn), lambda i, j, k, c0=c0: (k, c0 + j)))
        operands.append(arr)
    for arr, shape, imap in extras:
        in_specs.append(pl.BlockSpec(shape, lambda i, j, k, imap=imap: imap(i, j)))
        operands.append(arr)
    out_shape = [jax.ShapeDtypeStruct((m, n), dt) for dt in out_dtypes]
    out_specs = [pl.BlockSpec((tm, tn), lambda i, j, k: (i, j)) for _ in out_dtypes]
    scratch = [pltpu.VMEM((tm, tn), F32) for _ in bs] if nk > 1 else []
    body = functools.partial(_mm_body, na=len(a_list), nb=len(bs), ne=len(extras), no=len(out_dtypes), nk=nk,
                             prologue=prologue, epilogue=epilogue)
    outs = pl.pallas_call(
        body, grid=grid, in_specs=in_specs, out_specs=out_specs, out_shape=out_shape,
        scratch_shapes=scratch, compiler_params=_params("parallel", "parallel", "arbitrary"), name=name,
    )(*operands)
    return outs


def _epi_identity(accs, extras):
    return [accs[0]]


def _epi_bias(accs, extras):
    return [accs[0] + extras[0]]


def _epi_swiglu(accs, extras):
    return [_silu(accs[0]) * accs[1]]


def _epi_glu(accs, extras):
    z, bias = extras
    return [z * jax.nn.sigmoid(accs[0] + bias)]


class _Mod:
    def __init__(self, arr, d, l, expanded):
        self.arr, self.d, self.l, self.expanded = arr, d, l, expanded

    def spec(self, piece, tm):
        d = self.d
        if self.expanded:
            return pl.BlockSpec((None, tm, d), lambda i, *_: (0, i, piece))
        per = self.l // tm
        return pl.BlockSpec((None, 1, d), lambda i, *_: (i // per, 0, piece))


def _modulate_body(x_ref, sc_ref, sh_ref, h_ref):
    h_ref[...] = (x_ref[...] * (1.0 + sc_ref[...]) + sh_ref[...]).astype(h_ref.dtype)


def _modulate(x, mod, tm):
    m, d = x.shape
    row = pl.BlockSpec((tm, d), lambda i: (i, 0))
    return pl.pallas_call(
        _modulate_body, grid=(m // tm,), in_specs=[row, mod.spec(1, tm), mod.spec(0, tm)],
        out_specs=row, out_shape=jax.ShapeDtypeStruct((m, d), BF16),
        compiler_params=_params("parallel"), name="modulate",
    )(x, mod.arr, mod.arr)


def _ln_body(*refs, alpha, has_next):
    if has_next:
        x_ref, f_ref, g_ref, w_ref, b_ref, sc_ref, sh_ref, xo_ref, ho_ref = refs
    else:
        x_ref, f_ref, g_ref, w_ref, b_ref, xo_ref = refs
    y = alpha * x_ref[...] + (1.0 + g_ref[...]) * f_ref[...]
    yc = y - jnp.mean(y, axis=-1, keepdims=True)
    var = jnp.mean(yc * yc, axis=-1, keepdims=True)
    xn = yc * lax.rsqrt(var + LN_EPS) * w_ref[...] + b_ref[...]
    xo_ref[...] = xn
    if has_next:
        ho_ref[...] = (xn * (1.0 + sc_ref[...]) + sh_ref[...]).astype(ho_ref.dtype)


def _deepnorm(x, f, mod, gate_piece, w, b, layer, alpha, tm, nxt=None):
    m, d = x.shape
    row = pl.BlockSpec((tm, d), lambda i: (i, 0))
    vec = pl.BlockSpec((None, 1, d), lambda i: (layer, 0, 0))
    in_specs = [row, row, mod.spec(gate_piece, tm), vec, vec]
    operands = [x, f, mod.arr, w, b]
    out_shape = [jax.ShapeDtypeStruct((m, d), F32)]
    out_specs = [row]
    if nxt is not None:
        nmod, sc_piece, sh_piece = nxt
        in_specs += [nmod.spec(sc_piece, tm), nmod.spec(sh_piece, tm)]
        operands += [nmod.arr, nmod.arr]
        out_shape.append(jax.ShapeDtypeStruct((m, d), BF16))
        out_specs.append(row)
    outs = pl.pallas_call(
        functools.partial(_ln_body, alpha=alpha, has_next=nxt is not None),
        grid=(m // tm,), in_specs=in_specs, out_specs=out_specs, out_shape=out_shape,
        compiler_params=_params("parallel"), name="deepnorm",
    )(*operands)
    return outs if nxt is not None else (outs[0], None)


def _s5_body(u_ref, bt_ref, ct_ref, sc_ref, d_ref, x0_ref, z_ref, xo_ref, xs_ref, st_ref, *, t_rows):
    t = pl.program_id(2)
    sw = st_ref.shape[-1] // 2

    @pl.when(t == 0)
    def _():
        st_ref[...] = x0_ref[...]

    seg = t_rows // SUBLANES
    u = jnp.concatenate([u_ref[pl.ds(j, SUBLANES, stride=seg), :] for j in range(seg)], axis=0)
    xs_ref[...] = jnp.dot(_bf(u), bt_ref[...], preferred_element_type=F32)
    ar, ai = sc_ref[0:8, :], sc_ref[8:16, :]
    unroll = 2 if seg % 2 == 0 else 1

    def cmul_add(xr, xi, ar, ai, sr, si):
        return xr + ar * sr - ai * si, xi + ar * si + ai * sr

    def local(j, carry):
        off = pl.multiple_of(j * SUBLANES, SUBLANES)
        xr, xi = cmul_add(xs_ref[pl.ds(off, SUBLANES), 0:sw], xs_ref[pl.ds(off, SUBLANES), sw:2 * sw],
                          ar, ai, carry[0], carry[1])
        xs_ref[pl.ds(off, SUBLANES), 0:sw] = xr
        xs_ref[pl.ds(off, SUBLANES), sw:2 * sw] = xi
        return xr, xi

    zero = jnp.zeros((SUBLANES, sw), F32)
    gr, gi = lax.fori_loop(0, seg, local, (zero, zero), unroll=unroll)

    for step, d in enumerate((1, 2, 4)):
        r0 = 16 + 16 * step
        gr, gi = cmul_add(gr, gi, sc_ref[r0:r0 + 8, :], sc_ref[r0 + 8:r0 + 16, :],
                          pltpu.roll(gr, d, 0), pltpu.roll(gi, d, 0))
    st = st_ref[...]
    sr, si = st[:, 0:sw], st[:, sw:2 * sw]
    gr, gi = cmul_add(gr, gi, sc_ref[64:72, :], sc_ref[72:80, :], sr, si)
    fr, fi = gr[SUBLANES - 1:SUBLANES, :], gi[SUBLANES - 1:SUBLANES, :]
    st_ref[:, 0:sw] = fr
    st_ref[:, sw:2 * sw] = fi

    row = lax.broadcasted_iota(jnp.int32, (SUBLANES, sw), 0)
    cr = jnp.where(row == 0, sr, pltpu.roll(gr, 1, 0))
    ci = jnp.where(row == 0, si, pltpu.roll(gi, 1, 0))

    def fix(j, carry):
        off = pl.multiple_of(j * SUBLANES, SUBLANES)
        dr = ar * carry[0] - ai * carry[1]
        di = ar * carry[1] + ai * carry[0]
        xs_ref[pl.ds(off, SUBLANES), 0:sw] += dr
        xs_ref[pl.ds(off, SUBLANES), sw:2 * sw] += di
        return dr, di

    lax.fori_loop(0, seg, fix, (cr, ci), unroll=unroll)

    y = jnp.dot(_bf(xs_ref[...]), ct_ref[...], preferred_element_type=F32) + d_ref[...] * u
    z = 0.5 * y * (1.0 + jnp.tanh(math.sqrt(2.0 / math.pi) * (y + 0.044715 * (y * y * y))))
    for j in range(seg):
        z_ref[pl.ds(j, SUBLANES, stride=seg), :] = z[j * SUBLANES:(j + 1) * SUBLANES, :]

    @pl.when(t == pl.num_programs(2) - 1)
    def _():
        xo_ref[:, 0:sw] = fr
        xo_ref[:, sw:2 * sw] = fi


def _s5_tables(a_re, a_im, log_dt, b_re, b_im, c_re, c_im, seg):
    g, p = a_re.shape
    hg = b_re.shape[-1]
    gb = g // S5_LANE_GROUPS
    lr_, li_ = a_re.astype(F32), a_im.astype(F32)
    dt = jnp.exp(log_dt.astype(F32))[:, None]
    mag = jnp.exp(lr_ * dt)
    ab_r, ab_i = mag * jnp.cos(li_ * dt), mag * jnp.sin(li_ * dt)
    den = lr_ * lr_ + li_ * li_
    cr_ = ((ab_r - 1.0) * lr_ + ab_i * li_) / den
    ci_ = (ab_i * lr_ - (ab_r - 1.0) * li_) / den
    br_, bi_ = b_re.astype(F32), b_im.astype(F32)
    bb_r = cr_[..., None] * br_ - ci_[..., None] * bi_
    bb_i = cr_[..., None] * bi_ + ci_[..., None] * br_
    eye = jnp.eye(S5_LANE_GROUPS, dtype=F32)

    def blockdiag_in(w):
        w = w.reshape(gb, S5_LANE_GROUPS, p, hg)
        return jnp.einsum('bgph,gk->bghkp', w, eye).reshape(gb, S5_LANE_GROUPS * hg, S5_LANE_GROUPS * p)

    def blockdiag_out(w):
        w = w.reshape(gb, S5_LANE_GROUPS, hg, p)
        return jnp.einsum('bgkp,gj->bgpjk', w, eye).reshape(gb, S5_LANE_GROUPS * p, S5_LANE_GROUPS * hg)

    bt = jnp.concatenate([blockdiag_in(bb_r), blockdiag_in(bb_i)], axis=2).astype(BF16)
    ct = jnp.concatenate([blockdiag_out(c_re.astype(F32)), blockdiag_out(-c_im.astype(F32))], axis=1).astype(BF16)
    sw = S5_LANE_GROUPS * p

    def rows8(z):
        return jnp.broadcast_to(z.reshape(gb, 1, sw), (gb, SUBLANES, sw))

    def cmul(x, y):
        return x[0] * y[0] - x[1] * y[1], x[0] * y[1] + x[1] * y[0]

    a_seg, sq, e = None, (ab_r, ab_i), seg
    while e:
        if e & 1:
            a_seg = sq if a_seg is None else cmul(a_seg, sq)
        sq, e = cmul(sq, sq), e >> 1
    pows = [a_seg]
    for _ in range(SUBLANES - 1):
        pows.append(cmul(pows[-1], a_seg))
    pieces = [rows8(ab_r), rows8(ab_i)]
    trow = jnp.arange(SUBLANES)[None, :, None]
    for d in (1, 2, 4):
        pieces += [jnp.where(trow >= d, rows8(pows[d - 1][part]), 0.0) for part in (0, 1)]
    for part in (0, 1):
        pieces.append(jnp.stack([pw[part] for pw in pows], axis=0).reshape(SUBLANES, gb, sw).transpose(1, 0, 2))
    sc = jnp.concatenate(pieces, axis=1)
    return bt, ct, sc


def _s5(proj, n, l, tables, d_skip, layer, x0, t_rows):
    bt, ct, sc = tables
    gb, cw, sw2 = bt.shape
    nt = l // t_rows
    grid = (n, gb, nt)
    z, xo = pl.pallas_call(
        functools.partial(_s5_body, t_rows=t_rows), grid=grid,
        in_specs=[
            pl.BlockSpec((t_rows, cw), lambda b, g, t: (b * nt + t, g)),
            pl.BlockSpec((None, cw, sw2), lambda b, g, t: (g, 0, 0)),
            pl.BlockSpec((None, sw2, cw), lambda b, g, t: (g, 0, 0)),
            pl.BlockSpec((None, 10 * SUBLANES, sw2 // 2), lambda b, g, t: (g, 0, 0)),
            pl.BlockSpec((None, 1, cw), lambda b, g, t: (layer, 0, g)),
            pl.BlockSpec((None, 1, sw2), lambda b, g, t: (b * gb + g, 0, 0)),
        ],
        out_specs=[
            pl.BlockSpec((t_rows, cw), lambda b, g, t: (b * nt + t, g)),
            pl.BlockSpec((None, 1, sw2), lambda b, g, t: (b * gb + g, 0, 0)),
        ],
        out_shape=[jax.ShapeDtypeStruct((n * l, gb * cw), F32),
                   jax.ShapeDtypeStruct((n * gb, 1, sw2), F32)],
        scratch_shapes=[pltpu.VMEM((t_rows, sw2), F32), pltpu.VMEM((1, sw2), F32)],
        compiler_params=_params("parallel", "parallel", "arbitrary"), name="s5_scan",
    )(proj, bt, ct, sc, d_skip, x0)
    return z, xo


def _strict_upper2(tk):
    rj = lax.broadcasted_iota(jnp.int32, (tk, tk), 0)
    cs = lax.broadcasted_iota(jnp.int32, (tk, tk), 1)
    upper = jnp.where(rj > cs, 1.0, 0.0).astype(BF16)
    return jnp.concatenate([upper, upper], axis=0)


def _sb_tile(q, kt, vt, carry, mask, scale, upper2):
    z = lax.dot_general(q, _bf(kt), _NT, preferred_element_type=F32) * scale
    ls = _log_sigmoid(z)
    lf = ls - z
    if mask is not None:
        lf = jnp.where(mask, lf, 0.0)
    hi = lf.astype(BF16)
    lo = (lf - hi.astype(F32)).astype(BF16)
    newer = jnp.dot(jnp.concatenate([hi, lo], axis=1), upper2, preferred_element_type=F32)
    w = jnp.exp(ls + carry + newer)
    if mask is not None:
        w = jnp.where(mask, w, 0.0)
    out = jnp.dot(w.astype(BF16), _bf(vt), preferred_element_type=F32)
    return out, carry + jnp.sum(lf, axis=1, keepdims=True)


def _sb_body(q_ref, kd_ref, vd_ref, kp_ref, vp_ref, o_ref, *, tk, prev_tiles, hb):
    tq = q_ref.shape[0]
    d = q_ref.shape[1] // hb
    scale = d ** -0.5
    r = lax.broadcasted_iota(jnp.int32, (tq, tq), 0)
    c = lax.broadcasted_iota(jnp.int32, (tq, tq), 1)
    upper2 = _strict_upper2(tk)
    upper2_diag = upper2 if tq == tk else _strict_upper2(tq)
    nprev = prev_tiles(pl.program_id(2))
    first_off = pl.multiple_of(jnp.maximum(nprev - 1, 0) * tk, tk)

    def earlier(ref, off, h):
        if len(ref.shape) == 3:
            return ref[pl.ds(off, tk), h, :]
        return ref[pl.ds(off, tk), h * d:(h + 1) * d]

    states = []
    for h in range(hb):
        cols = slice(h * d, (h + 1) * d)
        q = _bf(q_ref[:, cols])
        acc, carry = _sb_tile(q, kd_ref[:, cols], vd_ref[:, cols], jnp.zeros((tq, 1), F32), c < r, scale,
                              upper2_diag)
        out, carry = _sb_tile(q, earlier(kp_ref, first_off, h), earlier(vp_ref, first_off, h), carry,
                              nprev > 0, scale, upper2)
        states.append((q, acc + out, carry))

    def cond(state):
        j, _, _, cmax = state
        return jnp.logical_and(j >= 0, cmax > SB_EXIT)

    for h, (q, acc, carry) in enumerate(states):
        cols = slice(h * d, (h + 1) * d)

        def body(state, q=q, h=h):
            j, acc, carry, _ = state
            off = pl.multiple_of(j * tk, tk)
            out, carry = _sb_tile(q, earlier(kp_ref, off, h), earlier(vp_ref, off, h), carry, None, scale, upper2)
            return j - 1, acc + out, carry, jnp.max(carry)

        first = jnp.asarray(nprev - 2, jnp.int32)
        _, acc, _, _ = lax.while_loop(cond, body, (first, acc, carry, jnp.max(carry)))
        o_ref[:, cols] = acc.astype(o_ref.dtype)


def _stick_breaking(proj, n, l, heads, hd, qcol, kcol, vcol, prev=None):
    if prev is None:
        hb = 2
        tq = _pick(l, (256, 128, 64, 32, 16, 8))
        tk = tq
        nq = l // tq
        kp, vp = proj, proj
        kp_spec = pl.BlockSpec((l, hb * hd), lambda b, h, i: (b, kcol // hb + h))
        vp_spec = pl.BlockSpec((l, hb * hd), lambda b, h, i: (b, vcol // hb + h))
        prev_tiles = lambda i: i
    else:
        hb = heads
        kc, vc, layer = prev
        past = kc.shape[1]
        tq, nq = l, 1
        tk = _pick(past, (256, 128, 64, 32, 16, 8))
        kp, vp = kc, vc
        kp_spec = pl.BlockSpec((None, past, heads, hd), lambda b, h, i: (layer * n + b, 0, 0, 0))
        vp_spec = pl.BlockSpec((None, past, heads, hd), lambda b, h, i: (layer * n + b, 0, 0, 0))
        prev_tiles = lambda i: past // tk
    assert heads % hb == 0 and qcol % hb == 0 and kcol % hb == 0 and vcol % hb == 0
    w = hb * hd
    return pl.pallas_call(
        functools.partial(_sb_body, tk=tk, prev_tiles=prev_tiles, hb=hb), grid=(n, heads // hb, nq),
        in_specs=[
            pl.BlockSpec((tq, w), lambda b, h, i: (b * nq + i, qcol // hb + h)),
            pl.BlockSpec((tq, w), lambda b, h, i: (b * nq + i, kcol // hb + h)),
            pl.BlockSpec((tq, w), lambda b, h, i: (b * nq + i, vcol // hb + h)),
            kp_spec, vp_spec,
        ],
        out_specs=pl.BlockSpec((tq, w), lambda b, h, i: (b * nq + i, h)),
        out_shape=jax.ShapeDtypeStruct((n * l, heads * hd), BF16),
        compiler_params=_params("parallel", "parallel", "arbitrary"), name="stick_breaking",
    )(proj, proj, proj, kp, vp)


def _gla_body(q_ref, k_ref, v_ref, g_ref, lr_ref, wlr_ref, blr_ref, nw_ref, s0_ref, y_ref, so_ref, st_ref, *, heads):
    t = pl.program_id(1)
    c = q_ref.shape[0]

    @pl.when(t == 0)
    def _():
        st_ref[...] = s0_ref[...]

    gate_in = jnp.dot(_bf(lr_ref[...]), _bf(wlr_ref[...]), preferred_element_type=F32) + blr_ref[...]
    logg = _log_sigmoid(gate_in) * (1.0 / GLA_GATE_NORM)
    rt = lax.broadcasted_iota(jnp.int32, (c, c), 0)
    cs = lax.broadcasted_iota(jnp.int32, (c, c), 1)
    lower = jnp.where(cs <= rt, 1.0, 0.0).astype(BF16)
    hi = logg.astype(BF16)
    lo = (logg - hi.astype(F32)).astype(BF16)
    b_all = jnp.dot(lower, hi, preferred_element_type=F32) + jnp.dot(lower, lo, preferred_element_type=F32)
    last = t == pl.num_programs(1) - 1
    for hh in range(heads):
        _gla_head(hh, b_all, q_ref, k_ref, v_ref, g_ref, nw_ref, y_ref, so_ref, st_ref, last, heads)


def _gla_head(hh, b_all, q_ref, k_ref, v_ref, g_ref, nw_ref, y_ref, so_ref, st_ref, last, heads):
    c = q_ref.shape[0]
    dk = q_ref.shape[1] // heads
    dv = v_ref.shape[1] // heads
    q = q_ref[:, hh * dk:(hh + 1) * dk] * (dk ** -0.5)
    k = k_ref[:, hh * dk:(hh + 1) * dk]
    vb = _bf(v_ref[:, hh * dv:(hh + 1) * dv])
    b = b_all[:, hh * dk:(hh + 1) * dk]
    st = st_ref[hh]
    o_inter = lax.dot_general(_bf(q * jnp.exp(b)), _bf(st), _NT, preferred_element_type=F32)
    row = lax.broadcasted_iota(jnp.int32, (GLA_SUB, 1), 0)
    lane = lax.broadcasted_iota(jnp.int32, (GLA_SUB, GLA_SUB), 1)
    outs = []
    for blk in range(c // GLA_SUB):
        r0, r1 = blk * GLA_SUB, (blk + 1) * GLA_SUB
        bi, qi, ki = b[r0:r1], q[r0:r1], k[r0:r1]
        att = jnp.zeros((GLA_SUB, GLA_SUB), F32)
        for s in range(GLA_SUB):
            dec = jnp.exp(jnp.where(row >= s, bi - bi[s:s + 1, :], NEG))
            col = jnp.sum(qi * dec * ki[s:s + 1, :], axis=1, keepdims=True)
            att = jnp.where(lane == s, col, att)
        oi = jnp.dot(_bf(att), vb[r0:r1], preferred_element_type=F32)
        if blk > 0:
            bref = b[r0 - 1:r0, :]
            qt = qi * jnp.exp(bi - bref)
            kt = k[0:r0] * jnp.exp(bref - b[0:r0])
            a_off = lax.dot_general(_bf(qt), _bf(kt), _NT, preferred_element_type=F32)
            oi = oi + jnp.dot(_bf(a_off), vb[0:r0], preferred_element_type=F32)
        outs.append(oi)
    o = (outs[0] if len(outs) == 1 else jnp.concatenate(outs, axis=0)) + o_inter
    o = o * lax.rsqrt(jnp.mean(o * o, axis=-1, keepdims=True) + RMS_EPS) * nw_ref[...]
    y_ref[:, hh * dv:(hh + 1) * dv] = (o * _silu(g_ref[:, hh * dv:(hh + 1) * dv])).astype(y_ref.dtype)

    bl = b[c - 1:c, :]
    kh = k * jnp.exp(bl - b)
    new = st * jnp.exp(bl) + lax.dot_general(vb, _bf(kh), _TN, preferred_element_type=F32)
    st_ref[hh] = new

    @pl.when(last)
    def _():
        so_ref[hh] = new


def _gla(proj, lr, n, l, heads, dk, dv, w_lr, b_lr, norm_w, layer, s0t):
    c = min(CHUNK, l)
    nc = l // c
    wk, wv = heads * dk, heads * dv
    assert (2 * wk) % wv == 0
    v_blk = (2 * wk) // wv
    rank_pad = lr.shape[1]
    y, so = pl.pallas_call(
        functools.partial(_gla_body, heads=heads), grid=(n, nc),
        in_specs=[
            pl.BlockSpec((c, wk), lambda b, t: (b * nc + t, 0)),
            pl.BlockSpec((c, wk), lambda b, t: (b * nc + t, 1)),
            pl.BlockSpec((c, wv), lambda b, t: (b * nc + t, v_blk)),
            pl.BlockSpec((c, wv), lambda b, t: (b * nc + t, v_blk + 1)),
            pl.BlockSpec((c, rank_pad), lambda b, t: (b * nc + t, 0)),
            pl.BlockSpec((None, rank_pad, wk), lambda b, t: (layer, 0, 0)),
            pl.BlockSpec((None, 1, wk), lambda b, t: (layer, 0, 0)),
            pl.BlockSpec((None, 1, dv), lambda b, t: (layer, 0, 0)),
            pl.BlockSpec((heads, dv, dk), lambda b, t: (b, 0, 0)),
        ],
        out_specs=[
            pl.BlockSpec((c, wv), lambda b, t: (b * nc + t, 0)),
            pl.BlockSpec((heads, dv, dk), lambda b, t: (b, 0, 0)),
        ],
        out_shape=[jax.ShapeDtypeStruct((n * l, wv), BF16),
                   jax.ShapeDtypeStruct((n * heads, dv, dk), F32)],
        scratch_shapes=[pltpu.VMEM((heads, dv, dk), F32)],
        compiler_params=_params("parallel", "arbitrary"), name="gla",
    )(proj, proj, proj, proj, lr, w_lr, b_lr, norm_w, s0t)
    return y, so


def _band_body(q_ref, k_ref, v_ref, bias_ref, o_ref, *, window, past, hb):
    tq = q_ref.shape[0]
    d = q_ref.shape[1] // hb
    qs = pl.program_id(2) * tq
    starts = [pl.multiple_of(jnp.maximum(qs - past + blk * tq, 0), tq) for blk in range(window // tq)]
    exists = lax.broadcasted_iota(jnp.int32, (tq, window), 1) >= past - qs
    for h in range(hb):
        cols = slice(h * d, (h + 1) * d)
        q = _bf(q_ref[:, cols])
        kw = jnp.concatenate([_bf(k_ref[pl.ds(st, tq), cols]) for st in starts], axis=0)
        vw = jnp.concatenate([_bf(v_ref[pl.ds(st, tq), cols]) for st in starts], axis=0)
        s = lax.dot_general(q, kw, _NT, preferred_element_type=F32) * (d ** -0.5) + bias_ref[h]
        s = jnp.where(exists, s, NEG)
        p = jnp.exp(s - jnp.max(s, axis=-1, keepdims=True))
        o = jnp.dot(p.astype(BF16), vw, preferred_element_type=F32) / jnp.sum(p, axis=-1, keepdims=True)
        o_ref[:, cols] = o.astype(o_ref.dtype)


def _toeplitz_bias(table, tq, window, offset, static_mask=None):
    rows, heads = table.shape
    clip = (rows - 1) // 2
    period = tq + window
    m = np.arange(period)
    m = np.where(m >= window, m - period, m)
    idx = np.clip(offset - m, -clip, clip) + clip
    vec = table.astype(F32)[idx, :].T
    flat = jnp.tile(vec, (1, tq))[:, :tq * (period - 1)]
    bias = flat.reshape(heads, tq, period - 1)[:, :, :window]
    if static_mask is not None:
        bias = jnp.where(static_mask[None], bias, NEG)
    return bias


def _band(q_arr, qcol, k_arr, v_arr, kcol, vcol, n, l, heads, hd, bias, tq, window, past, out_rows):
    nq = l // tq
    lk = k_arr.shape[0] // n
    hb = 2
    assert heads % hb == 0 and qcol % hb == 0 and kcol % hb == 0 and vcol % hb == 0
    w = hb * hd
    return pl.pallas_call(
        functools.partial(_band_body, window=window, past=past, hb=hb), grid=(n, heads // hb, nq),
        in_specs=[
            pl.BlockSpec((tq, w), lambda b, h, i: (b * nq + i, qcol // hb + h)),
            pl.BlockSpec((lk, w), lambda b, h, i: (b, kcol // hb + h)),
            pl.BlockSpec((lk, w), lambda b, h, i: (b, vcol // hb + h)),
            pl.BlockSpec((hb, tq, window), lambda b, h, i: (h, 0, 0)),
        ],
        out_specs=pl.BlockSpec((tq, w), lambda b, h, i: (b * nq + i, h)),
        out_shape=jax.ShapeDtypeStruct((out_rows, heads * hd), BF16),
        compiler_params=_params("parallel", "parallel", "arbitrary"), name="band_attention",
    )(q_arr, k_arr, v_arr, bias)


def _band_decode_body(q_ref, kn_ref, vn_ref, kc_ref, vc_ref, bc_ref, bn_ref, o_ref, *, heads):
    d = q_ref.shape[1] // heads
    scale = d ** -0.5
    for h in range(heads):
        cols = slice(h * d, (h + 1) * d)
        q = _bf(q_ref[:, cols])
        sc = lax.dot_general(q, _bf(kc_ref[:, h, :]), _NT, preferred_element_type=F32) * scale + bc_ref[h]
        sn = lax.dot_general(q, _bf(kn_ref[:, cols]), _NT, preferred_element_type=F32) * scale + bn_ref[h]
        m = jnp.maximum(jnp.max(sc, axis=-1, keepdims=True), jnp.max(sn, axis=-1, keepdims=True))
        pc, pn = jnp.exp(sc - m), jnp.exp(sn - m)
        o = (jnp.dot(pc.astype(BF16), _bf(vc_ref[:, h, :]), preferred_element_type=F32)
             + jnp.dot(pn.astype(BF16), _bf(vn_ref[:, cols]), preferred_element_type=F32))
        o = o / (jnp.sum(pc, axis=-1, keepdims=True) + jnp.sum(pn, axis=-1, keepdims=True))
        o_ref[:, cols] = o.astype(o_ref.dtype)


def _band_decode(projb, n, l, heads, hd, k_cache, v_cache, layer, bias_c, bias_n):
    w = heads * hd
    r = k_cache.shape[1]
    new = lambda blk: pl.BlockSpec((l, w), lambda b: (b, blk))
    cache = pl.BlockSpec((None, r, heads, hd), lambda b: (layer * n + b, 0, 0, 0))
    return pl.pallas_call(
        functools.partial(_band_decode_body, heads=heads), grid=(n,),
        in_specs=[new(0), new(1), new(2), cache, cache,
                  pl.BlockSpec((heads, l, r), lambda b: (0, 0, 0)),
                  pl.BlockSpec((heads, l, l), lambda b: (0, 0, 0))],
        out_specs=pl.BlockSpec((l, w), lambda b: (b, 0)),
        out_shape=jax.ShapeDtypeStruct((n * l, w), BF16),
        compiler_params=_params("parallel"), name="band_decode",
    )(projb, projb, projb, k_cache, v_cache, bias_c, bias_n)


def _kv_rows_body(*refs, nl, heads, hd):
    k_refs, v_refs = refs[0:nl], refs[nl:2 * nl]
    ko_ref, vo_ref = refs[2 * nl], refs[2 * nl + 1]
    layer = pl.program_id(0)
    for li in range(nl):
        @pl.when(layer == li)
        def _(li=li):
            for h in range(heads):
                ko_ref[:, h, :] = k_refs[li][:, h * hd:(h + 1) * hd]
                vo_ref[:, h, :] = v_refs[li][:, h * hd:(h + 1) * hd]


def _kv_rows(projs, kblk, vblk, heads, hd):
    nl = len(projs)
    m = projs[0].shape[0]
    w = heads * hd
    tm = _pick(m, (256, 128, 64, 32, 16, 8))
    nt = m // tm

    def in_spec(li, blk):
        return pl.BlockSpec((tm, w), lambda layer, i: (jnp.clip(i + (layer - li) * nt, 0, nt - 1), blk))

    out_spec = pl.BlockSpec((None, tm, heads, hd), lambda layer, i: (layer, i, 0, 0))
    out_sds = jax.ShapeDtypeStruct((nl, m, heads, hd), F32)
    return pl.pallas_call(
        functools.partial(_kv_rows_body, nl=nl, heads=heads, hd=hd), grid=(nl, nt),
        in_specs=[in_spec(li, kblk) for li in range(nl)] + [in_spec(li, vblk) for li in range(nl)],
        out_specs=[out_spec, out_spec], out_shape=[out_sds, out_sds],
        compiler_params=_params("arbitrary", "arbitrary"), name="kv_rows",
    )(*projs, *projs)


def _trunk(x, n, l, mods, st, p, dims):
    m, d = x.shape
    depth = p['w_mod'].shape[0]
    alpha = (2 * depth) ** 0.25
    expanded = l % 256 != 0
    tm_big = m if expanded else _pick(l, (1024, 512, 256))
    tm_row = m if expanded else 256
    d_ff = p['ffn_w_gate'].shape[2]
    g_, p_, hg = dims['s5']
    mix_a = g_ * hg
    sb_heads, sb_hd = dims['sb']
    gla_heads, gla_dk, gla_dv = dims['gla']
    ca_heads, ca_hd = dims['ca']
    sb_w = sb_heads * sb_hd
    ca_w = ca_heads * ca_hd
    gla_w = gla_heads * gla_dv

    def mod_of(layer):
        if expanded:
            arr = jnp.repeat(mods[layer], l, axis=0)[None]
        else:
            arr = mods[layer][:, None, :]
        return _Mod(arr, d, l, expanded)

    outs = {k: [] for k in ('s5_re', 's5_im', 'gla', 'band_k', 'band_v')}
    projs_ab = []
    mod = mod_of(0)
    h = _modulate(x, mod, tm_row)
    for layer in range(depth):
        i = layer // 2
        if layer % 2 == 0:
            in_ab = p['ab_w_in'].shape[2]
            proj, = _matmul(h, [(p['ab_w_in'], i, 0)], n=in_ab, tm=tm_big, tn=256, tk=d, out_dtypes=[F32],
                            epilogue=_epi_identity, name="ab_in")
            gb = g_ // S5_LANE_GROUPS
            sw = S5_LANE_GROUPS * p_
            x0 = jnp.concatenate([st['s5_re'][i].reshape(n * gb, 1, sw), st['s5_im'][i].reshape(n * gb, 1, sw)], axis=2)
            t_rows = _pick(l, (256, 128, 64, 32, 16, 8))
            tables = _s5_tables(*p['s5_raw'][i], seg=t_rows // SUBLANES)
            z, xo = _s5(proj, n, l, tables, p['s5_d'], i, x0, t_rows)
            outs['s5_re'].append(xo[:, 0, :sw].reshape(n, g_, p_))
            outs['s5_im'].append(xo[:, 0, sw:].reshape(n, g_, p_))
            tn_glu = 256
            ya, = _matmul(z, [(p['s5_w_glu'], i, 0)], n=mix_a, tm=tm_big, tn=tn_glu, tk=mix_a, out_dtypes=[BF16],
                          epilogue=_epi_glu, name="s5_glu",
                          extras=[(z, (tm_big, tn_glu), lambda a, b: (a, b)),
                                  (p['s5_b_glu'], (None, 1, tn_glu), lambda a, b, i=i: (i, 0, b))])
            qc, kc, vc = mix_a // sb_hd, (mix_a + sb_w) // sb_hd, (mix_a + 2 * sb_w) // sb_hd
            prev = None if st['sb_k'] is None else (st['sb_k'], st['sb_v'], i)
            yb = _stick_breaking(proj, n, l, sb_heads, sb_hd, qc, kc, vc, prev)
            projs_ab.append(proj)
            merged = [ya, yb]
            w_out = p['ab_w_out']
        else:
            n_gla = 2 * gla_heads * gla_dk + 2 * gla_w
            proj, = _matmul(h, [(p['cd_w_gla'][i], None, 0)], n=n_gla, tm=tm_big, tn=256, tk=d, out_dtypes=[F32],
                            epilogue=_epi_identity, name="cd_in_gla")
            projb, = _matmul(h, [(p['cd_w_band'][i], None, 0)], n=3 * ca_w, tm=tm_big, tn=256, tk=d,
                             out_dtypes=[F32], epilogue=_epi_identity, name="cd_in_band")
            lr, = _matmul(h, [(p['cd_w_lrin'][i], None, 0)], n=LANES, tm=tm_big, tn=LANES, tk=d, out_dtypes=[F32],
                          epilogue=_epi_identity, name="cd_lr")
            s0t = jnp.swapaxes(st['gla'][i], -1, -2).reshape(n * gla_heads, gla_dv, gla_dk)
            yc, so = _gla(proj, lr, n, l, gla_heads, gla_dk, gla_dv, p['gla_w_lr_pad'], p['gla_b_lr'],
                          p['gla_norm_w'], i, s0t)
            outs['gla'].append(jnp.swapaxes(so.reshape(n, gla_heads, gla_dv, gla_dk), -1, -2))
            ka = projb[:, ca_w:2 * ca_w]
            va = projb[:, 2 * ca_w:3 * ca_w]
            table = p['ca_rel_bias'][i]
            if st['band_k'] is None:
                tq = _pick(l, (256, 128, 64))
                window = tq + BAND_PAST
                qchunk = np.arange(tq)[:, None] // CHUNK
                kchunk = np.arange(window)[None, :] // CHUNK
                static_mask = (kchunk >= qchunk) & (kchunk <= qchunk + BAND_PREV)
                bias = _toeplitz_bias(table, tq, window, BAND_PAST, static_mask)
                yd = _band(projb, 0, projb, projb, ca_heads, 2 * ca_heads, n, l, ca_heads, ca_hd, bias, tq, window,
                           BAND_PAST, m)
                keep = min(BAND_PAST, l)
                outs['band_k'].append(ka.reshape(n, l, ca_heads, ca_hd)[:, l - keep:])
                outs['band_v'].append(va.reshape(n, l, ca_heads, ca_hd)[:, l - keep:])
            else:
                r = st['band_k'].shape[1]
                bias = _toeplitz_bias(table, l, r + l, r)
                yd = _band_decode(projb, n, l, ca_heads, ca_hd, st['band_k'], st['band_v'], i,
                                  bias[:, :, :r], bias[:, :, r:])
                outs['band_k'].append(ka.reshape(n, l, ca_heads, ca_hd))
                outs['band_v'].append(va.reshape(n, l, ca_heads, ca_hd))
            merged = [yc, yd]
            w_out = p['cd_w_out']
        f, = _matmul(merged, [(w_out, i, 0)], n=d, tm=tm_big, tn=256, tk=sum(y.shape[1] for y in merged),
                     out_dtypes=[F32], epilogue=_epi_identity, name="mix_out")
        x, h = _deepnorm(x, f, mod, 2, p['ln1_w'], p['ln1_b'], layer, alpha, tm_row, nxt=(mod, 4, 3))
        tn_ff = _pick(d_ff, (256, 128))
        act, = _matmul(h, [(p['ffn_w_gate'], layer, 0), (p['ffn_w_up'], layer, 0)], n=d_ff, tm=tm_big, tn=tn_ff,
                       tk=d, out_dtypes=[BF16], epilogue=_epi_swiglu, name="ffn_in")
        ff, = _matmul(act, [(p['ffn_w_down'], layer, 0)], n=d, tm=min(tm_big, 512), tn=256, tk=d_ff,
                      out_dtypes=[F32], epilogue=_epi_identity, name="ffn_out")
        if layer + 1 < depth:
            nmod = mod_of(layer + 1)
            x, h = _deepnorm(x, ff, mod, 5, p['ln2_w'], p['ln2_b'], layer, alpha, tm_row, nxt=(nmod, 1, 0))
            mod = nmod
        else:
            x, _ = _deepnorm(x, ff, mod, 5, p['ln2_w'], p['ln2_b'], layer, alpha, tm_row)
    outs = {k: jnp.stack(v) for k, v in outs.items()}
    assert mix_a % sb_w == 0
    sb_k, sb_v = _kv_rows(projs_ab, mix_a // sb_w + 1, mix_a // sb_w + 2, sb_heads, sb_hd)
    outs['sb_k'] = sb_k.reshape(len(projs_ab), n, l, sb_heads, sb_hd)
    outs['sb_v'] = sb_v.reshape(len(projs_ab), n, l, sb_heads, sb_hd)
    return x, outs


def kernel(x_prompt, x_sample, state_s5_re, state_s5_im, cache_sb_k, cache_sb_v, state_gla, cache_band_k, cache_band_v, c_prompt, c_sample, w_mod, b_mod, ln1_w, ln1_b, ln2_w, ln2_b, ab_w_in, ab_w_out, s5_a_re, s5_a_im, s5_log_dt, s5_b_re, s5_b_im, s5_c_re, s5_c_im, s5_d, s5_w_glu, s5_b_glu, cd_w_in, cd_w_out, gla_w_lr, gla_b_lr, gla_norm_w, ca_rel_bias, ffn_w_gate, ffn_w_up, ffn_w_down):
    nb, seq, d = x_prompt.shape
    db, dseq, _ = x_sample.shape
    depth = w_mod.shape[0]
    n_even, g_, p_ = s5_a_re.shape
    hg = s5_b_re.shape[-1]
    _, _, past, sb_heads, sb_hd = cache_sb_k.shape
    n_odd, _, gla_heads, gla_dk, gla_dv = state_gla.shape
    _, _, band_rows, ca_heads, ca_hd = cache_band_k.shape
    rank = gla_w_lr.shape[1]
    dims = {'s5': (g_, p_, hg), 'sb': (sb_heads, sb_hd), 'gla': (gla_heads, gla_dk, gla_dv), 'ca': (ca_heads, ca_hd)}

    lr0 = 2 * gla_heads * gla_dk + 2 * gla_heads * gla_dv
    p = dict(
        w_mod=w_mod, ln1_w=ln1_w[:, None, :], ln1_b=ln1_b[:, None, :], ln2_w=ln2_w[:, None, :], ln2_b=ln2_b[:, None, :],
        ab_w_in=ab_w_in, ab_w_out=ab_w_out, s5_d=s5_d[:, None, :], s5_w_glu=s5_w_glu, s5_b_glu=s5_b_glu[:, None, :],
        cd_w_out=cd_w_out, gla_b_lr=gla_b_lr[:, None, :], gla_norm_w=gla_norm_w[:, None, :], ca_rel_bias=ca_rel_bias,
        ffn_w_gate=ffn_w_gate, ffn_w_up=ffn_w_up, ffn_w_down=ffn_w_down.astype(BF16),
        cd_w_gla=[cd_w_in[i, :, :lr0] for i in range(n_odd)],
        cd_w_band=[cd_w_in[i, :, lr0 + rank:] for i in range(n_odd)],
        cd_w_lrin=[jnp.pad(cd_w_in[i, :, lr0:lr0 + rank], ((0, 0), (0, LANES - rank))) for i in range(n_odd)],
        gla_w_lr_pad=jnp.pad(gla_w_lr, ((0, 0), (0, LANES - rank), (0, 0))),
        s5_raw=[(s5_a_re[i], s5_a_im[i], s5_log_dt[i], s5_b_re[i], s5_b_im[i], s5_c_re[i], s5_c_im[i])
                for i in range(n_even)],
    )

    rows = nb + db
    rows_pad = -(-rows // 16) * 16
    c_all = jnp.concatenate([c_prompt, c_sample, jnp.zeros((rows_pad - rows, d), F32)], axis=0)
    mods = []
    for layer in range(depth):
        tn_mod = _pick(6 * d, (512, 256, 128))
        mod_l, = _matmul(c_all, [(w_mod, layer, 0)], n=6 * d, tm=rows_pad, tn=tn_mod, tk=d, out_dtypes=[F32],
                         epilogue=_epi_bias, prologue=_silu, name="adaln_mod",
                         extras=[(b_mod[:, None, :], (None, 1, tn_mod), lambda a, b, layer=layer: (layer, 0, b))])
        mods.append(mod_l)
    mods = jnp.stack(mods)
    mods_p, mods_s = mods[:, :nb], mods[:, nb:rows]

    zeros_s5 = jnp.zeros((n_even, nb, g_, p_), F32)
    st_p = dict(s5_re=zeros_s5, s5_im=zeros_s5, sb_k=None, sb_v=None,
                gla=jnp.zeros((n_odd, nb, gla_heads, gla_dk, gla_dv), F32), band_k=None, band_v=None)
    y_p, o_p = _trunk(x_prompt.reshape(nb * seq, d), nb, seq, mods_p, st_p, p, dims)
    st_s = dict(s5_re=state_s5_re, s5_im=state_s5_im,
                sb_k=cache_sb_k.reshape(n_even * db, past, sb_heads, sb_hd),
                sb_v=cache_sb_v.reshape(n_even * db, past, sb_heads, sb_hd),
                gla=state_gla,
                band_k=cache_band_k.reshape(n_odd * db, band_rows, ca_heads, ca_hd),
                band_v=cache_band_v.reshape(n_odd * db, band_rows, ca_heads, ca_hd))
    y_s, o_s = _trunk(x_sample.reshape(db * dseq, d), db, dseq, mods_s, st_s, p, dims)
    return (y_p.reshape(nb, seq, d), y_s.reshape(db, dseq, d),
            o_p['s5_re'], o_p['s5_im'], o_p['sb_k'], o_p['sb_v'], o_p['gla'], o_p['band_k'], o_p['band_v'],
            o_s['s5_re'], o_s['s5_im'], o_s['sb_k'], o_s['sb_v'], o_s['gla'], o_s['band_k'], o_s['band_v'])
```

```python
import functools
import math

import numpy as np
import jax
import jax.numpy as jnp
from jax import lax
from jax.experimental import pallas as pl
from jax.experimental.pallas import tpu as pltpu

F32 = jnp.float32
BF16 = jnp.bfloat16

CHUNK = 64
BAND_PREV = 8
BAND_PAST = BAND_PREV * CHUNK
GLA_GATE_NORM = 16.0
LN_EPS = 1e-5
RMS_EPS = 1e-6

LANES = 128
SUBLANES = 8
V7X_VMEM_BYTES = 64 * 1024 * 1024
VMEM_LIMIT = V7X_VMEM_BYTES - 8 * 1024 * 1024

GLA_SUB = 16
S5_LANE_GROUPS = 8
NEG = -1e30
SB_EXIT = -104.0

_NT = (((1,), (1,)), ((), ()))
_TN = (((0,), (0,)), ((), ()))


def _pick(dim, candidates):
    for c in candidates:
        if c <= dim and dim % c == 0:
            return c
    return dim


def _params(*sem):
    return pltpu.CompilerParams(dimension_semantics=sem, vmem_limit_bytes=VMEM_LIMIT)


def _bf(x):
    return x if x.dtype == BF16 else x.astype(BF16)


def _log_sigmoid(z):
    return jnp.minimum(z, 0.0) - jnp.log(1.0 + jnp.exp(-jnp.abs(z)))


def _silu(x):
    return x * jax.nn.sigmoid(x)


def _mm_body(*refs, na, nb, ne, no, nk, prologue, epilogue):
    a_refs, refs = refs[:na], refs[na:]
    b_refs = refs[0:nb]
    e_refs = refs[nb:nb + ne]
    o_refs = refs[nb + ne:nb + ne + no]
    acc_refs = refs[nb + ne + no:]
    a_parts = [a_ref[...] for a_ref in a_refs]
    if prologue is not None:
        a_parts = [prologue(a) for a in a_parts]
    a_parts = [_bf(a) for a in a_parts]
    a = a_parts[0] if na == 1 else jnp.concatenate(a_parts, axis=1)
    parts = [jnp.dot(a, _bf(b_ref[...]), preferred_element_type=F32) for b_ref in b_refs]

    def finish(accs):
        outs = epilogue(accs, [e[...] for e in e_refs])
        for o_ref, o in zip(o_refs, outs):
            o_ref[...] = o.astype(o_ref.dtype)

    if nk == 1:
        finish(parts)
        return
    k = pl.program_id(2)

    @pl.when(k == 0)
    def _():
        for acc, p in zip(acc_refs, parts):
            acc[...] = p

    @pl.when(k > 0)
    def _():
        for acc, p in zip(acc_refs, parts):
            acc[...] += p

    @pl.when(k == nk - 1)
    def _():
        finish([acc[...] for acc in acc_refs])


def _matmul(a, bs, *, n, tm, tn, tk, out_dtypes, epilogue, extras=(), prologue=None, name="matmul"):
    a_list = list(a) if isinstance(a, (list, tuple)) else [a]
    m = a_list[0].shape[0]
    kdim = sum(x.shape[1] for x in a_list)
    nk = kdim // tk
    assert len(a_list) == 1 or nk == 1
    grid = (m // tm, n // tn, nk)
    if len(a_list) == 1:
        in_specs = [pl.BlockSpec((tm, tk), lambda i, j, k: (i, k))]
    else:
        in_specs = [pl.BlockSpec((tm, x.shape[1]), lambda i, j, k: (i, 0)) for x in a_list]
    operands = list(a_list)
    for arr, layer, c0 in bs:
        if arr.ndim == 3:
            in_specs.append(pl.BlockSpec((None, tk, tn), lambda i, j, k, layer=layer, c0=c0: (layer, k, c0 + j)))
        else:
            in_specs.append(pl.BlockSpec((tk, tn), lambda i, j, k, c0=c0: (k, c0 + j)))
        operands.append(arr)
    for arr, shape, imap in extras:
        in_specs.append(pl.BlockSpec(shape, lambda i, j, k, imap=imap: imap(i, j)))
        operands.append(arr)
    out_shape = [jax.ShapeDtypeStruct((m, n), dt) for dt in out_dtypes]
    out_specs = [pl.BlockSpec((tm, tn), lambda i, j, k: (i, j)) for _ in out_dtypes]
    scratch = [pltpu.VMEM((tm, tn), F32) for _ in bs] if nk > 1 else []
    body = functools.partial(_mm_body, na=len(a_list), nb=len(bs), ne=len(extras), no=len(out_dtypes), nk=nk,
                             prologue=prologue, epilogue=epilogue)
    outs = pl.pallas_call(
        body, grid=grid, in_specs=in_specs, out_specs=out_specs, out_shape=out_shape,
        scratch_shapes=scratch, compiler_params=_params("parallel", "parallel", "arbitrary"), name=name,
    )(*operands)
    return outs


def _epi_identity(accs, extras):
    return [accs[0]]


def _epi_bias(accs, extras):
    return [accs[0] + extras[0]]


def _epi_swiglu(accs, extras):
    return [_silu(accs[0]) * accs[1]]


def _epi_glu(accs, extras):
    z, bias = extras
    return [z * jax.nn.sigmoid(accs[0] + bias)]


class _Mod:
    def __init__(self, arr, d, l, expanded):
        self.arr, self.d, self.l, self.expanded = arr, d, l, expanded

    def spec(self, piece, tm):
        d = self.d
        if self.expanded:
            return pl.BlockSpec((None, tm, d), lambda i, *_: (0, i, piece))
        per = self.l // tm
        return pl.BlockSpec((None, 1, d), lambda i, *_: (i // per, 0, piece))


def _modulate_body(x_ref, sc_ref, sh_ref, h_ref):
    h_ref[...] = (x_ref[...] * (1.0 + sc_ref[...]) + sh_ref[...]).astype(h_ref.dtype)


def _modulate(x, mod, tm):
    m, d = x.shape
    row = pl.BlockSpec((tm, d), lambda i: (i, 0))
    return pl.pallas_call(
        _modulate_body, grid=(m // tm,), in_specs=[row, mod.spec(1, tm), mod.spec(0, tm)],
        out_specs=row, out_shape=jax.ShapeDtypeStruct((m, d), BF16),
        compiler_params=_params("parallel"), name="modulate",
    )(x, mod.arr, mod.arr)


def _ln_body(*refs, alpha, has_next):
    if has_next:
        x_ref, f_ref, g_ref, w_ref, b_ref, sc_ref, sh_ref, xo_ref, ho_ref = refs
    else:
        x_ref, f_ref, g_ref, w_ref, b_ref, xo_ref = refs
    y = alpha * x_ref[...] + (1.0 + g_ref[...]) * f_ref[...]
    yc = y - jnp.mean(y, axis=-1, keepdims=True)
    var = jnp.mean(yc * yc, axis=-1, keepdims=True)
    xn = yc * lax.rsqrt(var + LN_EPS) * w_ref[...] + b_ref[...]
    xo_ref[...] = xn
    if has_next:
        ho_ref[...] = (xn * (1.0 + sc_ref[...]) + sh_ref[...]).astype(ho_ref.dtype)


def _deepnorm(x, f, mod, gate_piece, w, b, layer, alpha, tm, nxt=None):
    m, d = x.shape
    row = pl.BlockSpec((tm, d), lambda i: (i, 0))
    vec = pl.BlockSpec((None, 1, d), lambda i: (layer, 0, 0))
    in_specs = [row, row, mod.spec(gate_piece, tm), vec, vec]
    operands = [x, f, mod.arr, w, b]
    out_shape = [jax.ShapeDtypeStruct((m, d), F32)]
    out_specs = [row]
    if nxt is not None:
        nmod, sc_piece, sh_piece = nxt
        in_specs += [nmod.spec(sc_piece, tm), nmod.spec(sh_piece, tm)]
        operands += [nmod.arr, nmod.arr]
        out_shape.append(jax.ShapeDtypeStruct((m, d), BF16))
        out_specs.append(row)
    outs = pl.pallas_call(
        functools.partial(_ln_body, alpha=alpha, has_next=nxt is not None),
        grid=(m // tm,), in_specs=in_specs, out_specs=out_specs, out_shape=out_shape,
        compiler_params=_params("parallel"), name="deepnorm",
    )(*operands)
    return outs if nxt is not None else (outs[0], None)


def _s5_body(u_ref, bt_ref, ct_ref, sc_ref, d_ref, x0_ref, z_ref, xo_ref, xs_ref, st_ref, *, t_rows):
    t = pl.program_id(2)
    sw = st_ref.shape[-1] // 2

    @pl.when(t == 0)
    def _():
        st_ref[...] = x0_ref[...]

    seg = t_rows // SUBLANES
    u = jnp.concatenate([u_ref[pl.ds(j, SUBLANES, stride=seg), :] for j in range(seg)], axis=0)
    xs_ref[...] = jnp.dot(_bf(u), bt_ref[...], preferred_element_type=F32)
    ar, ai = sc_ref[0:8, :], sc_ref[8:16, :]
    unroll = 2 if seg % 2 == 0 else 1

    def cmul_add(xr, xi, ar, ai, sr, si):
        return xr + ar * sr - ai * si, xi + ar * si + ai * sr

    def local(j, carry):
        off = pl.multiple_of(j * SUBLANES, SUBLANES)
        xr, xi = cmul_add(xs_ref[pl.ds(off, SUBLANES), 0:sw], xs_ref[pl.ds(off, SUBLANES), sw:2 * sw],
                          ar, ai, carry[0], carry[1])
        xs_ref[pl.ds(off, SUBLANES), 0:sw] = xr
        xs_ref[pl.ds(off, SUBLANES), sw:2 * sw] = xi
        return xr, xi

    zero = jnp.zeros((SUBLANES, sw), F32)
    gr, gi = lax.fori_loop(0, seg, local, (zero, zero), unroll=unroll)

    for step, d in enumerate((1, 2, 4)):
        r0 = 16 + 16 * step
        gr, gi = cmul_add(gr, gi, sc_ref[r0:r0 + 8, :], sc_ref[r0 + 8:r0 + 16, :],
                          pltpu.roll(gr, d, 0), pltpu.roll(gi, d, 0))
    st = st_ref[...]
    sr, si = st[:, 0:sw], st[:, sw:2 * sw]
    gr, gi = cmul_add(gr, gi, sc_ref[64:72, :], sc_ref[72:80, :], sr, si)
    fr, fi = gr[SUBLANES - 1:SUBLANES, :], gi[SUBLANES - 1:SUBLANES, :]
    st_ref[:, 0:sw] = fr
    st_ref[:, sw:2 * sw] = fi

    row = lax.broadcasted_iota(jnp.int32, (SUBLANES, sw), 0)
    cr = jnp.where(row == 0, sr, pltpu.roll(gr, 1, 0))
    ci = jnp.where(row == 0, si, pltpu.roll(gi, 1, 0))

    def fix(j, carry):
        off = pl.multiple_of(j * SUBLANES, SUBLANES)
        dr = ar * carry[0] - ai * carry[1]
        di = ar * carry[1] + ai * carry[0]
        xs_ref[pl.ds(off, SUBLANES), 0:sw] += dr
        xs_ref[pl.ds(off, SUBLANES), sw:2 * sw] += di
        return dr, di

    lax.fori_loop(0, seg, fix, (cr, ci), unroll=unroll)

    y = jnp.dot(_bf(xs_ref[...]), ct_ref[...], preferred_element_type=F32) + d_ref[...] * u
    z = 0.5 * y * (1.0 + jnp.tanh(math.sqrt(2.0 / math.pi) * (y + 0.044715 * (y * y * y))))
    for j in range(seg):
        z_ref[pl.ds(j, SUBLANES, stride=seg), :] = z[j * SUBLANES:(j + 1) * SUBLANES, :]

    @pl.when(t == pl.num_programs(2) - 1)
    def _():
        xo_ref[:, 0:sw] = fr
        xo_ref[:, sw:2 * sw] = fi


def _s5_tables(a_re, a_im, log_dt, b_re, b_im, c_re, c_im, seg):
    g, p = a_re.shape
    hg = b_re.shape[-1]
    gb = g // S5_LANE_GROUPS
    lr_, li_ = a_re.astype(F32), a_im.astype(F32)
    dt = jnp.exp(log_dt.astype(F32))[:, None]
    mag = jnp.exp(lr_ * dt)
    ab_r, ab_i = mag * jnp.cos(li_ * dt), mag * jnp.sin(li_ * dt)
    den = lr_ * lr_ + li_ * li_
    cr_ = ((ab_r - 1.0) * lr_ + ab_i * li_) / den
    ci_ = (ab_i * lr_ - (ab_r - 1.0) * li_) / den
    br_, bi_ = b_re.astype(F32), b_im.astype(F32)
    bb_r = cr_[..., None] * br_ - ci_[..., None] * bi_
    bb_i = cr_[..., None] * bi_ + ci_[..., None] * br_
    eye = jnp.eye(S5_LANE_GROUPS, dtype=F32)

    def blockdiag_in(w):
        w = w.reshape(gb, S5_LANE_GROUPS, p, hg)
        return jnp.einsum('bgph,gk->bghkp', w, eye).reshape(gb, S5_LANE_GROUPS * hg, S5_LANE_GROUPS * p)

    def blockdiag_out(w):
        w = w.reshape(gb, S5_LANE_GROUPS, hg, p)
        return jnp.einsum('bgkp,gj->bgpjk', w, eye).reshape(gb, S5_LANE_GROUPS * p, S5_LANE_GROUPS * hg)

    bt = jnp.concatenate([blockdiag_in(bb_r), blockdiag_in(bb_i)], axis=2).astype(BF16)
    ct = jnp.concatenate([blockdiag_out(c_re.astype(F32)), blockdiag_out(-c_im.astype(F32))], axis=1).astype(BF16)
    sw = S5_LANE_GROUPS * p

    def rows8(z):
        return jnp.broadcast_to(z.reshape(gb, 1, sw), (gb, SUBLANES, sw))

    def cmul(x, y):
        return x[0] * y[0] - x[1] * y[1], x[0] * y[1] + x[1] * y[0]

    a_seg, sq, e = None, (ab_r, ab_i), seg
    while e:
        if e & 1:
            a_seg = sq if a_seg is None else cmul(a_seg, sq)
        sq, e = cmul(sq, sq), e >> 1
    pows = [a_seg]
    for _ in range(SUBLANES - 1):
        pows.append(cmul(pows[-1], a_seg))
    pieces = [rows8(ab_r), rows8(ab_i)]
    trow = jnp.arange(SUBLANES)[None, :, None]
    for d in (1, 2, 4):
        pieces += [jnp.where(trow >= d, rows8(pows[d - 1][part]), 0.0) for part in (0, 1)]
    for part in (0, 1):
        pieces.append(jnp.stack([pw[part] for pw in pows], axis=0).reshape(SUBLANES, gb, sw).transpose(1, 0, 2))
    sc = jnp.concatenate(pieces, axis=1)
    return bt, ct, sc


def _s5(proj, n, l, tables, d_skip, layer, x0, t_rows):
    bt, ct, sc = tables
    gb, cw, sw2 = bt.shape
    nt = l // t_rows
    grid = (n, gb, nt)
    z, xo = pl.pallas_call(
        functools.partial(_s5_body, t_rows=t_rows), grid=grid,
        in_specs=[
            pl.BlockSpec((t_rows, cw), lambda b, g, t: (b * nt + t, g)),
            pl.BlockSpec((None, cw, sw2), lambda b, g, t: (g, 0, 0)),
            pl.BlockSpec((None, sw2, cw), lambda b, g, t: (g, 0, 0)),
            pl.BlockSpec((None, 10 * SUBLANES, sw2 // 2), lambda b, g, t: (g, 0, 0)),
            pl.BlockSpec((None, 1, cw), lambda b, g, t: (layer, 0, g)),
            pl.BlockSpec((None, 1, sw2), lambda b, g, t: (b * gb + g, 0, 0)),
        ],
        out_specs=[
            pl.BlockSpec((t_rows, cw), lambda b, g, t: (b * nt + t, g)),
            pl.BlockSpec((None, 1, sw2), lambda b, g, t: (b * gb + g, 0, 0)),
        ],
        out_shape=[jax.ShapeDtypeStruct((n * l, gb * cw), F32),
                   jax.ShapeDtypeStruct((n * gb, 1, sw2), F32)],
        scratch_shapes=[pltpu.VMEM((t_rows, sw2), F32), pltpu.VMEM((1, sw2), F32)],
        compiler_params=_params("parallel", "parallel", "arbitrary"), name="s5_scan",
    )(proj, bt, ct, sc, d_skip, x0)
    return z, xo


def _strict_upper2(tk):
    rj = lax.broadcasted_iota(jnp.int32, (tk, tk), 0)
    cs = lax.broadcasted_iota(jnp.int32, (tk, tk), 1)
    upper = jnp.where(rj > cs, 1.0, 0.0).astype(BF16)
    return jnp.concatenate([upper, upper], axis=0)


def _sb_tile(q, kt, vt, carry, mask, scale, upper2):
    z = lax.dot_general(q, _bf(kt), _NT, preferred_element_type=F32) * scale
    ls = _log_sigmoid(z)
    lf = ls - z
    if mask is not None:
        lf = jnp.where(mask, lf, 0.0)
    hi = lf.astype(BF16)
    lo = (lf - hi.astype(F32)).astype(BF16)
    newer = jnp.dot(jnp.concatenate([hi, lo], axis=1), upper2, preferred_element_type=F32)
    w = jnp.exp(ls + carry + newer)
    if mask is not None:
        w = jnp.where(mask, w, 0.0)
    out = jnp.dot(w.astype(BF16), _bf(vt), preferred_element_type=F32)
    return out, carry + jnp.sum(lf, axis=1, keepdims=True)


def _sb_body(q_ref, kd_ref, vd_ref, kp_ref, vp_ref, o_ref, *, tk, prev_tiles, hb):
    tq = q_ref.shape[0]
    d = q_ref.shape[1] // hb
    scale = d ** -0.5
    r = lax.broadcasted_iota(jnp.int32, (tq, tq), 0)
    c = lax.broadcasted_iota(jnp.int32, (tq, tq), 1)
    upper2 = _strict_upper2(tk)
    upper2_diag = upper2 if tq == tk else _strict_upper2(tq)
    nprev = prev_tiles(pl.program_id(2))
    first_off = pl.multiple_of(jnp.maximum(nprev - 1, 0) * tk, tk)

    def earlier(ref, off, h):
        if len(ref.shape) == 3:
            return ref[pl.ds(off, tk), h, :]
        return ref[pl.ds(off, tk), h * d:(h + 1) * d]

    states = []
    for h in range(hb):
        cols = slice(h * d, (h + 1) * d)
        q = _bf(q_ref[:, cols])
        acc, carry = _sb_tile(q, kd_ref[:, cols], vd_ref[:, cols], jnp.zeros((tq, 1), F32), c < r, scale,
                              upper2_diag)
        out, carry = _sb_tile(q, earlier(kp_ref, first_off, h), earlier(vp_ref, first_off, h), carry,
                              nprev > 0, scale, upper2)
        states.append((q, acc + out, carry))

    def cond(state):
        j, _, _, cmax = state
        return jnp.logical_and(j >= 0, cmax > SB_EXIT)

    for h, (q, acc, carry) in enumerate(states):
        cols = slice(h * d, (h + 1) * d)

        def body(state, q=q, h=h):
            j, acc, carry, _ = state
            off = pl.multiple_of(j * tk, tk)
            out, carry = _sb_tile(q, earlier(kp_ref, off, h), earlier(vp_ref, off, h), carry, None, scale, upper2)
            return j - 1, acc + out, carry, jnp.max(carry)

        first = jnp.asarray(nprev - 2, jnp.int32)
        _, acc, _, _ = lax.while_loop(cond, body, (first, acc, carry, jnp.max(carry)))
        o_ref[:, cols] = acc.astype(o_ref.dtype)


def _stick_breaking(proj, n, l, heads, hd, qcol, kcol, vcol, prev=None):
    if prev is None:
        hb = 2
        tq = _pick(l, (256, 128, 64, 32, 16, 8))
        tk = tq
        nq = l // tq
        kp, vp = proj, proj
        kp_spec = pl.BlockSpec((l, hb * hd), lambda b, h, i: (b, kcol // hb + h))
        vp_spec = pl.BlockSpec((l, hb * hd), lambda b, h, i: (b, vcol // hb + h))
        prev_tiles = lambda i: i
    else:
        hb = heads
        kc, vc, layer = prev
        past = kc.shape[1]
        tq, nq = l, 1
        tk = _pick(past, (256, 128, 64, 32, 16, 8))
        kp, vp = kc, vc
        kp_spec = pl.BlockSpec((None, past, heads, hd), lambda b, h, i: (layer * n + b, 0, 0, 0))
        vp_spec = pl.BlockSpec((None, past, heads, hd), lambda b, h, i: (layer * n + b, 0, 0, 0))
        prev_tiles = lambda i: past // tk
    assert heads % hb == 0 and qcol % hb == 0 and kcol % hb == 0 and vcol % hb == 0
    w = hb * hd
    return pl.pallas_call(
        functools.partial(_sb_body, tk=tk, prev_tiles=prev_tiles, hb=hb), grid=(n, heads // hb, nq),
        in_specs=[
            pl.BlockSpec((tq, w), lambda b, h, i: (b * nq + i, qcol // hb + h)),
            pl.BlockSpec((tq, w), lambda b, h, i: (b * nq + i, kcol // hb + h)),
            pl.BlockSpec((tq, w), lambda b, h, i: (b * nq + i, vcol // hb + h)),
            kp_spec, vp_spec,
        ],
        out_specs=pl.BlockSpec((tq, w), lambda b, h, i: (b * nq + i, h)),
        out_shape=jax.ShapeDtypeStruct((n * l, heads * hd), BF16),
        compiler_params=_params("parallel", "parallel", "arbitrary"), name="stick_breaking",
    )(proj, proj, proj, kp, vp)


def _gla_body(q_ref, k_ref, v_ref, g_ref, lr_ref, wlr_ref, blr_ref, nw_ref, s0_ref, y_ref, so_ref, st_ref, *, heads):
    t = pl.program_id(1)
    c = q_ref.shape[0]

    @pl.when(t == 0)
    def _():
        st_ref[...] = s0_ref[...]

    gate_in = jnp.dot(_bf(lr_ref[...]), _bf(wlr_ref[...]), preferred_element_type=F32) + blr_ref[...]
    logg = _log_sigmoid(gate_in) * (1.0 / GLA_GATE_NORM)
    rt = lax.broadcasted_iota(jnp.int32, (c, c), 0)
    cs = lax.broadcasted_iota(jnp.int32, (c, c), 1)
    lower = jnp.where(cs <= rt, 1.0, 0.0).astype(BF16)
    hi = logg.astype(BF16)
    lo = (logg - hi.astype(F32)).astype(BF16)
    b_all = jnp.dot(lower, hi, preferred_element_type=F32) + jnp.dot(lower, lo, preferred_element_type=F32)
    last = t == pl.num_programs(1) - 1
    for hh in range(heads):
        _gla_head(hh, b_all, q_ref, k_ref, v_ref, g_ref, nw_ref, y_ref, so_ref, st_ref, last, heads)


def _gla_head(hh, b_all, q_ref, k_ref, v_ref, g_ref, nw_ref, y_ref, so_ref, st_ref, last, heads):
    c = q_ref.shape[0]
    dk = q_ref.shape[1] // heads
    dv = v_ref.shape[1] // heads
    q = q_ref[:, hh * dk:(hh + 1) * dk] * (dk ** -0.5)
    k = k_ref[:, hh * dk:(hh + 1) * dk]
    vb = _bf(v_ref[:, hh * dv:(hh + 1) * dv])
    b = b_all[:, hh * dk:(hh + 1) * dk]
    st = st_ref[hh]
    o_inter = lax.dot_general(_bf(q * jnp.exp(b)), _bf(st), _NT, preferred_element_type=F32)
    row = lax.broadcasted_iota(jnp.int32, (GLA_SUB, 1), 0)
    lane = lax.broadcasted_iota(jnp.int32, (GLA_SUB, GLA_SUB), 1)
    outs = []
    for blk in range(c // GLA_SUB):
        r0, r1 = blk * GLA_SUB, (blk + 1) * GLA_SUB
        bi, qi, ki = b[r0:r1], q[r0:r1], k[r0:r1]
        att = jnp.zeros((GLA_SUB, GLA_SUB), F32)
        for s in range(GLA_SUB):
            dec = jnp.exp(jnp.where(row >= s, bi - bi[s:s + 1, :], NEG))
            col = jnp.sum(qi * dec * ki[s:s + 1, :], axis=1, keepdims=True)
            att = jnp.where(lane == s, col, att)
        oi = jnp.dot(_bf(att), vb[r0:r1], preferred_element_type=F32)
        if blk > 0:
            bref = b[r0 - 1:r0, :]
            qt = qi * jnp.exp(bi - bref)
            kt = k[0:r0] * jnp.exp(bref - b[0:r0])
            a_off = lax.dot_general(_bf(qt), _bf(kt), _NT, preferred_element_type=F32)
            oi = oi + jnp.dot(_bf(a_off), vb[0:r0], preferred_element_type=F32)
        outs.append(oi)
    o = (outs[0] if len(outs) == 1 else jnp.concatenate(outs, axis=0)) + o_inter
    o = o * lax.rsqrt(jnp.mean(o * o, axis=-1, keepdims=True) + RMS_EPS) * nw_ref[...]
    y_ref[:, hh * dv:(hh + 1) * dv] = (o * _silu(g_ref[:, hh * dv:(hh + 1) * dv])).astype(y_ref.dtype)

    bl = b[c - 1:c, :]
    kh = k * jnp.exp(bl - b)
    new = st * jnp.exp(bl) + lax.dot_general(vb, _bf(kh), _TN, preferred_element_type=F32)
    st_ref[hh] = new

    @pl.when(last)
    def _():
        so_ref[hh] = new


def _gla(proj, lr, n, l, heads, dk, dv, w_lr, b_lr, norm_w, layer, s0t):
    c = min(CHUNK, l)
    nc = l // c
    wk, wv = heads * dk, heads * dv
    assert (2 * wk) % wv == 0
    v_blk = (2 * wk) // wv
    rank_pad = lr.shape[1]
    y, so = pl.pallas_call(
        functools.partial(_gla_body, heads=heads), grid=(n, nc),
        in_specs=[
            pl.BlockSpec((c, wk), lambda b, t: (b * nc + t, 0)),
            pl.BlockSpec((c, wk), lambda b, t: (b * nc + t, 1)),
            pl.BlockSpec((c, wv), lambda b, t: (b * nc + t, v_blk)),
            pl.BlockSpec((c, wv), lambda b, t: (b * nc + t, v_blk + 1)),
            pl.BlockSpec((c, rank_pad), lambda b, t: (b * nc + t, 0)),
            pl.BlockSpec((None, rank_pad, wk), lambda b, t: (layer, 0, 0)),
            pl.BlockSpec((None, 1, wk), lambda b, t: (layer, 0, 0)),
            pl.BlockSpec((None, 1, dv), lambda b, t: (layer, 0, 0)),
            pl.BlockSpec((heads, dv, dk), lambda b, t: (b, 0, 0)),
        ],
        out_specs=[
            pl.BlockSpec((c, wv), lambda b, t: (b * nc + t, 0)),
            pl.BlockSpec((heads, dv, dk), lambda b, t: (b, 0, 0)),
        ],
        out_shape=[jax.ShapeDtypeStruct((n * l, wv), BF16),
                   jax.ShapeDtypeStruct((n * heads, dv, dk), F32)],
        scratch_shapes=[pltpu.VMEM((heads, dv, dk), F32)],
        compiler_params=_params("parallel", "arbitrary"), name="gla",
    )(proj, proj, proj, proj, lr, w_lr, b_lr, norm_w, s0t)
    return y, so


def _band_body(q_ref, k_ref, v_ref, bias_ref, o_ref, *, window, past, hb):
    tq = q_ref.shape[0]
    d = q_ref.shape[1] // hb
    qs = pl.program_id(2) * tq
    starts = [pl.multiple_of(jnp.maximum(qs - past + blk * tq, 0), tq) for blk in range(window // tq)]
    exists = lax.broadcasted_iota(jnp.int32, (tq, window), 1) >= past - qs
    for h in range(hb):
        cols = slice(h * d, (h + 1) * d)
        q = _bf(q_ref[:, cols])
        kw = jnp.concatenate([_bf(k_ref[pl.ds(st, tq), cols]) for st in starts], axis=0)
        vw = jnp.concatenate([_bf(v_ref[pl.ds(st, tq), cols]) for st in starts], axis=0)
        s = lax.dot_general(q, kw, _NT, preferred_element_type=F32) * (d ** -0.5) + bias_ref[h]
        s = jnp.where(exists, s, NEG)
        p = jnp.exp(s - jnp.max(s, axis=-1, keepdims=True))
        o = jnp.dot(p.astype(BF16), vw, preferred_element_type=F32) / jnp.sum(p, axis=-1, keepdims=True)
        o_ref[:, cols] = o.astype(o_ref.dtype)


def _toeplitz_bias(table, tq, window, offset, static_mask=None):
    rows, heads = table.shape
    clip = (rows - 1) // 2
    period = tq + window
    m = np.arange(period)
    m = np.where(m >= window, m - period, m)
    idx = np.clip(offset - m, -clip, clip) + clip
    vec = table.astype(F32)[idx, :].T
    flat = jnp.tile(vec, (1, tq))[:, :tq * (period - 1)]
    bias = flat.reshape(heads, tq, period - 1)[:, :, :window]
    if static_mask is not None:
        bias = jnp.where(static_mask[None], bias, NEG)
    return bias


def _band(q_arr, qcol, k_arr, v_arr, kcol, vcol, n, l, heads, hd, bias, tq, window, past, out_rows):
    nq = l // tq
    lk = k_arr.shape[0] // n
    hb = 2
    assert heads % hb == 0 and qcol % hb == 0 and kcol % hb == 0 and vcol % hb == 0
    w = hb * hd
    return pl.pallas_call(
        functools.partial(_band_body, window=window, past=past, hb=hb), grid=(n, heads // hb, nq),
        in_specs=[
            pl.BlockSpec((tq, w), lambda b, h, i: (b * nq + i, qcol // hb + h)),
            pl.BlockSpec((lk, w), lambda b, h, i: (b, kcol // hb + h)),
            pl.BlockSpec((lk, w), lambda b, h, i: (b, vcol // hb + h)),
            pl.BlockSpec((hb, tq, window), lambda b, h, i: (h, 0, 0)),
        ],
        out_specs=pl.BlockSpec((tq, w), lambda b, h, i: (b * nq + i, h)),
        out_shape=jax.ShapeDtypeStruct((out_rows, heads * hd), BF16),
        compiler_params=_params("parallel", "parallel", "arbitrary"), name="band_attention",
    )(q_arr, k_arr, v_arr, bias)


def _band_decode_body(q_ref, kn_ref, vn_ref, kc_ref, vc_ref, bc_ref, bn_ref, o_ref, *, heads):
    d = q_ref.shape[1] // heads
    scale = d ** -0.5
    for h in range(heads):
        cols = slice(h * d, (h + 1) * d)
        q = _bf(q_ref[:, cols])
        sc = lax.dot_general(q, _bf(kc_ref[:, h, :]), _NT, preferred_element_type=F32) * scale + bc_ref[h]
        sn = lax.dot_general(q, _bf(kn_ref[:, cols]), _NT, preferred_element_type=F32) * scale + bn_ref[h]
        m = jnp.maximum(jnp.max(sc, axis=-1, keepdims=True), jnp.max(sn, axis=-1, keepdims=True))
        pc, pn = jnp.exp(sc - m), jnp.exp(sn - m)
        o = (jnp.dot(pc.astype(BF16), _bf(vc_ref[:, h, :]), preferred_element_type=F32)
             + jnp.dot(pn.astype(BF16), _bf(vn_ref[:, cols]), preferred_element_type=F32))
        o = o / (jnp.sum(pc, axis=-1, keepdims=True) + jnp.sum(pn, axis=-1, keepdims=True))
        o_ref[:, cols] = o.astype(o_ref.dtype)


def _band_decode(projb, n, l, heads, hd, k_cache, v_cache, layer, bias_c, bias_n):
    w = heads * hd
    r = k_cache.shape[1]
    new = lambda blk: pl.BlockSpec((l, w), lambda b: (b, blk))
    cache = pl.BlockSpec((None, r, heads, hd), lambda b: (layer * n + b, 0, 0, 0))
    return pl.pallas_call(
        functools.partial(_band_decode_body, heads=heads), grid=(n,),
        in_specs=[new(0), new(1), new(2), cache, cache,
                  pl.BlockSpec((heads, l, r), lambda b: (0, 0, 0)),
                  pl.BlockSpec((heads, l, l), lambda b: (0, 0, 0))],
        out_specs=pl.BlockSpec((l, w), lambda b: (b, 0)),
        out_shape=jax.ShapeDtypeStruct((n * l, w), BF16),
        compiler_params=_params("parallel"), name="band_decode",
    )(projb, projb, projb, k_cache, v_cache, bias_c, bias_n)


def _kv_rows_body(*refs, nl, heads, hd):
    k_refs, v_refs = refs[0:nl], refs[nl:2 * nl]
    ko_ref, vo_ref = refs[2 * nl], refs[2 * nl + 1]
    layer = pl.program_id(0)
    for li in range(nl):
        @pl.when(layer == li)
        def _(li=li):
            for h in range(heads):
                ko_ref[:, h, :] = k_refs[li][:, h * hd:(h + 1) * hd]
                vo_ref[:, h, :] = v_refs[li][:, h * hd:(h + 1) * hd]


def _kv_rows(projs, kblk, vblk, heads, hd):
    nl = len(projs)
    m = projs[0].shape[0]
    w = heads * hd
    tm = _pick(m, (256, 128, 64, 32, 16, 8))
    nt = m // tm

    def in_spec(li, blk):
        return pl.BlockSpec((tm, w), lambda layer, i: (jnp.clip(i + (layer - li) * nt, 0, nt - 1), blk))

    out_spec = pl.BlockSpec((None, tm, heads, hd), lambda layer, i: (layer, i, 0, 0))
    out_sds = jax.ShapeDtypeStruct((nl, m, heads, hd), F32)
    return pl.pallas_call(
        functools.partial(_kv_rows_body, nl=nl, heads=heads, hd=hd), grid=(nl, nt),
        in_specs=[in_spec(li, kblk) for li in range(nl)] + [in_spec(li, vblk) for li in range(nl)],
        out_specs=[out_spec, out_spec], out_shape=[out_sds, out_sds],
        compiler_params=_params("arbitrary", "arbitrary"), name="kv_rows",
    )(*projs, *projs)


def _trunk(x, n, l, mods, st, p, dims):
    m, d = x.shape
    depth = p['w_mod'].shape[0]
    alpha = (2 * depth) ** 0.25
    expanded = l % 256 != 0
    tm_big = m if expanded else _pick(l, (1024, 512, 256))
    tm_row = m if expanded else 256
    tn_wide = 512
    d_ff = p['ffn_w_gate'].shape[2]
    g_, p_, hg = dims['s5']
    mix_a = g_ * hg
    sb_heads, sb_hd = dims['sb']
    gla_heads, gla_dk, gla_dv = dims['gla']
    ca_heads, ca_hd = dims['ca']
    sb_w = sb_heads * sb_hd
    ca_w = ca_heads * ca_hd
    gla_w = gla_heads * gla_dv

    def mod_of(layer):
        if expanded:
            arr = jnp.repeat(mods[layer], l, axis=0)[None]
        else:
            arr = mods[layer][:, None, :]
        return _Mod(arr, d, l, expanded)

    outs = {k: [] for k in ('s5_re', 's5_im', 'gla', 'band_k', 'band_v')}
    projs_ab = []
    mod = mod_of(0)
    h = _modulate(x, mod, tm_row)
    for layer in range(depth):
        i = layer // 2
        if layer % 2 == 0:
            in_ab = p['ab_w_in'].shape[2]
            proj, = _matmul(h, [(p['ab_w_in'], i, 0)], n=in_ab, tm=tm_big, tn=tn_wide, tk=d, out_dtypes=[F32],
                            epilogue=_epi_identity, name="ab_in")
            gb = g_ // S5_LANE_GROUPS
            sw = S5_LANE_GROUPS * p_
            x0 = jnp.concatenate([st['s5_re'][i].reshape(n * gb, 1, sw), st['s5_im'][i].reshape(n * gb, 1, sw)], axis=2)
            t_rows = _pick(l, (512, 256, 128, 64, 32, 16, 8))
            tables = _s5_tables(*p['s5_raw'][i], seg=t_rows // SUBLANES)
            z, xo = _s5(proj, n, l, tables, p['s5_d'], i, x0, t_rows)
            outs['s5_re'].append(xo[:, 0, :sw].reshape(n, g_, p_))
            outs['s5_im'].append(xo[:, 0, sw:].reshape(n, g_, p_))
            tn_glu = tn_wide
            ya, = _matmul(z, [(p['s5_w_glu'], i, 0)], n=mix_a, tm=tm_big, tn=tn_glu, tk=mix_a, out_dtypes=[BF16],
                          epilogue=_epi_glu, name="s5_glu",
                          extras=[(z, (tm_big, tn_glu), lambda a, b: (a, b)),
                                  (p['s5_b_glu'], (None, 1, tn_glu), lambda a, b, i=i: (i, 0, b))])
            qc, kc, vc = mix_a // sb_hd, (mix_a + sb_w) // sb_hd, (mix_a + 2 * sb_w) // sb_hd
            prev = None if st['sb_k'] is None else (st['sb_k'], st['sb_v'], i)
            yb = _stick_breaking(proj, n, l, sb_heads, sb_hd, qc, kc, vc, prev)
            projs_ab.append(proj)
            merged = [ya, yb]
            w_out = p['ab_w_out']
        else:
            n_gla = 2 * gla_heads * gla_dk + 2 * gla_w
            proj, = _matmul(h, [(p['cd_w_gla'][i], None, 0)], n=n_gla, tm=tm_big, tn=tn_wide, tk=d, out_dtypes=[F32],
                            epilogue=_epi_identity, name="cd_in_gla")
            projb, = _matmul(h, [(p['cd_w_band'][i], None, 0)], n=3 * ca_w, tm=tm_big, tn=tn_wide, tk=d,
                             out_dtypes=[F32], epilogue=_epi_identity, name="cd_in_band")
            lr, = _matmul(h, [(p['cd_w_lrin'][i], None, 0)], n=LANES, tm=tm_big, tn=LANES, tk=d, out_dtypes=[F32],
                          epilogue=_epi_identity, name="cd_lr")
            s0t = jnp.swapaxes(st['gla'][i], -1, -2).reshape(n * gla_heads, gla_dv, gla_dk)
            yc, so = _gla(proj, lr, n, l, gla_heads, gla_dk, gla_dv, p['gla_w_lr_pad'], p['gla_b_lr'],
                          p['gla_norm_w'], i, s0t)
            outs['gla'].append(jnp.swapaxes(so.reshape(n, gla_heads, gla_dv, gla_dk), -1, -2))
            ka = projb[:, ca_w:2 * ca_w]
            va = projb[:, 2 * ca_w:3 * ca_w]
            table = p['ca_rel_bias'][i]
            if st['band_k'] is None:
                tq = _pick(l, (256, 128, 64))
                window = tq + BAND_PAST
                qchunk = np.arange(tq)[:, None] // CHUNK
                kchunk = np.arange(window)[None, :] // CHUNK
                static_mask = (kchunk >= qchunk) & (kchunk <= qchunk + BAND_PREV)
                bias = _toeplitz_bias(table, tq, window, BAND_PAST, static_mask)
                yd = _band(projb, 0, projb, projb, ca_heads, 2 * ca_heads, n, l, ca_heads, ca_hd, bias, tq, window,
                           BAND_PAST, m)
                keep = min(BAND_PAST, l)
                outs['band_k'].append(ka.reshape(n, l, ca_heads, ca_hd)[:, l - keep:])
                outs['band_v'].append(va.reshape(n, l, ca_heads, ca_hd)[:, l - keep:])
            else:
                r = st['band_k'].shape[1]
                bias = _toeplitz_bias(table, l, r + l, r)
                yd = _band_decode(projb, n, l, ca_heads, ca_hd, st['band_k'], st['band_v'], i,
                                  bias[:, :, :r], bias[:, :, r:])
                outs['band_k'].append(ka.reshape(n, l, ca_heads, ca_hd))
                outs['band_v'].append(va.reshape(n, l, ca_heads, ca_hd))
            merged = [yc, yd]
            w_out = p['cd_w_out']
        f, = _matmul(merged, [(w_out, i, 0)], n=d, tm=tm_big, tn=tn_wide, tk=sum(y.shape[1] for y in merged),
                     out_dtypes=[F32], epilogue=_epi_identity, name="mix_out")
        x, h = _deepnorm(x, f, mod, 2, p['ln1_w'], p['ln1_b'], layer, alpha, tm_row, nxt=(mod, 4, 3))
        tn_ff = _pick(d_ff, (256, 128))
        act, = _matmul(h, [(p['ffn_w_gate'], layer, 0), (p['ffn_w_up'], layer, 0)], n=d_ff, tm=tm_big, tn=tn_ff,
                       tk=d, out_dtypes=[BF16], epilogue=_epi_swiglu, name="ffn_in")
        ff, = _matmul(act, [(p['ffn_w_down'], layer, 0)], n=d, tm=min(tm_big, 512), tn=256, tk=d_ff,
                      out_dtypes=[F32], epilogue=_epi_identity, name="ffn_out")
        if layer + 1 < depth:
            nmod = mod_of(layer + 1)
            x, h = _deepnorm(x, ff, mod, 5, p['ln2_w'], p['ln2_b'], layer, alpha, tm_row, nxt=(nmod, 1, 0))
            mod = nmod
        else:
            x, _ = _deepnorm(x, ff, mod, 5, p['ln2_w'], p['ln2_b'], layer, alpha, tm_row)
    outs = {k: jnp.stack(v) for k, v in outs.items()}
    assert mix_a % sb_w == 0
    sb_k, sb_v = _kv_rows(projs_ab, mix_a // sb_w + 1, mix_a // sb_w + 2, sb_heads, sb_hd)
    outs['sb_k'] = sb_k.reshape(len(projs_ab), n, l, sb_heads, sb_hd)
    outs['sb_v'] = sb_v.reshape(len(projs_ab), n, l, sb_heads, sb_hd)
    return x, outs


def kernel(x_prompt, x_sample, state_s5_re, state_s5_im, cache_sb_k, cache_sb_v, state_gla, cache_band_k, cache_band_v, c_prompt, c_sample, w_mod, b_mod, ln1_w, ln1_b, ln2_w, ln2_b, ab_w_in, ab_w_out, s5_a_re, s5_a_im, s5_log_dt, s5_b_re, s5_b_im, s5_c_re, s5_c_im, s5_d, s5_w_glu, s5_b_glu, cd_w_in, cd_w_out, gla_w_lr, gla_b_lr, gla_norm_w, ca_rel_bias, ffn_w_gate, ffn_w_up, ffn_w_down):
    nb, seq, d = x_prompt.shape
    db, dseq, _ = x_sample.shape
    depth = w_mod.shape[0]
    n_even, g_, p_ = s5_a_re.shape
    hg = s5_b_re.shape[-1]
    _, _, past, sb_heads, sb_hd = cache_sb_k.shape
    n_odd, _, gla_heads, gla_dk, gla_dv = state_gla.shape
    _, _, band_rows, ca_heads, ca_hd = cache_band_k.shape
    rank = gla_w_lr.shape[1]
    dims = {'s5': (g_, p_, hg), 'sb': (sb_heads, sb_hd), 'gla': (gla_heads, gla_dk, gla_dv), 'ca': (ca_heads, ca_hd)}

    lr0 = 2 * gla_heads * gla_dk + 2 * gla_heads * gla_dv
    p = dict(
        w_mod=w_mod, ln1_w=ln1_w[:, None, :], ln1_b=ln1_b[:, None, :], ln2_w=ln2_w[:, None, :], ln2_b=ln2_b[:, None, :],
        ab_w_in=ab_w_in, ab_w_out=ab_w_out, s5_d=s5_d[:, None, :], s5_w_glu=s5_w_glu, s5_b_glu=s5_b_glu[:, None, :],
        cd_w_out=cd_w_out, gla_b_lr=gla_b_lr[:, None, :], gla_norm_w=gla_norm_w[:, None, :], ca_rel_bias=ca_rel_bias,
        ffn_w_gate=ffn_w_gate, ffn_w_up=ffn_w_up, ffn_w_down=ffn_w_down.astype(BF16),
        cd_w_gla=[cd_w_in[i, :, :lr0] for i in range(n_odd)],
        cd_w_band=[cd_w_in[i, :, lr0 + rank:] for i in range(n_odd)],
        cd_w_lrin=[jnp.pad(cd_w_in[i, :, lr0:lr0 + rank], ((0, 0), (0, LANES - rank))) for i in range(n_odd)],
        gla_w_lr_pad=jnp.pad(gla_w_lr, ((0, 0), (0, LANES - rank), (0, 0))),
        s5_raw=[(s5_a_re[i], s5_a_im[i], s5_log_dt[i], s5_b_re[i], s5_b_im[i], s5_c_re[i], s5_c_im[i])
                for i in range(n_even)],
    )

    rows = nb + db
    rows_pad = -(-rows // 16) * 16
    c_all = jnp.concatenate([c_prompt, c_sample, jnp.zeros((rows_pad - rows, d), F32)], axis=0)
    mods = []
    for layer in range(depth):
        tn_mod = _pick(6 * d, (1024, 512, 256, 128))
        mod_l, = _matmul(c_all, [(w_mod, layer, 0)], n=6 * d, tm=rows_pad, tn=tn_mod, tk=d, out_dtypes=[F32],
                         epilogue=_epi_bias, prologue=_silu, name="adaln_mod",
                         extras=[(b_mod[:, None, :], (None, 1, tn_mod), lambda a, b, layer=layer: (layer, 0, b))])
        mods.append(mod_l)
    mods = jnp.stack(mods)
    mods_p, mods_s = mods[:, :nb], mods[:, nb:rows]

    zeros_s5 = jnp.zeros((n_even, nb, g_, p_), F32)
    st_p = dict(s5_re=zeros_s5, s5_im=zeros_s5, sb_k=None, sb_v=None,
                gla=jnp.zeros((n_odd, nb, gla_heads, gla_dk, gla_dv), F32), band_k=None, band_v=None)
    y_p, o_p = _trunk(x_prompt.reshape(nb * seq, d), nb, seq, mods_p, st_p, p, dims)
    st_s = dict(s5_re=state_s5_re, s5_im=state_s5_im,
                sb_k=cache_sb_k.reshape(n_even * db, past, sb_heads, sb_hd),
                sb_v=cache_sb_v.reshape(n_even * db, past, sb_heads, sb_hd),
                gla=state_gla,
                band_k=cache_band_k.reshape(n_odd * db, band_rows, ca_heads, ca_hd),
                band_v=cache_band_v.reshape(n_odd * db, band_rows, ca_heads, ca_hd))
    y_s, o_s = _trunk(x_sample.reshape(db * dseq, d), db, dseq, mods_s, st_s, p, dims)
    return (y_p.reshape(nb, seq, d), y_s.reshape(db, dseq, d),
            o_p['s5_re'], o_p['s5_im'], o_p['sb_k'], o_p['sb_v'], o_p['gla'], o_p['band_k'], o_p['band_v'],
            o_s['s5_re'], o_s['s5_im'], o_s['sb_k'], o_s['sb_v'], o_s['gla'], o_s['band_k'], o_s['band_v'])
```

```python
import functools
import math

import numpy as np
import jax
import jax.numpy as jnp
from jax import lax
from jax.experimental import pallas as pl
from jax.experimental.pallas import tpu as pltpu

F32 = jnp.float32
BF16 = jnp.bfloat16

CHUNK = 64
BAND_PREV = 8
BAND_PAST = BAND_PREV * CHUNK
GLA_GATE_NORM = 16.0
LN_EPS = 1e-5
RMS_EPS = 1e-6

LANES = 128
SUBLANES = 8
V7X_VMEM_BYTES = 64 * 1024 * 1024
VMEM_LIMIT = V7X_VMEM_BYTES - 8 * 1024 * 1024

GLA_SUB = 16
S5_LANE_GROUPS = 8
NEG = -1e30
SB_EXIT = -104.0

_NT = (((1,), (1,)), ((), ()))
_TN = (((0,), (0,)), ((), ()))


def _pick(dim, candidates):
    for c in candidates:
        if c <= dim and dim % c == 0:
            return c
    return dim


def _params(*sem):
    return pltpu.CompilerParams(dimension_semantics=sem, vmem_limit_bytes=VMEM_LIMIT)


def _bf(x):
    return x if x.dtype == BF16 else x.astype(BF16)


def _log_sigmoid(z):
    return jnp.minimum(z, 0.0) - jnp.log(1.0 + jnp.exp(-jnp.abs(z)))


def _silu(x):
    return x * jax.nn.sigmoid(x)


def _mm_body(*refs, na, nb, ne, no, nk, prologue, epilogue):
    a_refs, refs = refs[:na], refs[na:]
    b_refs = refs[0:nb]
    e_refs = refs[nb:nb + ne]
    o_refs = refs[nb + ne:nb + ne + no]
    acc_refs = refs[nb + ne + no:]
    a_parts = [a_ref[...] for a_ref in a_refs]
    if prologue is not None:
        a_parts = [prologue(a) for a in a_parts]
    a_parts = [_bf(a) for a in a_parts]
    a = a_parts[0] if na == 1 else jnp.concatenate(a_parts, axis=1)
    parts = [jnp.dot(a, _bf(b_ref[...]), preferred_element_type=F32) for b_ref in b_refs]

    def finish(accs):
        outs = epilogue(accs, [e[...] for e in e_refs])
        for o_ref, o in zip(o_refs, outs):
            o_ref[...] = o.astype(o_ref.dtype)

    if nk == 1:
        finish(parts)
        return
    k = pl.program_id(2)

    @pl.when(k == 0)
    def _():
        for acc, p in zip(acc_refs, parts):
            acc[...] = p

    @pl.when(k > 0)
    def _():
        for acc, p in zip(acc_refs, parts):
            acc[...] += p

    @pl.when(k == nk - 1)
    def _():
        finish([acc[...] for acc in acc_refs])


def _matmul(a, bs, *, n, tm, tn, tk, out_dtypes, epilogue, extras=(), prologue=None, a_buffers=None,
            name="matmul"):
    a_list = list(a) if isinstance(a, (list, tuple)) else [a]
    m = a_list[0].shape[0]
    kdim = sum(x.shape[1] for x in a_list)
    nk = kdim // tk
    assert len(a_list) == 1 or nk == 1
    grid = (m // tm, n // tn, nk)
    if len(a_list) == 1:
        mode = {} if a_buffers is None else {"pipeline_mode": pl.Buffered(a_buffers)}
        in_specs = [pl.BlockSpec((tm, tk), lambda i, j, k: (i, k), **mode)]
    else:
        in_specs = [pl.BlockSpec((tm, x.shape[1]), lambda i, j, k: (i, 0)) for x in a_list]
    operands = list(a_list)
    for arr, layer, c0 in bs:
        if arr.ndim == 3:
            in_specs.append(pl.BlockSpec((None, tk, tn), lambda i, j, k, layer=layer, c0=c0: (layer, k, c0 + j)))
        else:
            in_specs.append(pl.BlockSpec((tk, tn), lambda i, j, k, c0=c0: (k, c0 + j)))
        operands.append(arr)
    for arr, shape, imap in extras:
        in_specs.append(pl.BlockSpec(shape, lambda i, j, k, imap=imap: imap(i, j)))
        operands.append(arr)
    out_shape = [jax.ShapeDtypeStruct((m, n), dt) for dt in out_dtypes]
    out_specs = [pl.BlockSpec((tm, tn), lambda i, j, k: (i, j)) for _ in out_dtypes]
    scratch = [pltpu.VMEM((tm, tn), F32) for _ in bs] if nk > 1 else []
    body = functools.partial(_mm_body, na=len(a_list), nb=len(bs), ne=len(extras), no=len(out_dtypes), nk=nk,
                             prologue=prologue, epilogue=epilogue)
    outs = pl.pallas_call(
        body, grid=grid, in_specs=in_specs, out_specs=out_specs, out_shape=out_shape,
        scratch_shapes=scratch, compiler_params=_params("parallel", "parallel", "arbitrary"), name=name,
    )(*operands)
    return outs


def _epi_identity(accs, extras):
    return [accs[0]]


def _epi_bias(accs, extras):
    return [accs[0] + extras[0]]


def _epi_swiglu(accs, extras):
    return [_silu(accs[0]) * accs[1]]


def _epi_glu(accs, extras):
    z, bias = extras
    return [z * jax.nn.sigmoid(accs[0] + bias)]


class _Mod:
    def __init__(self, arr, d, l, expanded):
        self.arr, self.d, self.l, self.expanded = arr, d, l, expanded

    def spec(self, piece, tm):
        d = self.d
        if self.expanded:
            return pl.BlockSpec((None, tm, d), lambda i, *_: (0, i, piece))
        per = self.l // tm
        return pl.BlockSpec((None, 1, d), lambda i, *_: (i // per, 0, piece))


def _modulate_body(x_ref, sc_ref, sh_ref, h_ref):
    h_ref[...] = (x_ref[...] * (1.0 + sc_ref[...]) + sh_ref[...]).astype(h_ref.dtype)


def _modulate(x, mod, tm):
    m, d = x.shape
    row = pl.BlockSpec((tm, d), lambda i: (i, 0))
    return pl.pallas_call(
        _modulate_body, grid=(m // tm,), in_specs=[row, mod.spec(1, tm), mod.spec(0, tm)],
        out_specs=row, out_shape=jax.ShapeDtypeStruct((m, d), BF16),
        compiler_params=_params("parallel"), name="modulate",
    )(x, mod.arr, mod.arr)


def _ln_body(*refs, alpha, has_next):
    if has_next:
        x_ref, f_ref, g_ref, w_ref, b_ref, sc_ref, sh_ref, xo_ref, ho_ref = refs
    else:
        x_ref, f_ref, g_ref, w_ref, b_ref, xo_ref = refs
    y = alpha * x_ref[...] + (1.0 + g_ref[...]) * f_ref[...]
    yc = y - jnp.mean(y, axis=-1, keepdims=True)
    var = jnp.mean(yc * yc, axis=-1, keepdims=True)
    xn = yc * lax.rsqrt(var + LN_EPS) * w_ref[...] + b_ref[...]
    xo_ref[...] = xn
    if has_next:
        ho_ref[...] = (xn * (1.0 + sc_ref[...]) + sh_ref[...]).astype(ho_ref.dtype)


def _deepnorm(x, f, mod, gate_piece, w, b, layer, alpha, tm, nxt=None):
    m, d = x.shape
    row = pl.BlockSpec((tm, d), lambda i: (i, 0))
    vec = pl.BlockSpec((None, 1, d), lambda i: (layer, 0, 0))
    in_specs = [row, row, mod.spec(gate_piece, tm), vec, vec]
    operands = [x, f, mod.arr, w, b]
    out_shape = [jax.ShapeDtypeStruct((m, d), F32)]
    out_specs = [row]
    if nxt is not None:
        nmod, sc_piece, sh_piece = nxt
        in_specs += [nmod.spec(sc_piece, tm), nmod.spec(sh_piece, tm)]
        operands += [nmod.arr, nmod.arr]
        out_shape.append(jax.ShapeDtypeStruct((m, d), BF16))
        out_specs.append(row)
    outs = pl.pallas_call(
        functools.partial(_ln_body, alpha=alpha, has_next=nxt is not None),
        grid=(m // tm,), in_specs=in_specs, out_specs=out_specs, out_shape=out_shape,
        compiler_params=_params("parallel"), name="deepnorm",
    )(*operands)
    return outs if nxt is not None else (outs[0], None)


def _s5_body(u_ref, bt_ref, ct_ref, sc_ref, d_ref, x0_ref, z_ref, xo_ref, xs_ref, st_ref, *, t_rows):
    t = pl.program_id(2)
    sw = st_ref.shape[-1] // 2

    @pl.when(t == 0)
    def _():
        st_ref[...] = x0_ref[...]

    seg = t_rows // SUBLANES
    u = jnp.concatenate([u_ref[pl.ds(j, SUBLANES, stride=seg), :] for j in range(seg)], axis=0)
    xs_ref[...] = jnp.dot(_bf(u), bt_ref[...], preferred_element_type=F32)
    ar, ai = sc_ref[0:8, :], sc_ref[8:16, :]
    unroll = 2 if seg % 2 == 0 else 1

    def cmul_add(xr, xi, ar, ai, sr, si):
        return xr + ar * sr - ai * si, xi + ar * si + ai * sr

    def local(j, carry):
        off = pl.multiple_of(j * SUBLANES, SUBLANES)
        xr, xi = cmul_add(xs_ref[pl.ds(off, SUBLANES), 0:sw], xs_ref[pl.ds(off, SUBLANES), sw:2 * sw],
                          ar, ai, carry[0], carry[1])
        xs_ref[pl.ds(off, SUBLANES), 0:sw] = xr
        xs_ref[pl.ds(off, SUBLANES), sw:2 * sw] = xi
        return xr, xi

    zero = jnp.zeros((SUBLANES, sw), F32)
    gr, gi = lax.fori_loop(0, seg, local, (zero, zero), unroll=unroll)

    for step, d in enumerate((1, 2, 4)):
        r0 = 16 + 16 * step
        gr, gi = cmul_add(gr, gi, sc_ref[r0:r0 + 8, :], sc_ref[r0 + 8:r0 + 16, :],
                          pltpu.roll(gr, d, 0), pltpu.roll(gi, d, 0))
    st = st_ref[...]
    sr, si = st[:, 0:sw], st[:, sw:2 * sw]
    gr, gi = cmul_add(gr, gi, sc_ref[64:72, :], sc_ref[72:80, :], sr, si)
    fr, fi = gr[SUBLANES - 1:SUBLANES, :], gi[SUBLANES - 1:SUBLANES, :]
    st_ref[:, 0:sw] = fr
    st_ref[:, sw:2 * sw] = fi

    row = lax.broadcasted_iota(jnp.int32, (SUBLANES, sw), 0)
    cr = jnp.where(row == 0, sr, pltpu.roll(gr, 1, 0))
    ci = jnp.where(row == 0, si, pltpu.roll(gi, 1, 0))

    def fix(j, carry):
        off = pl.multiple_of(j * SUBLANES, SUBLANES)
        dr = ar * carry[0] - ai * carry[1]
        di = ar * carry[1] + ai * carry[0]
        xs_ref[pl.ds(off, SUBLANES), 0:sw] += dr
        xs_ref[pl.ds(off, SUBLANES), sw:2 * sw] += di
        return dr, di

    lax.fori_loop(0, seg, fix, (cr, ci), unroll=unroll)

    y = jnp.dot(_bf(xs_ref[...]), ct_ref[...], preferred_element_type=F32) + d_ref[...] * u
    z = 0.5 * y * (1.0 + jnp.tanh(math.sqrt(2.0 / math.pi) * (y + 0.044715 * (y * y * y))))
    for j in range(seg):
        z_ref[pl.ds(j, SUBLANES, stride=seg), :] = z[j * SUBLANES:(j + 1) * SUBLANES, :]

    @pl.when(t == pl.num_programs(2) - 1)
    def _():
        xo_ref[:, 0:sw] = fr
        xo_ref[:, sw:2 * sw] = fi


def _s5_tables(a_re, a_im, log_dt, b_re, b_im, c_re, c_im, seg):
    g, p = a_re.shape
    hg = b_re.shape[-1]
    gb = g // S5_LANE_GROUPS
    lr_, li_ = a_re.astype(F32), a_im.astype(F32)
    dt = jnp.exp(log_dt.astype(F32))[:, None]
    mag = jnp.exp(lr_ * dt)
    ab_r, ab_i = mag * jnp.cos(li_ * dt), mag * jnp.sin(li_ * dt)
    den = lr_ * lr_ + li_ * li_
    cr_ = ((ab_r - 1.0) * lr_ + ab_i * li_) / den
    ci_ = (ab_i * lr_ - (ab_r - 1.0) * li_) / den
    br_, bi_ = b_re.astype(F32), b_im.astype(F32)
    bb_r = cr_[..., None] * br_ - ci_[..., None] * bi_
    bb_i = cr_[..., None] * bi_ + ci_[..., None] * br_
    eye = jnp.eye(S5_LANE_GROUPS, dtype=F32)

    def blockdiag_in(w):
        w = w.reshape(gb, S5_LANE_GROUPS, p, hg)
        return jnp.einsum('bgph,gk->bghkp', w, eye).reshape(gb, S5_LANE_GROUPS * hg, S5_LANE_GROUPS * p)

    def blockdiag_out(w):
        w = w.reshape(gb, S5_LANE_GROUPS, hg, p)
        return jnp.einsum('bgkp,gj->bgpjk', w, eye).reshape(gb, S5_LANE_GROUPS * p, S5_LANE_GROUPS * hg)

    bt = jnp.concatenate([blockdiag_in(bb_r), blockdiag_in(bb_i)], axis=2).astype(BF16)
    ct = jnp.concatenate([blockdiag_out(c_re.astype(F32)), blockdiag_out(-c_im.astype(F32))], axis=1).astype(BF16)
    sw = S5_LANE_GROUPS * p

    def rows8(z):
        return jnp.broadcast_to(z.reshape(gb, 1, sw), (gb, SUBLANES, sw))

    def cmul(x, y):
        return x[0] * y[0] - x[1] * y[1], x[0] * y[1] + x[1] * y[0]

    a_seg, sq, e = None, (ab_r, ab_i), seg
    while e:
        if e & 1:
            a_seg = sq if a_seg is None else cmul(a_seg, sq)
        sq, e = cmul(sq, sq), e >> 1
    pows = [a_seg]
    for _ in range(SUBLANES - 1):
        pows.append(cmul(pows[-1], a_seg))
    pieces = [rows8(ab_r), rows8(ab_i)]
    trow = jnp.arange(SUBLANES)[None, :, None]
    for d in (1, 2, 4):
        pieces += [jnp.where(trow >= d, rows8(pows[d - 1][part]), 0.0) for part in (0, 1)]
    for part in (0, 1):
        pieces.append(jnp.stack([pw[part] for pw in pows], axis=0).reshape(SUBLANES, gb, sw).transpose(1, 0, 2))
    sc = jnp.concatenate(pieces, axis=1)
    return bt, ct, sc


def _s5(proj, n, l, tables, d_skip, layer, x0, t_rows):
    bt, ct, sc = tables
    gb, cw, sw2 = bt.shape
    nt = l // t_rows
    grid = (n, gb, nt)
    z, xo = pl.pallas_call(
        functools.partial(_s5_body, t_rows=t_rows), grid=grid,
        in_specs=[
            pl.BlockSpec((t_rows, cw), lambda b, g, t: (b * nt + t, g)),
            pl.BlockSpec((None, cw, sw2), lambda b, g, t: (g, 0, 0)),
            pl.BlockSpec((None, sw2, cw), lambda b, g, t: (g, 0, 0)),
            pl.BlockSpec((None, 10 * SUBLANES, sw2 // 2), lambda b, g, t: (g, 0, 0)),
            pl.BlockSpec((None, 1, cw), lambda b, g, t: (layer, 0, g)),
            pl.BlockSpec((None, 1, sw2), lambda b, g, t: (b * gb + g, 0, 0)),
        ],
        out_specs=[
            pl.BlockSpec((t_rows, cw), lambda b, g, t: (b * nt + t, g)),
            pl.BlockSpec((None, 1, sw2), lambda b, g, t: (b * gb + g, 0, 0)),
        ],
        out_shape=[jax.ShapeDtypeStruct((n * l, gb * cw), F32),
                   jax.ShapeDtypeStruct((n * gb, 1, sw2), F32)],
        scratch_shapes=[pltpu.VMEM((t_rows, sw2), F32), pltpu.VMEM((1, sw2), F32)],
        compiler_params=_params("parallel", "parallel", "arbitrary"), name="s5_scan",
    )(proj, bt, ct, sc, d_skip, x0)
    return z, xo


def _strict_upper2(tk):
    rj = lax.broadcasted_iota(jnp.int32, (tk, tk), 0)
    cs = lax.broadcasted_iota(jnp.int32, (tk, tk), 1)
    upper = jnp.where(rj > cs, 1.0, 0.0).astype(BF16)
    return jnp.concatenate([upper, upper], axis=0)


def _sb_tile(q, kt, vt, carry, mask, scale, upper2):
    z = lax.dot_general(q, _bf(kt), _NT, preferred_element_type=F32) * scale
    ls = _log_sigmoid(z)
    lf = ls - z
    if mask is not None:
        lf = jnp.where(mask, lf, 0.0)
    hi = lf.astype(BF16)
    lo = (lf - hi.astype(F32)).astype(BF16)
    newer = jnp.dot(jnp.concatenate([hi, lo], axis=1), upper2, preferred_element_type=F32)
    w = jnp.exp(ls + carry + newer)
    if mask is not None:
        w = jnp.where(mask, w, 0.0)
    out = jnp.dot(w.astype(BF16), _bf(vt), preferred_element_type=F32)
    return out, carry + jnp.sum(lf, axis=1, keepdims=True)


def _sb_body(q_ref, kd_ref, vd_ref, kp_ref, vp_ref, o_ref, *, tk, prev_tiles, hb):
    tq = q_ref.shape[0]
    d = q_ref.shape[1] // hb
    scale = d ** -0.5
    r = lax.broadcasted_iota(jnp.int32, (tq, tq), 0)
    c = lax.broadcasted_iota(jnp.int32, (tq, tq), 1)
    upper2 = _strict_upper2(tk)
    upper2_diag = upper2 if tq == tk else _strict_upper2(tq)
    nprev = prev_tiles(pl.program_id(2))
    first_off = pl.multiple_of(jnp.maximum(nprev - 1, 0) * tk, tk)

    def earlier(ref, off, h):
        if len(ref.shape) == 3:
            return ref[pl.ds(off, tk), h, :]
        return ref[pl.ds(off, tk), h * d:(h + 1) * d]

    states = []
    for h in range(hb):
        cols = slice(h * d, (h + 1) * d)
        q = _bf(q_ref[:, cols])
        acc, carry = _sb_tile(q, kd_ref[:, cols], vd_ref[:, cols], jnp.zeros((tq, 1), F32), c < r, scale,
                              upper2_diag)
        out, carry = _sb_tile(q, earlier(kp_ref, first_off, h), earlier(vp_ref, first_off, h), carry,
                              nprev > 0, scale, upper2)
        states.append((q, acc + out, carry))

    def cond(state):
        j, _, _, cmax = state
        return jnp.logical_and(j >= 0, cmax > SB_EXIT)

    for h, (q, acc, carry) in enumerate(states):
        cols = slice(h * d, (h + 1) * d)

        def body(state, q=q, h=h):
            j, acc, carry, _ = state
            off = pl.multiple_of(j * tk, tk)
            out, carry = _sb_tile(q, earlier(kp_ref, off, h), earlier(vp_ref, off, h), carry, None, scale, upper2)
            return j - 1, acc + out, carry, jnp.max(carry)

        first = jnp.asarray(nprev - 2, jnp.int32)
        _, acc, _, _ = lax.while_loop(cond, body, (first, acc, carry, jnp.max(carry)))
        o_ref[:, cols] = acc.astype(o_ref.dtype)


def _stick_breaking(proj, n, l, heads, hd, qcol, kcol, vcol, prev=None):
    if prev is None:
        hb = 2
        tq = _pick(l, (256, 128, 64, 32, 16, 8))
        tk = tq
        nq = l // tq
        kp, vp = proj, proj
        kp_spec = pl.BlockSpec((l, hb * hd), lambda b, h, i: (b, kcol // hb + h))
        vp_spec = pl.BlockSpec((l, hb * hd), lambda b, h, i: (b, vcol // hb + h))
        prev_tiles = lambda i: i
    else:
        hb = heads
        kc, vc, layer = prev
        past = kc.shape[1]
        tq, nq = l, 1
        tk = _pick(past, (256, 128, 64, 32, 16, 8))
        kp, vp = kc, vc
        kp_spec = pl.BlockSpec((None, past, heads, hd), lambda b, h, i: (layer * n + b, 0, 0, 0))
        vp_spec = pl.BlockSpec((None, past, heads, hd), lambda b, h, i: (layer * n + b, 0, 0, 0))
        prev_tiles = lambda i: past // tk
    assert heads % hb == 0 and qcol % hb == 0 and kcol % hb == 0 and vcol % hb == 0
    w = hb * hd
    return pl.pallas_call(
        functools.partial(_sb_body, tk=tk, prev_tiles=prev_tiles, hb=hb), grid=(n, heads // hb, nq),
        in_specs=[
            pl.BlockSpec((tq, w), lambda b, h, i: (b * nq + i, qcol // hb + h)),
            pl.BlockSpec((tq, w), lambda b, h, i: (b * nq + i, kcol // hb + h)),
            pl.BlockSpec((tq, w), lambda b, h, i: (b * nq + i, vcol // hb + h)),
            kp_spec, vp_spec,
        ],
        out_specs=pl.BlockSpec((tq, w), lambda b, h, i: (b * nq + i, h)),
        out_shape=jax.ShapeDtypeStruct((n * l, heads * hd), BF16),
        compiler_params=_params("parallel", "parallel", "arbitrary"), name="stick_breaking",
    )(proj, proj, proj, kp, vp)


def _gla_body(q_ref, k_ref, v_ref, g_ref, lr_ref, wlr_ref, blr_ref, nw_ref, s0_ref, y_ref, so_ref, st_ref, *, heads):
    t = pl.program_id(1)
    c = q_ref.shape[0]

    @pl.when(t == 0)
    def _():
        st_ref[...] = s0_ref[...]

    gate_in = jnp.dot(_bf(lr_ref[...]), _bf(wlr_ref[...]), preferred_element_type=F32) + blr_ref[...]
    logg = _log_sigmoid(gate_in) * (1.0 / GLA_GATE_NORM)
    rt = lax.broadcasted_iota(jnp.int32, (c, c), 0)
    cs = lax.broadcasted_iota(jnp.int32, (c, c), 1)
    lower = jnp.where(cs <= rt, 1.0, 0.0).astype(BF16)
    hi = logg.astype(BF16)
    lo = (logg - hi.astype(F32)).astype(BF16)
    b_all = jnp.dot(lower, hi, preferred_element_type=F32) + jnp.dot(lower, lo, preferred_element_type=F32)
    last = t == pl.num_programs(1) - 1
    for hh in range(heads):
        _gla_head(hh, b_all, q_ref, k_ref, v_ref, g_ref, nw_ref, y_ref, so_ref, st_ref, last, heads)


def _gla_head(hh, b_all, q_ref, k_ref, v_ref, g_ref, nw_ref, y_ref, so_ref, st_ref, last, heads):
    c = q_ref.shape[0]
    dk = q_ref.shape[1] // heads
    dv = v_ref.shape[1] // heads
    q = q_ref[:, hh * dk:(hh + 1) * dk] * (dk ** -0.5)
    k = k_ref[:, hh * dk:(hh + 1) * dk]
    vb = _bf(v_ref[:, hh * dv:(hh + 1) * dv])
    b = b_all[:, hh * dk:(hh + 1) * dk]
    st = st_ref[hh]
    o_inter = lax.dot_general(_bf(q * jnp.exp(b)), _bf(st), _NT, preferred_element_type=F32)
    row = lax.broadcasted_iota(jnp.int32, (GLA_SUB, 1), 0)
    lane = lax.broadcasted_iota(jnp.int32, (GLA_SUB, GLA_SUB), 1)
    outs = []
    for blk in range(c // GLA_SUB):
        r0, r1 = blk * GLA_SUB, (blk + 1) * GLA_SUB
        bi, qi, ki = b[r0:r1], q[r0:r1], k[r0:r1]
        att = jnp.zeros((GLA_SUB, GLA_SUB), F32)
        for s in range(GLA_SUB):
            dec = jnp.exp(jnp.where(row >= s, bi - bi[s:s + 1, :], NEG))
            col = jnp.sum(qi * dec * ki[s:s + 1, :], axis=1, keepdims=True)
            att = jnp.where(lane == s, col, att)
        oi = jnp.dot(_bf(att), vb[r0:r1], preferred_element_type=F32)
        if blk > 0:
            bref = b[r0 - 1:r0, :]
            qt = qi * jnp.exp(bi - bref)
            kt = k[0:r0] * jnp.exp(bref - b[0:r0])
            a_off = lax.dot_general(_bf(qt), _bf(kt), _NT, preferred_element_type=F32)
            oi = oi + jnp.dot(_bf(a_off), vb[0:r0], preferred_element_type=F32)
        outs.append(oi)
    o = (outs[0] if len(outs) == 1 else jnp.concatenate(outs, axis=0)) + o_inter
    o = o * lax.rsqrt(jnp.mean(o * o, axis=-1, keepdims=True) + RMS_EPS) * nw_ref[...]
    y_ref[:, hh * dv:(hh + 1) * dv] = (o * _silu(g_ref[:, hh * dv:(hh + 1) * dv])).astype(y_ref.dtype)

    bl = b[c - 1:c, :]
    kh = k * jnp.exp(bl - b)
    new = st * jnp.exp(bl) + lax.dot_general(vb, _bf(kh), _TN, preferred_element_type=F32)
    st_ref[hh] = new

    @pl.when(last)
    def _():
        so_ref[hh] = new


def _gla(proj, lr, n, l, heads, dk, dv, w_lr, b_lr, norm_w, layer, s0t):
    c = min(CHUNK, l)
    nc = l // c
    wk, wv = heads * dk, heads * dv
    assert (2 * wk) % wv == 0
    v_blk = (2 * wk) // wv
    rank_pad = lr.shape[1]
    y, so = pl.pallas_call(
        functools.partial(_gla_body, heads=heads), grid=(n, nc),
        in_specs=[
            pl.BlockSpec((c, wk), lambda b, t: (b * nc + t, 0)),
            pl.BlockSpec((c, wk), lambda b, t: (b * nc + t, 1)),
            pl.BlockSpec((c, wv), lambda b, t: (b * nc + t, v_blk)),
            pl.BlockSpec((c, wv), lambda b, t: (b * nc + t, v_blk + 1)),
            pl.BlockSpec((c, rank_pad), lambda b, t: (b * nc + t, 0)),
            pl.BlockSpec((None, rank_pad, wk), lambda b, t: (layer, 0, 0)),
            pl.BlockSpec((None, 1, wk), lambda b, t: (layer, 0, 0)),
            pl.BlockSpec((None, 1, dv), lambda b, t: (layer, 0, 0)),
            pl.BlockSpec((heads, dv, dk), lambda b, t: (b, 0, 0)),
        ],
        out_specs=[
            pl.BlockSpec((c, wv), lambda b, t: (b * nc + t, 0)),
            pl.BlockSpec((heads, dv, dk), lambda b, t: (b, 0, 0)),
        ],
        out_shape=[jax.ShapeDtypeStruct((n * l, wv), BF16),
                   jax.ShapeDtypeStruct((n * heads, dv, dk), F32)],
        scratch_shapes=[pltpu.VMEM((heads, dv, dk), F32)],
        compiler_params=_params("parallel", "arbitrary"), name="gla",
    )(proj, proj, proj, proj, lr, w_lr, b_lr, norm_w, s0t)
    return y, so


def _band_body(q_ref, k_ref, v_ref, bias_ref, o_ref, *, window, past, hb):
    tq = q_ref.shape[0]
    d = q_ref.shape[1] // hb
    qs = pl.program_id(2) * tq
    starts = [pl.multiple_of(jnp.maximum(qs - past + blk * tq, 0), tq) for blk in range(window // tq)]
    exists = lax.broadcasted_iota(jnp.int32, (tq, window), 1) >= past - qs
    for h in range(hb):
        cols = slice(h * d, (h + 1) * d)
        q = _bf(q_ref[:, cols])
        kw = jnp.concatenate([_bf(k_ref[pl.ds(st, tq), cols]) for st in starts], axis=0)
        vw = jnp.concatenate([_bf(v_ref[pl.ds(st, tq), cols]) for st in starts], axis=0)
        s = lax.dot_general(q, kw, _NT, preferred_element_type=F32) * (d ** -0.5) + bias_ref[h]
        s = jnp.where(exists, s, NEG)
        p = jnp.exp(s - jnp.max(s, axis=-1, keepdims=True))
        o = jnp.dot(p.astype(BF16), vw, preferred_element_type=F32) / jnp.sum(p, axis=-1, keepdims=True)
        o_ref[:, cols] = o.astype(o_ref.dtype)


def _toeplitz_bias(table, tq, window, offset, static_mask=None):
    rows, heads = table.shape
    clip = (rows - 1) // 2
    period = tq + window
    m = np.arange(period)
    m = np.where(m >= window, m - period, m)
    idx = np.clip(offset - m, -clip, clip) + clip
    vec = table.astype(F32)[idx, :].T
    flat = jnp.tile(vec, (1, tq))[:, :tq * (period - 1)]
    bias = flat.reshape(heads, tq, period - 1)[:, :, :window]
    if static_mask is not None:
        bias = jnp.where(static_mask[None], bias, NEG)
    return bias


def _band(q_arr, qcol, k_arr, v_arr, kcol, vcol, n, l, heads, hd, bias, tq, window, past, out_rows):
    nq = l // tq
    lk = k_arr.shape[0] // n
    hb = 2
    assert heads % hb == 0 and qcol % hb == 0 and kcol % hb == 0 and vcol % hb == 0
    w = hb * hd
    return pl.pallas_call(
        functools.partial(_band_body, window=window, past=past, hb=hb), grid=(n, heads // hb, nq),
        in_specs=[
            pl.BlockSpec((tq, w), lambda b, h, i: (b * nq + i, qcol // hb + h)),
            pl.BlockSpec((lk, w), lambda b, h, i: (b, kcol // hb + h)),
            pl.BlockSpec((lk, w), lambda b, h, i: (b, vcol // hb + h)),
            pl.BlockSpec((hb, tq, window), lambda b, h, i: (h, 0, 0)),
        ],
        out_specs=pl.BlockSpec((tq, w), lambda b, h, i: (b * nq + i, h)),
        out_shape=jax.ShapeDtypeStruct((out_rows, heads * hd), BF16),
        compiler_params=_params("parallel", "parallel", "arbitrary"), name="band_attention",
    )(q_arr, k_arr, v_arr, bias)


def _band_decode_body(q_ref, kn_ref, vn_ref, kc_ref, vc_ref, bc_ref, bn_ref, o_ref, *, heads):
    d = q_ref.shape[1] // heads
    scale = d ** -0.5
    for h in range(heads):
        cols = slice(h * d, (h + 1) * d)
        q = _bf(q_ref[:, cols])
        sc = lax.dot_general(q, _bf(kc_ref[:, h, :]), _NT, preferred_element_type=F32) * scale + bc_ref[h]
        sn = lax.dot_general(q, _bf(kn_ref[:, cols]), _NT, preferred_element_type=F32) * scale + bn_ref[h]
        m = jnp.maximum(jnp.max(sc, axis=-1, keepdims=True), jnp.max(sn, axis=-1, keepdims=True))
        pc, pn = jnp.exp(sc - m), jnp.exp(sn - m)
        o = (jnp.dot(pc.astype(BF16), _bf(vc_ref[:, h, :]), preferred_element_type=F32)
             + jnp.dot(pn.astype(BF16), _bf(vn_ref[:, cols]), preferred_element_type=F32))
        o = o / (jnp.sum(pc, axis=-1, keepdims=True) + jnp.sum(pn, axis=-1, keepdims=True))
        o_ref[:, cols] = o.astype(o_ref.dtype)


def _band_decode(projb, n, l, heads, hd, k_cache, v_cache, layer, bias_c, bias_n):
    w = heads * hd
    r = k_cache.shape[1]
    new = lambda blk: pl.BlockSpec((l, w), lambda b: (b, blk))
    cache = pl.BlockSpec((None, r, heads, hd), lambda b: (layer * n + b, 0, 0, 0))
    return pl.pallas_call(
        functools.partial(_band_decode_body, heads=heads), grid=(n,),
        in_specs=[new(0), new(1), new(2), cache, cache,
                  pl.BlockSpec((heads, l, r), lambda b: (0, 0, 0)),
                  pl.BlockSpec((heads, l, l), lambda b: (0, 0, 0))],
        out_specs=pl.BlockSpec((l, w), lambda b: (b, 0)),
        out_shape=jax.ShapeDtypeStruct((n * l, w), BF16),
        compiler_params=_params("parallel"), name="band_decode",
    )(projb, projb, projb, k_cache, v_cache, bias_c, bias_n)


def _kv_rows_body(*refs, nl, heads, hd):
    k_refs, v_refs = refs[0:nl], refs[nl:2 * nl]
    ko_ref, vo_ref = refs[2 * nl], refs[2 * nl + 1]
    layer = pl.program_id(0)
    for li in range(nl):
        @pl.when(layer == li)
        def _(li=li):
            for h in range(heads):
                ko_ref[:, h, :] = k_refs[li][:, h * hd:(h + 1) * hd]
                vo_ref[:, h, :] = v_refs[li][:, h * hd:(h + 1) * hd]


def _kv_rows(projs, kblk, vblk, heads, hd):
    nl = len(projs)
    m = projs[0].shape[0]
    w = heads * hd
    tm = _pick(m, (256, 128, 64, 32, 16, 8))
    nt = m // tm

    def in_spec(li, blk):
        return pl.BlockSpec((tm, w), lambda layer, i: (jnp.clip(i + (layer - li) * nt, 0, nt - 1), blk))

    out_spec = pl.BlockSpec((None, tm, heads, hd), lambda layer, i: (layer, i, 0, 0))
    out_sds = jax.ShapeDtypeStruct((nl, m, heads, hd), F32)
    return pl.pallas_call(
        functools.partial(_kv_rows_body, nl=nl, heads=heads, hd=hd), grid=(nl, nt),
        in_specs=[in_spec(li, kblk) for li in range(nl)] + [in_spec(li, vblk) for li in range(nl)],
        out_specs=[out_spec, out_spec], out_shape=[out_sds, out_sds],
        compiler_params=_params("arbitrary", "arbitrary"), name="kv_rows",
    )(*projs, *projs)


def _trunk(x, n, l, mods, st, p, dims):
    m, d = x.shape
    depth = p['w_mod'].shape[0]
    alpha = (2 * depth) ** 0.25
    expanded = l % 256 != 0
    tm_big = m if expanded else _pick(l, (1024, 512, 256))
    tm_row = m if expanded else 256
    tn_wide = 512
    d_ff = p['ffn_w_gate'].shape[2]
    g_, p_, hg = dims['s5']
    mix_a = g_ * hg
    sb_heads, sb_hd = dims['sb']
    gla_heads, gla_dk, gla_dv = dims['gla']
    ca_heads, ca_hd = dims['ca']
    sb_w = sb_heads * sb_hd
    ca_w = ca_heads * ca_hd
    gla_w = gla_heads * gla_dv

    def mod_of(layer):
        if expanded:
            arr = jnp.repeat(mods[layer], l, axis=0)[None]
        else:
            arr = mods[layer][:, None, :]
        return _Mod(arr, d, l, expanded)

    outs = {k: [] for k in ('s5_re', 's5_im', 'gla', 'band_k', 'band_v')}
    projs_ab = []
    mod = mod_of(0)
    h = _modulate(x, mod, tm_row)
    for layer in range(depth):
        i = layer // 2
        if layer % 2 == 0:
            in_ab = p['ab_w_in'].shape[2]
            proj, = _matmul(h, [(p['ab_w_in'], i, 0)], n=in_ab, tm=tm_big, tn=tn_wide, tk=d, out_dtypes=[F32],
                            epilogue=_epi_identity, name="ab_in")
            gb = g_ // S5_LANE_GROUPS
            sw = S5_LANE_GROUPS * p_
            x0 = jnp.concatenate([st['s5_re'][i].reshape(n * gb, 1, sw), st['s5_im'][i].reshape(n * gb, 1, sw)], axis=2)
            t_rows = _pick(l, (512, 256, 128, 64, 32, 16, 8))
            tables = _s5_tables(*p['s5_raw'][i], seg=t_rows // SUBLANES)
            z, xo = _s5(proj, n, l, tables, p['s5_d'], i, x0, t_rows)
            outs['s5_re'].append(xo[:, 0, :sw].reshape(n, g_, p_))
            outs['s5_im'].append(xo[:, 0, sw:].reshape(n, g_, p_))
            tn_glu = tn_wide
            ya, = _matmul(z, [(p['s5_w_glu'], i, 0)], n=mix_a, tm=tm_big, tn=tn_glu, tk=mix_a, out_dtypes=[BF16],
                          epilogue=_epi_glu, name="s5_glu",
                          extras=[(z, (tm_big, tn_glu), lambda a, b: (a, b)),
                                  (p['s5_b_glu'], (None, 1, tn_glu), lambda a, b, i=i: (i, 0, b))])
            qc, kc, vc = mix_a // sb_hd, (mix_a + sb_w) // sb_hd, (mix_a + 2 * sb_w) // sb_hd
            prev = None if st['sb_k'] is None else (st['sb_k'], st['sb_v'], i)
            yb = _stick_breaking(proj, n, l, sb_heads, sb_hd, qc, kc, vc, prev)
            projs_ab.append(proj)
            merged = [ya, yb]
            w_out = p['ab_w_out']
        else:
            n_gla = 2 * gla_heads * gla_dk + 2 * gla_w
            proj, = _matmul(h, [(p['cd_w_gla'][i], None, 0)], n=n_gla, tm=tm_big, tn=tn_wide, tk=d, out_dtypes=[F32],
                            epilogue=_epi_identity, name="cd_in_gla")
            projb, = _matmul(h, [(p['cd_w_band'][i], None, 0)], n=3 * ca_w, tm=tm_big, tn=tn_wide, tk=d,
                             out_dtypes=[F32], epilogue=_epi_identity, name="cd_in_band")
            lr, = _matmul(h, [(p['cd_w_lrin'][i], None, 0)], n=LANES, tm=tm_big, tn=LANES, tk=d, out_dtypes=[F32],
                          epilogue=_epi_identity, name="cd_lr")
            s0t = jnp.swapaxes(st['gla'][i], -1, -2).reshape(n * gla_heads, gla_dv, gla_dk)
            yc, so = _gla(proj, lr, n, l, gla_heads, gla_dk, gla_dv, p['gla_w_lr_pad'], p['gla_b_lr'],
                          p['gla_norm_w'], i, s0t)
            outs['gla'].append(jnp.swapaxes(so.reshape(n, gla_heads, gla_dv, gla_dk), -1, -2))
            ka = projb[:, ca_w:2 * ca_w]
            va = projb[:, 2 * ca_w:3 * ca_w]
            table = p['ca_rel_bias'][i]
            if st['band_k'] is None:
                tq = _pick(l, (256, 128, 64))
                window = tq + BAND_PAST
                qchunk = np.arange(tq)[:, None] // CHUNK
                kchunk = np.arange(window)[None, :] // CHUNK
                static_mask = (kchunk >= qchunk) & (kchunk <= qchunk + BAND_PREV)
                bias = _toeplitz_bias(table, tq, window, BAND_PAST, static_mask)
                yd = _band(projb, 0, projb, projb, ca_heads, 2 * ca_heads, n, l, ca_heads, ca_hd, bias, tq, window,
                           BAND_PAST, m)
                keep = min(BAND_PAST, l)
                outs['band_k'].append(ka.reshape(n, l, ca_heads, ca_hd)[:, l - keep:])
                outs['band_v'].append(va.reshape(n, l, ca_heads, ca_hd)[:, l - keep:])
            else:
                r = st['band_k'].shape[1]
                bias = _toeplitz_bias(table, l, r + l, r)
                yd = _band_decode(projb, n, l, ca_heads, ca_hd, st['band_k'], st['band_v'], i,
                                  bias[:, :, :r], bias[:, :, r:])
                outs['band_k'].append(ka.reshape(n, l, ca_heads, ca_hd))
                outs['band_v'].append(va.reshape(n, l, ca_heads, ca_hd))
            merged = [yc, yd]
            w_out = p['cd_w_out']
        f, = _matmul(merged, [(w_out, i, 0)], n=d, tm=tm_big, tn=tn_wide, tk=sum(y.shape[1] for y in merged),
                     out_dtypes=[F32], epilogue=_epi_identity, name="mix_out")
        x, h = _deepnorm(x, f, mod, 2, p['ln1_w'], p['ln1_b'], layer, alpha, tm_row, nxt=(mod, 4, 3))
        tn_ff = _pick(d_ff, (256, 128))
        tall = not expanded and l % 2048 == 0
        act, = _matmul(h, [(p['ffn_w_gate'], layer, 0), (p['ffn_w_up'], layer, 0)], n=d_ff,
                       tm=2048 if tall else tm_big, tn=tn_ff, tk=d, out_dtypes=[BF16], epilogue=_epi_swiglu,
                       a_buffers=1 if tall else None, name="ffn_in")
        ff, = _matmul(act, [(p['ffn_w_down'], layer, 0)], n=d, tm=min(tm_big, 512), tn=tn_wide, tk=d_ff,
                      out_dtypes=[F32], epilogue=_epi_identity, name="ffn_out")
        if layer + 1 < depth:
            nmod = mod_of(layer + 1)
            x, h = _deepnorm(x, ff, mod, 5, p['ln2_w'], p['ln2_b'], layer, alpha, tm_row, nxt=(nmod, 1, 0))
            mod = nmod
        else:
            x, _ = _deepnorm(x, ff, mod, 5, p['ln2_w'], p['ln2_b'], layer, alpha, tm_row)
    outs = {k: jnp.stack(v) for k, v in outs.items()}
    assert mix_a % sb_w == 0
    sb_k, sb_v = _kv_rows(projs_ab, mix_a // sb_w + 1, mix_a // sb_w + 2, sb_heads, sb_hd)
    outs['sb_k'] = sb_k.reshape(len(projs_ab), n, l, sb_heads, sb_hd)
    outs['sb_v'] = sb_v.reshape(len(projs_ab), n, l, sb_heads, sb_hd)
    return x, outs


def kernel(x_prompt, x_sample, state_s5_re, state_s5_im, cache_sb_k, cache_sb_v, state_gla, cache_band_k, cache_band_v, c_prompt, c_sample, w_mod, b_mod, ln1_w, ln1_b, ln2_w, ln2_b, ab_w_in, ab_w_out, s5_a_re, s5_a_im, s5_log_dt, s5_b_re, s5_b_im, s5_c_re, s5_c_im, s5_d, s5_w_glu, s5_b_glu, cd_w_in, cd_w_out, gla_w_lr, gla_b_lr, gla_norm_w, ca_rel_bias, ffn_w_gate, ffn_w_up, ffn_w_down):
    nb, seq, d = x_prompt.shape
    db, dseq, _ = x_sample.shape
    depth = w_mod.shape[0]
    n_even, g_, p_ = s5_a_re.shape
    hg = s5_b_re.shape[-1]
    _, _, past, sb_heads, sb_hd = cache_sb_k.shape
    n_odd, _, gla_heads, gla_dk, gla_dv = state_gla.shape
    _, _, band_rows, ca_heads, ca_hd = cache_band_k.shape
    rank = gla_w_lr.shape[1]
    dims = {'s5': (g_, p_, hg), 'sb': (sb_heads, sb_hd), 'gla': (gla_heads, gla_dk, gla_dv), 'ca': (ca_heads, ca_hd)}

    lr0 = 2 * gla_heads * gla_dk + 2 * gla_heads * gla_dv
    p = dict(
        w_mod=w_mod, ln1_w=ln1_w[:, None, :], ln1_b=ln1_b[:, None, :], ln2_w=ln2_w[:, None, :], ln2_b=ln2_b[:, None, :],
        ab_w_in=ab_w_in, ab_w_out=ab_w_out, s5_d=s5_d[:, None, :], s5_w_glu=s5_w_glu, s5_b_glu=s5_b_glu[:, None, :],
        cd_w_out=cd_w_out, gla_b_lr=gla_b_lr[:, None, :], gla_norm_w=gla_norm_w[:, None, :], ca_rel_bias=ca_rel_bias,
        ffn_w_gate=ffn_w_gate, ffn_w_up=ffn_w_up, ffn_w_down=ffn_w_down.astype(BF16),
        cd_w_gla=[cd_w_in[i, :, :lr0] for i in range(n_odd)],
        cd_w_band=[cd_w_in[i, :, lr0 + rank:] for i in range(n_odd)],
        cd_w_lrin=[jnp.pad(cd_w_in[i, :, lr0:lr0 + rank], ((0, 0), (0, LANES - rank))) for i in range(n_odd)],
        gla_w_lr_pad=jnp.pad(gla_w_lr, ((0, 0), (0, LANES - rank), (0, 0))),
        s5_raw=[(s5_a_re[i], s5_a_im[i], s5_log_dt[i], s5_b_re[i], s5_b_im[i], s5_c_re[i], s5_c_im[i])
                for i in range(n_even)],
    )

    rows = nb + db
    rows_pad = -(-rows // 16) * 16
    c_all = jnp.concatenate([c_prompt, c_sample, jnp.zeros((rows_pad - rows, d), F32)], axis=0)
    mods = []
    for layer in range(depth):
        tn_mod = _pick(6 * d, (1024, 512, 256, 128))
        mod_l, = _matmul(c_all, [(w_mod, layer, 0)], n=6 * d, tm=rows_pad, tn=tn_mod, tk=d, out_dtypes=[F32],
                         epilogue=_epi_bias, prologue=_silu, name="adaln_mod",
                         extras=[(b_mod[:, None, :], (None, 1, tn_mod), lambda a, b, layer=layer: (layer, 0, b))])
        mods.append(mod_l)
    mods = jnp.stack(mods)
    mods_p, mods_s = mods[:, :nb], mods[:, nb:rows]

    zeros_s5 = jnp.zeros((n_even, nb, g_, p_), F32)
    st_p = dict(s5_re=zeros_s5, s5_im=zeros_s5, sb_k=None, sb_v=None,
                gla=jnp.zeros((n_odd, nb, gla_heads, gla_dk, gla_dv), F32), band_k=None, band_v=None)
    y_p, o_p = _trunk(x_prompt.reshape(nb * seq, d), nb, seq, mods_p, st_p, p, dims)
    st_s = dict(s5_re=state_s5_re, s5_im=state_s5_im,
                sb_k=cache_sb_k.reshape(n_even * db, past, sb_heads, sb_hd),
                sb_v=cache_sb_v.reshape(n_even * db, past, sb_heads, sb_hd),
                gla=state_gla,
                band_k=cache_band_k.reshape(n_odd * db, band_rows, ca_heads, ca_hd),
                band_v=cache_band_v.reshape(n_odd * db, band_rows, ca_heads, ca_hd))
    y_s, o_s = _trunk(x_sample.reshape(db * dseq, d), db, dseq, mods_s, st_s, p, dims)
    return (y_p.reshape(nb, seq, d), y_s.reshape(db, dseq, d),
            o_p['s5_re'], o_p['s5_im'], o_p['sb_k'], o_p['sb_v'], o_p['gla'], o_p['band_k'], o_p['band_v'],
            o_s['s5_re'], o_s['s5_im'], o_s['sb_k'], o_s['sb_v'], o_s['gla'], o_s['band_k'], o_s['band_v'])
```

```python
import functools
import math

import numpy as np
import jax
import jax.numpy as jnp
from jax import lax
from jax.experimental import pallas as pl
from jax.experimental.pallas import tpu as pltpu

F32 = jnp.float32
BF16 = jnp.bfloat16

CHUNK = 64
BAND_PREV = 8
BAND_PAST = BAND_PREV * CHUNK
GLA_GATE_NORM = 16.0
LN_EPS = 1e-5
RMS_EPS = 1e-6

LANES = 128
SUBLANES = 8
V7X_VMEM_BYTES = 64 * 1024 * 1024
VMEM_LIMIT = V7X_VMEM_BYTES - 8 * 1024 * 1024

GLA_SUB = 16
S5_LANE_GROUPS = 8
NEG = -1e30
SB_EXIT = -104.0

_NT = (((1,), (1,)), ((), ()))
_TN = (((0,), (0,)), ((), ()))


def _pick(dim, candidates):
    for c in candidates:
        if c <= dim and dim % c == 0:
            return c
    return dim


def _params(*sem):
    return pltpu.CompilerParams(dimension_semantics=sem, vmem_limit_bytes=VMEM_LIMIT)


def _bf(x):
    return x if x.dtype == BF16 else x.astype(BF16)


def _log_sigmoid(z):
    return jnp.minimum(z, 0.0) - jnp.log(1.0 + jnp.exp(-jnp.abs(z)))


def _silu(x):
    return x * jax.nn.sigmoid(x)


def _mm_body(*refs, na, nb, ne, no, nk, prologue, epilogue):
    a_refs, refs = refs[:na], refs[na:]
    b_refs = refs[0:nb]
    e_refs = refs[nb:nb + ne]
    o_refs = refs[nb + ne:nb + ne + no]
    acc_refs = refs[nb + ne + no:]
    a_parts = [a_ref[...] for a_ref in a_refs]
    if prologue is not None:
        a_parts = [prologue(a) for a in a_parts]
    a_parts = [_bf(a) for a in a_parts]
    a = a_parts[0] if na == 1 else jnp.concatenate(a_parts, axis=1)
    parts = [jnp.dot(a, _bf(b_ref[...]), preferred_element_type=F32) for b_ref in b_refs]

    def finish(accs):
        outs = epilogue(accs, [e[...] for e in e_refs])
        for o_ref, o in zip(o_refs, outs):
            o_ref[...] = o.astype(o_ref.dtype)

    if nk == 1:
        finish(parts)
        return
    k = pl.program_id(2)

    @pl.when(k == 0)
    def _():
        for acc, p in zip(acc_refs, parts):
            acc[...] = p

    @pl.when(k > 0)
    def _():
        for acc, p in zip(acc_refs, parts):
            acc[...] += p

    @pl.when(k == nk - 1)
    def _():
        finish([acc[...] for acc in acc_refs])


def _matmul(a, bs, *, n, tm, tn, tk, out_dtypes, epilogue, extras=(), prologue=None, a_buffers=None,
            name="matmul"):
    a_list = list(a) if isinstance(a, (list, tuple)) else [a]
    m = a_list[0].shape[0]
    kdim = sum(x.shape[1] for x in a_list)
    nk = kdim // tk
    assert len(a_list) == 1 or nk == 1
    grid = (m // tm, n // tn, nk)
    if len(a_list) == 1:
        mode = {} if a_buffers is None else {"pipeline_mode": pl.Buffered(a_buffers)}
        in_specs = [pl.BlockSpec((tm, tk), lambda i, j, k: (i, k), **mode)]
    else:
        in_specs = [pl.BlockSpec((tm, x.shape[1]), lambda i, j, k: (i, 0)) for x in a_list]
    operands = list(a_list)
    for arr, layer, c0 in bs:
        if arr.ndim == 3:
            in_specs.append(pl.BlockSpec((None, tk, tn), lambda i, j, k, layer=layer, c0=c0: (layer, k, c0 + j)))
        else:
            in_specs.append(pl.BlockSpec((tk, tn), lambda i, j, k, c0=c0: (k, c0 + j)))
        operands.append(arr)
    for arr, shape, imap in extras:
        in_specs.append(pl.BlockSpec(shape, lambda i, j, k, imap=imap: imap(i, j)))
        operands.append(arr)
    out_shape = [jax.ShapeDtypeStruct((m, n), dt) for dt in out_dtypes]
    out_specs = [pl.BlockSpec((tm, tn), lambda i, j, k: (i, j)) for _ in out_dtypes]
    scratch = [pltpu.VMEM((tm, tn), F32) for _ in bs] if nk > 1 else []
    body = functools.partial(_mm_body, na=len(a_list), nb=len(bs), ne=len(extras), no=len(out_dtypes), nk=nk,
                             prologue=prologue, epilogue=epilogue)
    outs = pl.pallas_call(
        body, grid=grid, in_specs=in_specs, out_specs=out_specs, out_shape=out_shape,
        scratch_shapes=scratch, compiler_params=_params("parallel", "parallel", "arbitrary"), name=name,
    )(*operands)
    return outs


def _epi_identity(accs, extras):
    return [accs[0]]


def _epi_bias(accs, extras):
    return [accs[0] + extras[0]]


def _epi_swiglu(accs, extras):
    return [_silu(accs[0]) * accs[1]]


def _epi_glu(accs, extras):
    z, bias = extras
    return [z * jax.nn.sigmoid(accs[0] + bias)]


class _Mod:
    def __init__(self, arr, d, l, expanded):
        self.arr, self.d, self.l, self.expanded = arr, d, l, expanded

    def spec(self, piece, tm):
        d = self.d
        if self.expanded:
            return pl.BlockSpec((None, tm, d), lambda i, *_: (0, i, piece))
        per = self.l // tm
        return pl.BlockSpec((None, 1, d), lambda i, *_: (i // per, 0, piece))


def _modulate_body(x_ref, sc_ref, sh_ref, h_ref):
    h_ref[...] = (x_ref[...] * (1.0 + sc_ref[...]) + sh_ref[...]).astype(h_ref.dtype)


def _modulate(x, mod, tm):
    m, d = x.shape
    row = pl.BlockSpec((tm, d), lambda i: (i, 0))
    return pl.pallas_call(
        _modulate_body, grid=(m // tm,), in_specs=[row, mod.spec(1, tm), mod.spec(0, tm)],
        out_specs=row, out_shape=jax.ShapeDtypeStruct((m, d), BF16),
        compiler_params=_params("parallel"), name="modulate",
    )(x, mod.arr, mod.arr)


def _ln_body(*refs, alpha, has_next):
    if has_next:
        x_ref, f_ref, g_ref, w_ref, b_ref, sc_ref, sh_ref, xo_ref, ho_ref = refs
    else:
        x_ref, f_ref, g_ref, w_ref, b_ref, xo_ref = refs
    y = alpha * x_ref[...] + (1.0 + g_ref[...]) * f_ref[...]
    yc = y - jnp.mean(y, axis=-1, keepdims=True)
    var = jnp.mean(yc * yc, axis=-1, keepdims=True)
    xn = yc * lax.rsqrt(var + LN_EPS) * w_ref[...] + b_ref[...]
    xo_ref[...] = xn
    if has_next:
        ho_ref[...] = (xn * (1.0 + sc_ref[...]) + sh_ref[...]).astype(ho_ref.dtype)


def _deepnorm(x, f, mod, gate_piece, w, b, layer, alpha, tm, nxt=None):
    m, d = x.shape
    row = pl.BlockSpec((tm, d), lambda i: (i, 0))
    vec = pl.BlockSpec((None, 1, d), lambda i: (layer, 0, 0))
    in_specs = [row, row, mod.spec(gate_piece, tm), vec, vec]
    operands = [x, f, mod.arr, w, b]
    out_shape = [jax.ShapeDtypeStruct((m, d), F32)]
    out_specs = [row]
    if nxt is not None:
        nmod, sc_piece, sh_piece = nxt
        in_specs += [nmod.spec(sc_piece, tm), nmod.spec(sh_piece, tm)]
        operands += [nmod.arr, nmod.arr]
        out_shape.append(jax.ShapeDtypeStruct((m, d), BF16))
        out_specs.append(row)
    outs = pl.pallas_call(
        functools.partial(_ln_body, alpha=alpha, has_next=nxt is not None),
        grid=(m // tm,), in_specs=in_specs, out_specs=out_specs, out_shape=out_shape,
        compiler_params=_params("parallel"), name="deepnorm",
    )(*operands)
    return outs if nxt is not None else (outs[0], None)


def _s5_body(u_ref, bt_ref, ct_ref, sc_ref, d_ref, x0_ref, z_ref, xo_ref, xs_ref, st_ref, *, t_rows, independent):
    t = pl.program_id(2)
    sw = st_ref.shape[-1] // 2

    if not independent:
        @pl.when(t == 0)
        def _():
            st_ref[...] = x0_ref[...]

    seg = t_rows // SUBLANES
    u = jnp.concatenate([u_ref[pl.ds(j, SUBLANES, stride=seg), :] for j in range(seg)], axis=0)
    xs_ref[...] = jnp.dot(_bf(u), bt_ref[...], preferred_element_type=F32)
    ar, ai = sc_ref[0:8, :], sc_ref[8:16, :]
    unroll = 2 if seg % 2 == 0 else 1

    def cmul_add(xr, xi, ar, ai, sr, si):
        return xr + ar * sr - ai * si, xi + ar * si + ai * sr

    def local(j, carry):
        off = pl.multiple_of(j * SUBLANES, SUBLANES)
        xr, xi = cmul_add(xs_ref[pl.ds(off, SUBLANES), 0:sw], xs_ref[pl.ds(off, SUBLANES), sw:2 * sw],
                          ar, ai, carry[0], carry[1])
        xs_ref[pl.ds(off, SUBLANES), 0:sw] = xr
        xs_ref[pl.ds(off, SUBLANES), sw:2 * sw] = xi
        return xr, xi

    zero = jnp.zeros((SUBLANES, sw), F32)
    gr, gi = lax.fori_loop(0, seg, local, (zero, zero), unroll=unroll)

    if independent:
        cr, ci = x0_ref[:, 0:sw], x0_ref[:, sw:2 * sw]
    else:
        for step, d in enumerate((1, 2, 4)):
            r0 = 16 + 16 * step
            gr, gi = cmul_add(gr, gi, sc_ref[r0:r0 + 8, :], sc_ref[r0 + 8:r0 + 16, :],
                              pltpu.roll(gr, d, 0), pltpu.roll(gi, d, 0))
        st = st_ref[...]
        sr, si = st[:, 0:sw], st[:, sw:2 * sw]
        gr, gi = cmul_add(gr, gi, sc_ref[64:72, :], sc_ref[72:80, :], sr, si)
        fr, fi = gr[SUBLANES - 1:SUBLANES, :], gi[SUBLANES - 1:SUBLANES, :]
        st_ref[:, 0:sw] = fr
        st_ref[:, sw:2 * sw] = fi
        row = lax.broadcasted_iota(jnp.int32, (SUBLANES, sw), 0)
        cr = jnp.where(row == 0, sr, pltpu.roll(gr, 1, 0))
        ci = jnp.where(row == 0, si, pltpu.roll(gi, 1, 0))

    def fix(j, carry):
        off = pl.multiple_of(j * SUBLANES, SUBLANES)
        dr = ar * carry[0] - ai * carry[1]
        di = ar * carry[1] + ai * carry[0]
        xs_ref[pl.ds(off, SUBLANES), 0:sw] += dr
        xs_ref[pl.ds(off, SUBLANES), sw:2 * sw] += di
        return dr, di

    dr, di = lax.fori_loop(0, seg, fix, (cr, ci), unroll=unroll)
    if independent:
        fr, fi = gr + dr, gi + di

    y = jnp.dot(_bf(xs_ref[...]), ct_ref[...], preferred_element_type=F32) + d_ref[...] * u
    z = 0.5 * y * (1.0 + jnp.tanh(math.sqrt(2.0 / math.pi) * (y + 0.044715 * (y * y * y))))
    for j in range(seg):
        z_ref[pl.ds(j, SUBLANES, stride=seg), :] = z[j * SUBLANES:(j + 1) * SUBLANES, :]

    @pl.when(t == pl.num_programs(2) - 1)
    def _():
        xo_ref[:, 0:sw] = fr
        xo_ref[:, sw:2 * sw] = fi


def _s5_tables(a_re, a_im, log_dt, b_re, b_im, c_re, c_im, seg):
    g, p = a_re.shape
    hg = b_re.shape[-1]
    gb = g // S5_LANE_GROUPS
    lr_, li_ = a_re.astype(F32), a_im.astype(F32)
    dt = jnp.exp(log_dt.astype(F32))[:, None]
    mag = jnp.exp(lr_ * dt)
    ab_r, ab_i = mag * jnp.cos(li_ * dt), mag * jnp.sin(li_ * dt)
    den = lr_ * lr_ + li_ * li_
    cr_ = ((ab_r - 1.0) * lr_ + ab_i * li_) / den
    ci_ = (ab_i * lr_ - (ab_r - 1.0) * li_) / den
    br_, bi_ = b_re.astype(F32), b_im.astype(F32)
    bb_r = cr_[..., None] * br_ - ci_[..., None] * bi_
    bb_i = cr_[..., None] * bi_ + ci_[..., None] * br_
    eye = jnp.eye(S5_LANE_GROUPS, dtype=F32)

    def blockdiag_in(w):
        w = w.reshape(gb, S5_LANE_GROUPS, p, hg)
        return jnp.einsum('bgph,gk->bghkp', w, eye).reshape(gb, S5_LANE_GROUPS * hg, S5_LANE_GROUPS * p)

    def blockdiag_out(w):
        w = w.reshape(gb, S5_LANE_GROUPS, hg, p)
        return jnp.einsum('bgkp,gj->bgpjk', w, eye).reshape(gb, S5_LANE_GROUPS * p, S5_LANE_GROUPS * hg)

    bt = jnp.concatenate([blockdiag_in(bb_r), blockdiag_in(bb_i)], axis=2).astype(BF16)
    ct = jnp.concatenate([blockdiag_out(c_re.astype(F32)), blockdiag_out(-c_im.astype(F32))], axis=1).astype(BF16)
    sw = S5_LANE_GROUPS * p

    def rows8(z):
        return jnp.broadcast_to(z.reshape(gb, 1, sw), (gb, SUBLANES, sw))

    def cmul(x, y):
        return x[0] * y[0] - x[1] * y[1], x[0] * y[1] + x[1] * y[0]

    a_seg, sq, e = None, (ab_r, ab_i), seg
    while e:
        if e & 1:
            a_seg = sq if a_seg is None else cmul(a_seg, sq)
        sq, e = cmul(sq, sq), e >> 1
    pows = [a_seg]
    for _ in range(SUBLANES - 1):
        pows.append(cmul(pows[-1], a_seg))
    pieces = [rows8(ab_r), rows8(ab_i)]
    trow = jnp.arange(SUBLANES)[None, :, None]
    for d in (1, 2, 4):
        pieces += [jnp.where(trow >= d, rows8(pows[d - 1][part]), 0.0) for part in (0, 1)]
    for part in (0, 1):
        pieces.append(jnp.stack([pw[part] for pw in pows], axis=0).reshape(SUBLANES, gb, sw).transpose(1, 0, 2))
    sc = jnp.concatenate(pieces, axis=1)
    return bt, ct, sc


def _s5(proj, n, l, tables, d_skip, layer, x0, t_rows, independent=False):
    bt, ct, sc = tables
    gb, cw, sw2 = bt.shape
    if independent:
        assert t_rows == SUBLANES * l and n % SUBLANES == 0
        nt, grid = 1, (n // SUBLANES, gb, 1)
        st_spec = pl.BlockSpec((None, SUBLANES, sw2), lambda b, g, t: (g, b, 0))
        st_shape = jax.ShapeDtypeStruct((gb, n, sw2), F32)
    else:
        nt = l // t_rows
        grid = (n, gb, nt)
        st_spec = pl.BlockSpec((None, 1, sw2), lambda b, g, t: (b * gb + g, 0, 0))
        st_shape = jax.ShapeDtypeStruct((n * gb, 1, sw2), F32)
    z, xo = pl.pallas_call(
        functools.partial(_s5_body, t_rows=t_rows, independent=independent), grid=grid,
        in_specs=[
            pl.BlockSpec((t_rows, cw), lambda b, g, t: (b * nt + t, g)),
            pl.BlockSpec((None, cw, sw2), lambda b, g, t: (g, 0, 0)),
            pl.BlockSpec((None, sw2, cw), lambda b, g, t: (g, 0, 0)),
            pl.BlockSpec((None, 10 * SUBLANES, sw2 // 2), lambda b, g, t: (g, 0, 0)),
            pl.BlockSpec((None, 1, cw), lambda b, g, t: (layer, 0, g)),
            st_spec,
        ],
        out_specs=[pl.BlockSpec((t_rows, cw), lambda b, g, t: (b * nt + t, g)), st_spec],
        out_shape=[jax.ShapeDtypeStruct((n * l, gb * cw), F32), st_shape],
        scratch_shapes=[pltpu.VMEM((t_rows, sw2), F32), pltpu.VMEM((1, sw2), F32)],
        compiler_params=_params("parallel", "parallel", "arbitrary"), name="s5_scan",
    )(proj, bt, ct, sc, d_skip, x0)
    return z, xo


def _strict_upper2(tk):
    rj = lax.broadcasted_iota(jnp.int32, (tk, tk), 0)
    cs = lax.broadcasted_iota(jnp.int32, (tk, tk), 1)
    upper = jnp.where(rj > cs, 1.0, 0.0).astype(BF16)
    return jnp.concatenate([upper, upper], axis=0)


def _sb_tile(q, kt, vt, carry, mask, scale, upper2):
    z = lax.dot_general(q, _bf(kt), _NT, preferred_element_type=F32) * scale
    ls = _log_sigmoid(z)
    lf = ls - z
    if mask is not None:
        lf = jnp.where(mask, lf, 0.0)
    hi = lf.astype(BF16)
    lo = (lf - hi.astype(F32)).astype(BF16)
    newer = jnp.dot(jnp.concatenate([hi, lo], axis=1), upper2, preferred_element_type=F32)
    w = jnp.exp(ls + carry + newer)
    if mask is not None:
        w = jnp.where(mask, w, 0.0)
    out = jnp.dot(w.astype(BF16), _bf(vt), preferred_element_type=F32)
    return out, carry + jnp.sum(lf, axis=1, keepdims=True)


def _sb_body(q_ref, kd_ref, vd_ref, kp_ref, vp_ref, o_ref, *, tk, prev_tiles, hb):
    tq = q_ref.shape[0]
    d = q_ref.shape[1] // hb
    scale = d ** -0.5
    r = lax.broadcasted_iota(jnp.int32, (tq, tq), 0)
    c = lax.broadcasted_iota(jnp.int32, (tq, tq), 1)
    upper2 = _strict_upper2(tk)
    upper2_diag = upper2 if tq == tk else _strict_upper2(tq)
    nprev = prev_tiles(pl.program_id(2))
    first_off = pl.multiple_of(jnp.maximum(nprev - 1, 0) * tk, tk)

    def earlier(ref, off, h):
        if len(ref.shape) == 3:
            return ref[pl.ds(off, tk), h, :]
        return ref[pl.ds(off, tk), h * d:(h + 1) * d]

    states = []
    for h in range(hb):
        cols = slice(h * d, (h + 1) * d)
        q = _bf(q_ref[:, cols])
        acc, carry = _sb_tile(q, kd_ref[:, cols], vd_ref[:, cols], jnp.zeros((tq, 1), F32), c < r, scale,
                              upper2_diag)
        out, carry = _sb_tile(q, earlier(kp_ref, first_off, h), earlier(vp_ref, first_off, h), carry,
                              nprev > 0, scale, upper2)
        states.append((q, acc + out, carry))

    def cond(state):
        j, _, _, cmax = state
        return jnp.logical_and(j >= 0, cmax > SB_EXIT)

    for h, (q, acc, carry) in enumerate(states):
        cols = slice(h * d, (h + 1) * d)

        def body(state, q=q, h=h):
            j, acc, carry, _ = state
            off = pl.multiple_of(j * tk, tk)
            out, carry = _sb_tile(q, earlier(kp_ref, off, h), earlier(vp_ref, off, h), carry, None, scale, upper2)
            return j - 1, acc + out, carry, jnp.max(carry)

        first = jnp.asarray(nprev - 2, jnp.int32)
        _, acc, _, _ = lax.while_loop(cond, body, (first, acc, carry, jnp.max(carry)))
        o_ref[:, cols] = acc.astype(o_ref.dtype)


def _stick_breaking(proj, n, l, heads, hd, qcol, kcol, vcol, prev=None):
    if prev is None:
        hb = 2
        tq = _pick(l, (256, 128, 64, 32, 16, 8))
        tk = tq
        nq = l // tq
        kp, vp = proj, proj
        kp_spec = pl.BlockSpec((l, hb * hd), lambda b, h, i: (b, kcol // hb + h))
        vp_spec = pl.BlockSpec((l, hb * hd), lambda b, h, i: (b, vcol // hb + h))
        prev_tiles = lambda i: i
    else:
        hb = heads
        kc, vc, layer = prev
        past = kc.shape[1]
        tq, nq = l, 1
        tk = _pick(past, (256, 128, 64, 32, 16, 8))
        kp, vp = kc, vc
        kp_spec = pl.BlockSpec((None, past, heads, hd), lambda b, h, i: (layer * n + b, 0, 0, 0))
        vp_spec = pl.BlockSpec((None, past, heads, hd), lambda b, h, i: (layer * n + b, 0, 0, 0))
        prev_tiles = lambda i: past // tk
    assert heads % hb == 0 and qcol % hb == 0 and kcol % hb == 0 and vcol % hb == 0
    w = hb * hd
    return pl.pallas_call(
        functools.partial(_sb_body, tk=tk, prev_tiles=prev_tiles, hb=hb), grid=(n, heads // hb, nq),
        in_specs=[
            pl.BlockSpec((tq, w), lambda b, h, i: (b * nq + i, qcol // hb + h)),
            pl.BlockSpec((tq, w), lambda b, h, i: (b * nq + i, kcol // hb + h)),
            pl.BlockSpec((tq, w), lambda b, h, i: (b * nq + i, vcol // hb + h)),
            kp_spec, vp_spec,
        ],
        out_specs=pl.BlockSpec((tq, w), lambda b, h, i: (b * nq + i, h)),
        out_shape=jax.ShapeDtypeStruct((n * l, heads * hd), BF16),
        compiler_params=_params("parallel", "parallel", "arbitrary"), name="stick_breaking",
    )(proj, proj, proj, kp, vp)


def _gla_body(q_ref, k_ref, v_ref, g_ref, lr_ref, wlr_ref, blr_ref, nw_ref, s0_ref, y_ref, so_ref, st_ref, *, heads):
    t = pl.program_id(1)
    c = q_ref.shape[0]

    @pl.when(t == 0)
    def _():
        st_ref[...] = s0_ref[...]

    gate_in = jnp.dot(_bf(lr_ref[...]), _bf(wlr_ref[...]), preferred_element_type=F32) + blr_ref[...]
    logg = _log_sigmoid(gate_in) * (1.0 / GLA_GATE_NORM)
    rt = lax.broadcasted_iota(jnp.int32, (c, c), 0)
    cs = lax.broadcasted_iota(jnp.int32, (c, c), 1)
    lower = jnp.where(cs <= rt, 1.0, 0.0).astype(BF16)
    hi = logg.astype(BF16)
    lo = (logg - hi.astype(F32)).astype(BF16)
    b_all = jnp.dot(lower, hi, preferred_element_type=F32) + jnp.dot(lower, lo, preferred_element_type=F32)
    last = t == pl.num_programs(1) - 1
    for hh in range(heads):
        _gla_head(hh, b_all, q_ref, k_ref, v_ref, g_ref, nw_ref, y_ref, so_ref, st_ref, last, heads)


def _gla_head(hh, b_all, q_ref, k_ref, v_ref, g_ref, nw_ref, y_ref, so_ref, st_ref, last, heads):
    c = q_ref.shape[0]
    dk = q_ref.shape[1] // heads
    dv = v_ref.shape[1] // heads
    q = q_ref[:, hh * dk:(hh + 1) * dk] * (dk ** -0.5)
    k = k_ref[:, hh * dk:(hh + 1) * dk]
    vb = _bf(v_ref[:, hh * dv:(hh + 1) * dv])
    b = b_all[:, hh * dk:(hh + 1) * dk]
    st = st_ref[hh]
    o_inter = lax.dot_general(_bf(q * jnp.exp(b)), _bf(st), _NT, preferred_element_type=F32)
    row = lax.broadcasted_iota(jnp.int32, (GLA_SUB, 1), 0)
    lane = lax.broadcasted_iota(jnp.int32, (GLA_SUB, GLA_SUB), 1)
    outs = []
    for blk in range(c // GLA_SUB):
        r0, r1 = blk * GLA_SUB, (blk + 1) * GLA_SUB
        bi, qi, ki = b[r0:r1], q[r0:r1], k[r0:r1]
        att = jnp.zeros((GLA_SUB, GLA_SUB), F32)
        for s in range(GLA_SUB):
            dec = jnp.exp(jnp.where(row >= s, bi - bi[s:s + 1, :], NEG))
            col = jnp.sum(qi * dec * ki[s:s + 1, :], axis=1, keepdims=True)
            att = jnp.where(lane == s, col, att)
        oi = jnp.dot(_bf(att), vb[r0:r1], preferred_element_type=F32)
        if blk > 0:
            bref = b[r0 - 1:r0, :]
            qt = qi * jnp.exp(bi - bref)
            kt = k[0:r0] * jnp.exp(bref - b[0:r0])
            a_off = lax.dot_general(_bf(qt), _bf(kt), _NT, preferred_element_type=F32)
            oi = oi + jnp.dot(_bf(a_off), vb[0:r0], preferred_element_type=F32)
        outs.append(oi)
    o = (outs[0] if len(outs) == 1 else jnp.concatenate(outs, axis=0)) + o_inter
    o = o * lax.rsqrt(jnp.mean(o * o, axis=-1, keepdims=True) + RMS_EPS) * nw_ref[...]
    y_ref[:, hh * dv:(hh + 1) * dv] = (o * _silu(g_ref[:, hh * dv:(hh + 1) * dv])).astype(y_ref.dtype)

    bl = b[c - 1:c, :]
    kh = k * jnp.exp(bl - b)
    new = st * jnp.exp(bl) + lax.dot_general(vb, _bf(kh), _TN, preferred_element_type=F32)
    st_ref[hh] = new

    @pl.when(last)
    def _():
        so_ref[hh] = new


def _gla(proj, lr, n, l, heads, dk, dv, w_lr, b_lr, norm_w, layer, s0t):
    c = min(CHUNK, l)
    nc = l // c
    wk, wv = heads * dk, heads * dv
    assert (2 * wk) % wv == 0
    v_blk = (2 * wk) // wv
    rank_pad = lr.shape[1]
    y, so = pl.pallas_call(
        functools.partial(_gla_body, heads=heads), grid=(n, nc),
        in_specs=[
            pl.BlockSpec((c, wk), lambda b, t: (b * nc + t, 0)),
            pl.BlockSpec((c, wk), lambda b, t: (b * nc + t, 1)),
            pl.BlockSpec((c, wv), lambda b, t: (b * nc + t, v_blk)),
            pl.BlockSpec((c, wv), lambda b, t: (b * nc + t, v_blk + 1)),
            pl.BlockSpec((c, rank_pad), lambda b, t: (b * nc + t, 0)),
            pl.BlockSpec((None, rank_pad, wk), lambda b, t: (layer, 0, 0)),
            pl.BlockSpec((None, 1, wk), lambda b, t: (layer, 0, 0)),
            pl.BlockSpec((None, 1, dv), lambda b, t: (layer, 0, 0)),
            pl.BlockSpec((heads, dv, dk), lambda b, t: (b, 0, 0)),
        ],
        out_specs=[
            pl.BlockSpec((c, wv), lambda b, t: (b * nc + t, 0)),
            pl.BlockSpec((heads, dv, dk), lambda b, t: (b, 0, 0)),
        ],
        out_shape=[jax.ShapeDtypeStruct((n * l, wv), BF16),
                   jax.ShapeDtypeStruct((n * heads, dv, dk), F32)],
        scratch_shapes=[pltpu.VMEM((heads, dv, dk), F32)],
        compiler_params=_params("parallel", "arbitrary"), name="gla",
    )(proj, proj, proj, proj, lr, w_lr, b_lr, norm_w, s0t)
    return y, so


def _band_body(q_ref, k_ref, v_ref, bias_ref, o_ref, *, window, past, hb):
    tq = q_ref.shape[0]
    d = q_ref.shape[1] // hb
    qs = pl.program_id(2) * tq
    starts = [pl.multiple_of(jnp.maximum(qs - past + blk * tq, 0), tq) for blk in range(window // tq)]
    exists = lax.broadcasted_iota(jnp.int32, (tq, window), 1) >= past - qs
    for h in range(hb):
        cols = slice(h * d, (h + 1) * d)
        q = _bf(q_ref[:, cols])
        kw = jnp.concatenate([_bf(k_ref[pl.ds(st, tq), cols]) for st in starts], axis=0)
        vw = jnp.concatenate([_bf(v_ref[pl.ds(st, tq), cols]) for st in starts], axis=0)
        s = lax.dot_general(q, kw, _NT, preferred_element_type=F32) * (d ** -0.5) + bias_ref[h]
        s = jnp.where(exists, s, NEG)
        p = jnp.exp(s - jnp.max(s, axis=-1, keepdims=True))
        o = jnp.dot(p.astype(BF16), vw, preferred_element_type=F32) / jnp.sum(p, axis=-1, keepdims=True)
        o_ref[:, cols] = o.astype(o_ref.dtype)


def _toeplitz_bias(table, tq, window, offset, static_mask=None):
    rows, heads = table.shape
    clip = (rows - 1) // 2
    period = tq + window
    m = np.arange(period)
    m = np.where(m >= window, m - period, m)
    idx = np.clip(offset - m, -clip, clip) + clip
    vec = table.astype(F32)[idx, :].T
    flat = jnp.tile(vec, (1, tq))[:, :tq * (period - 1)]
    bias = flat.reshape(heads, tq, period - 1)[:, :, :window]
    if static_mask is not None:
        bias = jnp.where(static_mask[None], bias, NEG)
    return bias


def _band(q_arr, qcol, k_arr, v_arr, kcol, vcol, n, l, heads, hd, bias, tq, window, past, out_rows):
    nq = l // tq
    lk = k_arr.shape[0] // n
    hb = 2
    assert heads % hb == 0 and qcol % hb == 0 and kcol % hb == 0 and vcol % hb == 0
    w = hb * hd
    return pl.pallas_call(
        functools.partial(_band_body, window=window, past=past, hb=hb), grid=(n, heads // hb, nq),
        in_specs=[
            pl.BlockSpec((tq, w), lambda b, h, i: (b * nq + i, qcol // hb + h)),
            pl.BlockSpec((lk, w), lambda b, h, i: (b, kcol // hb + h)),
            pl.BlockSpec((lk, w), lambda b, h, i: (b, vcol // hb + h)),
            pl.BlockSpec((hb, tq, window), lambda b, h, i: (h, 0, 0)),
        ],
        out_specs=pl.BlockSpec((tq, w), lambda b, h, i: (b * nq + i, h)),
        out_shape=jax.ShapeDtypeStruct((out_rows, heads * hd), BF16),
        compiler_params=_params("parallel", "parallel", "arbitrary"), name="band_attention",
    )(q_arr, k_arr, v_arr, bias)


def _band_decode_body(q_ref, kn_ref, vn_ref, kc_ref, vc_ref, bc_ref, bn_ref, o_ref, *, heads):
    d = q_ref.shape[1] // heads
    scale = d ** -0.5
    for h in range(heads):
        cols = slice(h * d, (h + 1) * d)
        q = _bf(q_ref[:, cols])
        sc = lax.dot_general(q, _bf(kc_ref[:, h, :]), _NT, preferred_element_type=F32) * scale + bc_ref[h]
        sn = lax.dot_general(q, _bf(kn_ref[:, cols]), _NT, preferred_element_type=F32) * scale + bn_ref[h]
        m = jnp.maximum(jnp.max(sc, axis=-1, keepdims=True), jnp.max(sn, axis=-1, keepdims=True))
        pc, pn = jnp.exp(sc - m), jnp.exp(sn - m)
        o = (jnp.dot(pc.astype(BF16), _bf(vc_ref[:, h, :]), preferred_element_type=F32)
             + jnp.dot(pn.astype(BF16), _bf(vn_ref[:, cols]), preferred_element_type=F32))
        o = o / (jnp.sum(pc, axis=-1, keepdims=True) + jnp.sum(pn, axis=-1, keepdims=True))
        o_ref[:, cols] = o.astype(o_ref.dtype)


def _band_decode(projb, n, l, heads, hd, k_cache, v_cache, layer, bias_c, bias_n):
    w = heads * hd
    r = k_cache.shape[1]
    new = lambda blk: pl.BlockSpec((l, w), lambda b: (b, blk))
    cache = pl.BlockSpec((None, r, heads, hd), lambda b: (layer * n + b, 0, 0, 0))
    return pl.pallas_call(
        functools.partial(_band_decode_body, heads=heads), grid=(n,),
        in_specs=[new(0), new(1), new(2), cache, cache,
                  pl.BlockSpec((heads, l, r), lambda b: (0, 0, 0)),
                  pl.BlockSpec((heads, l, l), lambda b: (0, 0, 0))],
        out_specs=pl.BlockSpec((l, w), lambda b: (b, 0)),
        out_shape=jax.ShapeDtypeStruct((n * l, w), BF16),
        compiler_params=_params("parallel"), name="band_decode",
    )(projb, projb, projb, k_cache, v_cache, bias_c, bias_n)


def _kv_rows_body(*refs, nl, heads, hd):
    k_refs, v_refs = refs[0:nl], refs[nl:2 * nl]
    ko_ref, vo_ref = refs[2 * nl], refs[2 * nl + 1]
    layer = pl.program_id(0)
    for li in range(nl):
        @pl.when(layer == li)
        def _(li=li):
            for h in range(heads):
                ko_ref[:, h, :] = k_refs[li][:, h * hd:(h + 1) * hd]
                vo_ref[:, h, :] = v_refs[li][:, h * hd:(h + 1) * hd]


def _kv_rows(projs, kblk, vblk, heads, hd):
    nl = len(projs)
    m = projs[0].shape[0]
    w = heads * hd
    tm = _pick(m, (256, 128, 64, 32, 16, 8))
    nt = m // tm

    def in_spec(li, blk):
        return pl.BlockSpec((tm, w), lambda layer, i: (jnp.clip(i + (layer - li) * nt, 0, nt - 1), blk))

    out_spec = pl.BlockSpec((None, tm, heads, hd), lambda layer, i: (layer, i, 0, 0))
    out_sds = jax.ShapeDtypeStruct((nl, m, heads, hd), F32)
    return pl.pallas_call(
        functools.partial(_kv_rows_body, nl=nl, heads=heads, hd=hd), grid=(nl, nt),
        in_specs=[in_spec(li, kblk) for li in range(nl)] + [in_spec(li, vblk) for li in range(nl)],
        out_specs=[out_spec, out_spec], out_shape=[out_sds, out_sds],
        compiler_params=_params("arbitrary", "arbitrary"), name="kv_rows",
    )(*projs, *projs)


def _trunk(x, n, l, mods, st, p, dims):
    m, d = x.shape
    depth = p['w_mod'].shape[0]
    alpha = (2 * depth) ** 0.25
    expanded = l % 256 != 0
    tm_big = m if expanded else _pick(l, (1024, 512, 256))
    tm_row = m if expanded else 256
    tn_wide = 512
    d_ff = p['ffn_w_gate'].shape[2]
    g_, p_, hg = dims['s5']
    mix_a = g_ * hg
    sb_heads, sb_hd = dims['sb']
    gla_heads, gla_dk, gla_dv = dims['gla']
    ca_heads, ca_hd = dims['ca']
    sb_w = sb_heads * sb_hd
    ca_w = ca_heads * ca_hd
    gla_w = gla_heads * gla_dv

    def mod_of(layer):
        if expanded:
            arr = jnp.repeat(mods[layer], l, axis=0)[None]
        else:
            arr = mods[layer][:, None, :]
        return _Mod(arr, d, l, expanded)

    outs = {k: [] for k in ('s5_re', 's5_im', 'gla', 'band_k', 'band_v')}
    projs_ab = []
    mod = mod_of(0)
    h = _modulate(x, mod, tm_row)
    for layer in range(depth):
        i = layer // 2
        if layer % 2 == 0:
            in_ab = p['ab_w_in'].shape[2]
            proj, = _matmul(h, [(p['ab_w_in'], i, 0)], n=in_ab, tm=tm_big, tn=tn_wide, tk=d, out_dtypes=[F32],
                            epilogue=_epi_identity, name="ab_in")
            gb = g_ // S5_LANE_GROUPS
            sw = S5_LANE_GROUPS * p_
            x0 = jnp.concatenate([st['s5_re'][i].reshape(n, gb, sw), st['s5_im'][i].reshape(n, gb, sw)], axis=2)
            together = l <= 64 and n % SUBLANES == 0
            if together:
                t_rows = SUBLANES * l
                x0 = jnp.swapaxes(x0, 0, 1)
            else:
                t_rows = _pick(l, (512, 256, 128, 64, 32, 16, 8))
                x0 = x0.reshape(n * gb, 1, 2 * sw)
            tables = _s5_tables(*p['s5_raw'][i], seg=t_rows // SUBLANES)
            z, xo = _s5(proj, n, l, tables, p['s5_d'], i, x0, t_rows, independent=together)
            xo = jnp.swapaxes(xo, 0, 1) if together else xo.reshape(n, gb, 2 * sw)
            outs['s5_re'].append(xo[:, :, :sw].reshape(n, g_, p_))
            outs['s5_im'].append(xo[:, :, sw:].reshape(n, g_, p_))
            tn_glu = tn_wide
            ya, = _matmul(z, [(p['s5_w_glu'], i, 0)], n=mix_a, tm=tm_big, tn=tn_glu, tk=mix_a, out_dtypes=[BF16],
                          epilogue=_epi_glu, name="s5_glu",
                          extras=[(z, (tm_big, tn_glu), lambda a, b: (a, b)),
                                  (p['s5_b_glu'], (None, 1, tn_glu), lambda a, b, i=i: (i, 0, b))])
            qc, kc, vc = mix_a // sb_hd, (mix_a + sb_w) // sb_hd, (mix_a + 2 * sb_w) // sb_hd
            prev = None if st['sb_k'] is None else (st['sb_k'], st['sb_v'], i)
            yb = _stick_breaking(proj, n, l, sb_heads, sb_hd, qc, kc, vc, prev)
            projs_ab.append(proj)
            merged = [ya, yb]
            w_out = p['ab_w_out']
        else:
            n_gla = 2 * gla_heads * gla_dk + 2 * gla_w
            proj, = _matmul(h, [(p['cd_w_gla'][i], None, 0)], n=n_gla, tm=tm_big, tn=tn_wide, tk=d, out_dtypes=[F32],
                            epilogue=_epi_identity, name="cd_in_gla")
            projb, = _matmul(h, [(p['cd_w_band'][i], None, 0)], n=3 * ca_w, tm=tm_big, tn=tn_wide, tk=d,
                             out_dtypes=[F32], epilogue=_epi_identity, name="cd_in_band")
            lr, = _matmul(h, [(p['cd_w_lrin'][i], None, 0)], n=LANES, tm=tm_big, tn=LANES, tk=d, out_dtypes=[F32],
                          epilogue=_epi_identity, name="cd_lr")
            s0t = jnp.swapaxes(st['gla'][i], -1, -2).reshape(n * gla_heads, gla_dv, gla_dk)
            yc, so = _gla(proj, lr, n, l, gla_heads, gla_dk, gla_dv, p['gla_w_lr_pad'], p['gla_b_lr'],
                          p['gla_norm_w'], i, s0t)
            outs['gla'].append(jnp.swapaxes(so.reshape(n, gla_heads, gla_dv, gla_dk), -1, -2))
            ka = projb[:, ca_w:2 * ca_w]
            va = projb[:, 2 * ca_w:3 * ca_w]
            table = p['ca_rel_bias'][i]
            if st['band_k'] is None:
                tq = _pick(l, (256, 128, 64))
                window = tq + BAND_PAST
                qchunk = np.arange(tq)[:, None] // CHUNK
                kchunk = np.arange(window)[None, :] // CHUNK
                static_mask = (kchunk >= qchunk) & (kchunk <= qchunk + BAND_PREV)
                bias = _toeplitz_bias(table, tq, window, BAND_PAST, static_mask)
                yd = _band(projb, 0, projb, projb, ca_heads, 2 * ca_heads, n, l, ca_heads, ca_hd, bias, tq, window,
                           BAND_PAST, m)
                keep = min(BAND_PAST, l)
                outs['band_k'].append(ka.reshape(n, l, ca_heads, ca_hd)[:, l - keep:])
                outs['band_v'].append(va.reshape(n, l, ca_heads, ca_hd)[:, l - keep:])
            else:
                r = st['band_k'].shape[1]
                bias = _toeplitz_bias(table, l, r + l, r)
                yd = _band_decode(projb, n, l, ca_heads, ca_hd, st['band_k'], st['band_v'], i,
                                  bias[:, :, :r], bias[:, :, r:])
                outs['band_k'].append(ka.reshape(n, l, ca_heads, ca_hd))
                outs['band_v'].append(va.reshape(n, l, ca_heads, ca_hd))
            merged = [yc, yd]
            w_out = p['cd_w_out']
        f, = _matmul(merged, [(w_out, i, 0)], n=d, tm=tm_big, tn=tn_wide, tk=sum(y.shape[1] for y in merged),
                     out_dtypes=[F32], epilogue=_epi_identity, name="mix_out")
        x, h = _deepnorm(x, f, mod, 2, p['ln1_w'], p['ln1_b'], layer, alpha, tm_row, nxt=(mod, 4, 3))
        tn_ff = _pick(d_ff, (256, 128))
        tall = not expanded and l % 2048 == 0
        act, = _matmul(h, [(p['ffn_w_gate'], layer, 0), (p['ffn_w_up'], layer, 0)], n=d_ff,
                       tm=2048 if tall else tm_big, tn=tn_ff, tk=d, out_dtypes=[BF16], epilogue=_epi_swiglu,
                       a_buffers=1 if tall else None, name="ffn_in")
        ff, = _matmul(act, [(p['ffn_w_down'], layer, 0)], n=d, tm=min(tm_big, 512), tn=tn_wide, tk=d_ff,
                      out_dtypes=[F32], epilogue=_epi_identity, name="ffn_out")
        if layer + 1 < depth:
            nmod = mod_of(layer + 1)
            x, h = _deepnorm(x, ff, mod, 5, p['ln2_w'], p['ln2_b'], layer, alpha, tm_row, nxt=(nmod, 1, 0))
            mod = nmod
        else:
            x, _ = _deepnorm(x, ff, mod, 5, p['ln2_w'], p['ln2_b'], layer, alpha, tm_row)
    outs = {k: jnp.stack(v) for k, v in outs.items()}
    assert mix_a % sb_w == 0
    sb_k, sb_v = _kv_rows(projs_ab, mix_a // sb_w + 1, mix_a // sb_w + 2, sb_heads, sb_hd)
    outs['sb_k'] = sb_k.reshape(len(projs_ab), n, l, sb_heads, sb_hd)
    outs['sb_v'] = sb_v.reshape(len(projs_ab), n, l, sb_heads, sb_hd)
    return x, outs


def kernel(x_prompt, x_sample, state_s5_re, state_s5_im, cache_sb_k, cache_sb_v, state_gla, cache_band_k, cache_band_v, c_prompt, c_sample, w_mod, b_mod, ln1_w, ln1_b, ln2_w, ln2_b, ab_w_in, ab_w_out, s5_a_re, s5_a_im, s5_log_dt, s5_b_re, s5_b_im, s5_c_re, s5_c_im, s5_d, s5_w_glu, s5_b_glu, cd_w_in, cd_w_out, gla_w_lr, gla_b_lr, gla_norm_w, ca_rel_bias, ffn_w_gate, ffn_w_up, ffn_w_down):
    nb, seq, d = x_prompt.shape
    db, dseq, _ = x_sample.shape
    depth = w_mod.shape[0]
    n_even, g_, p_ = s5_a_re.shape
    hg = s5_b_re.shape[-1]
    _, _, past, sb_heads, sb_hd = cache_sb_k.shape
    n_odd, _, gla_heads, gla_dk, gla_dv = state_gla.shape
    _, _, band_rows, ca_heads, ca_hd = cache_band_k.shape
    rank = gla_w_lr.shape[1]
    dims = {'s5': (g_, p_, hg), 'sb': (sb_heads, sb_hd), 'gla': (gla_heads, gla_dk, gla_dv), 'ca': (ca_heads, ca_hd)}

    lr0 = 2 * gla_heads * gla_dk + 2 * gla_heads * gla_dv
    p = dict(
        w_mod=w_mod, ln1_w=ln1_w[:, None, :], ln1_b=ln1_b[:, None, :], ln2_w=ln2_w[:, None, :], ln2_b=ln2_b[:, None, :],
        ab_w_in=ab_w_in, ab_w_out=ab_w_out, s5_d=s5_d[:, None, :], s5_w_glu=s5_w_glu, s5_b_glu=s5_b_glu[:, None, :],
        cd_w_out=cd_w_out, gla_b_lr=gla_b_lr[:, None, :], gla_norm_w=gla_norm_w[:, None, :], ca_rel_bias=ca_rel_bias,
        ffn_w_gate=ffn_w_gate, ffn_w_up=ffn_w_up, ffn_w_down=ffn_w_down.astype(BF16),
        cd_w_gla=[cd_w_in[i, :, :lr0] for i in range(n_odd)],
        cd_w_band=[cd_w_in[i, :, lr0 + rank:] for i in range(n_odd)],
        cd_w_lrin=[jnp.pad(cd_w_in[i, :, lr0:lr0 + rank], ((0, 0), (0, LANES - rank))) for i in range(n_odd)],
        gla_w_lr_pad=jnp.pad(gla_w_lr, ((0, 0), (0, LANES - rank), (0, 0))),
        s5_raw=[(s5_a_re[i], s5_a_im[i], s5_log_dt[i], s5_b_re[i], s5_b_im[i], s5_c_re[i], s5_c_im[i])
                for i in range(n_even)],
    )

    rows = nb + db
    rows_pad = -(-rows // 16) * 16
    c_all = jnp.concatenate([c_prompt, c_sample, jnp.zeros((rows_pad - rows, d), F32)], axis=0)
    mods = []
    for layer in range(depth):
        tn_mod = _pick(6 * d, (1024, 512, 256, 128))
        mod_l, = _matmul(c_all, [(w_mod, layer, 0)], n=6 * d, tm=rows_pad, tn=tn_mod, tk=d, out_dtypes=[F32],
                         epilogue=_epi_bias, prologue=_silu, name="adaln_mod",
                         extras=[(b_mod[:, None, :], (None, 1, tn_mod), lambda a, b, layer=layer: (layer, 0, b))])
        mods.append(mod_l)
    mods = jnp.stack(mods)
    mods_p, mods_s = mods[:, :nb], mods[:, nb:rows]

    zeros_s5 = jnp.zeros((n_even, nb, g_, p_), F32)
    st_p = dict(s5_re=zeros_s5, s5_im=zeros_s5, sb_k=None, sb_v=None,
                gla=jnp.zeros((n_odd, nb, gla_heads, gla_dk, gla_dv), F32), band_k=None, band_v=None)
    y_p, o_p = _trunk(x_prompt.reshape(nb * seq, d), nb, seq, mods_p, st_p, p, dims)
    st_s = dict(s5_re=state_s5_re, s5_im=state_s5_im,
                sb_k=cache_sb_k.reshape(n_even * db, past, sb_heads, sb_hd),
                sb_v=cache_sb_v.reshape(n_even * db, past, sb_heads, sb_hd),
                gla=state_gla,
                band_k=cache_band_k.reshape(n_odd * db, band_rows, ca_heads, ca_hd),
                band_v=cache_band_v.reshape(n_odd * db, band_rows, ca_heads, ca_hd))
    y_s, o_s = _trunk(x_sample.reshape(db * dseq, d), db, dseq, mods_s, st_s, p, dims)
    return (y_p.reshape(nb, seq, d), y_s.reshape(db, dseq, d),
            o_p['s5_re'], o_p['s5_im'], o_p['sb_k'], o_p['sb_v'], o_p['gla'], o_p['band_k'], o_p['band_v'],
            o_s['s5_re'], o_s['s5_im'], o_s['sb_k'], o_s['sb_v'], o_s['gla'], o_s['band_k'], o_s['band_v'])
```

```python
import functools
import math

import numpy as np
import jax
import jax.numpy as jnp
from jax import lax
from jax.experimental import pallas as pl
from jax.experimental.pallas import tpu as pltpu

F32 = jnp.float32
BF16 = jnp.bfloat16

CHUNK = 64
BAND_PREV = 8
BAND_PAST = BAND_PREV * CHUNK
GLA_GATE_NORM = 16.0
LN_EPS = 1e-5
RMS_EPS = 1e-6

LANES = 128
SUBLANES = 8
V7X_VMEM_BYTES = 64 * 1024 * 1024
VMEM_LIMIT = V7X_VMEM_BYTES - 8 * 1024 * 1024

GLA_SUB = 16
S5_LANE_GROUPS = 8
NEG = -1e30
SB_EXIT = -104.0

_NT = (((1,), (1,)), ((), ()))
_TN = (((0,), (0,)), ((), ()))


def _pick(dim, candidates):
    for c in candidates:
        if c <= dim and dim % c == 0:
            return c
    return dim


def _params(*sem):
    return pltpu.CompilerParams(dimension_semantics=sem, vmem_limit_bytes=VMEM_LIMIT)


def _bf(x):
    return x if x.dtype == BF16 else x.astype(BF16)


def _log_sigmoid(z):
    return jnp.minimum(z, 0.0) - jnp.log(1.0 + jnp.exp(-jnp.abs(z)))


def _silu(x):
    return x * jax.nn.sigmoid(x)


def _mm_body(*refs, na, nb, ne, no, nk, prologue, epilogue):
    a_refs, refs = refs[:na], refs[na:]
    b_refs = refs[0:nb]
    e_refs = refs[nb:nb + ne]
    o_refs = refs[nb + ne:nb + ne + no]
    acc_refs = refs[nb + ne + no:]
    a_parts = [a_ref[...] for a_ref in a_refs]
    if prologue is not None:
        a_parts = [prologue(a) for a in a_parts]
    a_parts = [_bf(a) for a in a_parts]
    a = a_parts[0] if na == 1 else jnp.concatenate(a_parts, axis=1)
    parts = [jnp.dot(a, _bf(b_ref[...]), preferred_element_type=F32) for b_ref in b_refs]

    def finish(accs):
        outs = epilogue(accs, [e[...] for e in e_refs])
        for o_ref, o in zip(o_refs, outs):
            o_ref[...] = o.astype(o_ref.dtype)

    if nk == 1:
        finish(parts)
        return
    k = pl.program_id(2)

    @pl.when(k == 0)
    def _():
        for acc, p in zip(acc_refs, parts):
            acc[...] = p

    @pl.when(k > 0)
    def _():
        for acc, p in zip(acc_refs, parts):
            acc[...] += p

    @pl.when(k == nk - 1)
    def _():
        finish([acc[...] for acc in acc_refs])


def _matmul(a, bs, *, n, tm, tn, tk, out_dtypes, epilogue, extras=(), prologue=None, a_buffers=None,
            name="matmul"):
    a_list = list(a) if isinstance(a, (list, tuple)) else [a]
    m = a_list[0].shape[0]
    kdim = sum(x.shape[1] for x in a_list)
    nk = kdim // tk
    assert len(a_list) == 1 or nk == 1
    grid = (m // tm, n // tn, nk)
    if len(a_list) == 1:
        mode = {} if a_buffers is None else {"pipeline_mode": pl.Buffered(a_buffers)}
        in_specs = [pl.BlockSpec((tm, tk), lambda i, j, k: (i, k), **mode)]
    else:
        in_specs = [pl.BlockSpec((tm, x.shape[1]), lambda i, j, k: (i, 0)) for x in a_list]
    operands = list(a_list)
    for arr, layer, c0 in bs:
        if arr.ndim == 3:
            in_specs.append(pl.BlockSpec((None, tk, tn), lambda i, j, k, layer=layer, c0=c0: (layer, k, c0 + j)))
        else:
            in_specs.append(pl.BlockSpec((tk, tn), lambda i, j, k, c0=c0: (k, c0 + j)))
        operands.append(arr)
    for arr, shape, imap in extras:
        in_specs.append(pl.BlockSpec(shape, lambda i, j, k, imap=imap: imap(i, j)))
        operands.append(arr)
    out_shape = [jax.ShapeDtypeStruct((m, n), dt) for dt in out_dtypes]
    out_specs = [pl.BlockSpec((tm, tn), lambda i, j, k: (i, j)) for _ in out_dtypes]
    scratch = [pltpu.VMEM((tm, tn), F32) for _ in bs] if nk > 1 else []
    body = functools.partial(_mm_body, na=len(a_list), nb=len(bs), ne=len(extras), no=len(out_dtypes), nk=nk,
                             prologue=prologue, epilogue=epilogue)
    outs = pl.pallas_call(
        body, grid=grid, in_specs=in_specs, out_specs=out_specs, out_shape=out_shape,
        scratch_shapes=scratch, compiler_params=_params("parallel", "parallel", "arbitrary"), name=name,
    )(*operands)
    return outs


def _epi_identity(accs, extras):
    return [accs[0]]


def _epi_bias(accs, extras):
    return [accs[0] + extras[0]]


def _epi_swiglu(accs, extras):
    return [_silu(accs[0]) * accs[1]]


def _epi_glu(accs, extras):
    z, bias = extras
    return [z * jax.nn.sigmoid(accs[0] + bias)]


class _Mod:
    def __init__(self, arr, d, l, expanded):
        self.arr, self.d, self.l, self.expanded = arr, d, l, expanded

    def spec(self, piece, tm):
        d = self.d
        if self.expanded:
            return pl.BlockSpec((None, tm, d), lambda i, *_: (0, i, piece))
        per = self.l // tm
        return pl.BlockSpec((None, 1, d), lambda i, *_: (i // per, 0, piece))


def _modulate_body(x_ref, sc_ref, sh_ref, h_ref):
    h_ref[...] = (x_ref[...] * (1.0 + sc_ref[...]) + sh_ref[...]).astype(h_ref.dtype)


def _modulate(x, mod, tm):
    m, d = x.shape
    row = pl.BlockSpec((tm, d), lambda i: (i, 0))
    return pl.pallas_call(
        _modulate_body, grid=(m // tm,), in_specs=[row, mod.spec(1, tm), mod.spec(0, tm)],
        out_specs=row, out_shape=jax.ShapeDtypeStruct((m, d), BF16),
        compiler_params=_params("parallel"), name="modulate",
    )(x, mod.arr, mod.arr)


def _ln_body(*refs, alpha, has_next):
    if has_next:
        x_ref, f_ref, g_ref, w_ref, b_ref, sc_ref, sh_ref, xo_ref, ho_ref = refs
    else:
        x_ref, f_ref, g_ref, w_ref, b_ref, xo_ref = refs
    y = alpha * x_ref[...] + (1.0 + g_ref[...]) * f_ref[...]
    yc = y - jnp.mean(y, axis=-1, keepdims=True)
    var = jnp.mean(yc * yc, axis=-1, keepdims=True)
    xn = yc * lax.rsqrt(var + LN_EPS) * w_ref[...] + b_ref[...]
    xo_ref[...] = xn
    if has_next:
        ho_ref[...] = (xn * (1.0 + sc_ref[...]) + sh_ref[...]).astype(ho_ref.dtype)


def _deepnorm(x, f, mod, gate_piece, w, b, layer, alpha, tm, nxt=None):
    m, d = x.shape
    row = pl.BlockSpec((tm, d), lambda i: (i, 0))
    vec = pl.BlockSpec((None, 1, d), lambda i: (layer, 0, 0))
    in_specs = [row, row, mod.spec(gate_piece, tm), vec, vec]
    operands = [x, f, mod.arr, w, b]
    out_shape = [jax.ShapeDtypeStruct((m, d), F32)]
    out_specs = [row]
    if nxt is not None:
        nmod, sc_piece, sh_piece = nxt
        in_specs += [nmod.spec(sc_piece, tm), nmod.spec(sh_piece, tm)]
        operands += [nmod.arr, nmod.arr]
        out_shape.append(jax.ShapeDtypeStruct((m, d), BF16))
        out_specs.append(row)
    outs = pl.pallas_call(
        functools.partial(_ln_body, alpha=alpha, has_next=nxt is not None),
        grid=(m // tm,), in_specs=in_specs, out_specs=out_specs, out_shape=out_shape,
        compiler_params=_params("parallel"), name="deepnorm",
    )(*operands)
    return outs if nxt is not None else (outs[0], None)


def _s5_body(u_ref, bt_ref, ct_ref, sc_ref, d_ref, x0_ref, z_ref, xo_ref, xs_ref, st_ref, *, t_rows, independent):
    t = pl.program_id(2)
    sw = st_ref.shape[-1] // 2

    if not independent:
        @pl.when(t == 0)
        def _():
            st_ref[...] = x0_ref[...]

    seg = t_rows // SUBLANES
    u = jnp.concatenate([u_ref[pl.ds(j, SUBLANES, stride=seg), :] for j in range(seg)], axis=0)
    xs_ref[...] = jnp.dot(_bf(u), bt_ref[...], preferred_element_type=F32)
    ar, ai = sc_ref[0:8, :], sc_ref[8:16, :]
    unroll = 2 if seg % 2 == 0 else 1

    def cmul_add(xr, xi, ar, ai, sr, si):
        return xr + ar * sr - ai * si, xi + ar * si + ai * sr

    def local(j, carry):
        off = pl.multiple_of(j * SUBLANES, SUBLANES)
        xr, xi = cmul_add(xs_ref[pl.ds(off, SUBLANES), 0:sw], xs_ref[pl.ds(off, SUBLANES), sw:2 * sw],
                          ar, ai, carry[0], carry[1])
        xs_ref[pl.ds(off, SUBLANES), 0:sw] = xr
        xs_ref[pl.ds(off, SUBLANES), sw:2 * sw] = xi
        return xr, xi

    zero = jnp.zeros((SUBLANES, sw), F32)
    gr, gi = lax.fori_loop(0, seg, local, (zero, zero), unroll=unroll)

    if independent:
        cr, ci = x0_ref[:, 0:sw], x0_ref[:, sw:2 * sw]
    else:
        for step, d in enumerate((1, 2, 4)):
            r0 = 16 + 16 * step
            gr, gi = cmul_add(gr, gi, sc_ref[r0:r0 + 8, :], sc_ref[r0 + 8:r0 + 16, :],
                              pltpu.roll(gr, d, 0), pltpu.roll(gi, d, 0))
        st = st_ref[...]
        sr, si = st[:, 0:sw], st[:, sw:2 * sw]
        gr, gi = cmul_add(gr, gi, sc_ref[64:72, :], sc_ref[72:80, :], sr, si)
        fr, fi = gr[SUBLANES - 1:SUBLANES, :], gi[SUBLANES - 1:SUBLANES, :]
        st_ref[:, 0:sw] = fr
        st_ref[:, sw:2 * sw] = fi
        row = lax.broadcasted_iota(jnp.int32, (SUBLANES, sw), 0)
        cr = jnp.where(row == 0, sr, pltpu.roll(gr, 1, 0))
        ci = jnp.where(row == 0, si, pltpu.roll(gi, 1, 0))

    def fix(j, carry):
        off = pl.multiple_of(j * SUBLANES, SUBLANES)
        dr = ar * carry[0] - ai * carry[1]
        di = ar * carry[1] + ai * carry[0]
        xs_ref[pl.ds(off, SUBLANES), 0:sw] += dr
        xs_ref[pl.ds(off, SUBLANES), sw:2 * sw] += di
        return dr, di

    dr, di = lax.fori_loop(0, seg, fix, (cr, ci), unroll=unroll)
    if independent:
        fr, fi = gr + dr, gi + di

    y = jnp.dot(_bf(xs_ref[...]), ct_ref[...], preferred_element_type=F32) + d_ref[...] * u
    z = 0.5 * y * (1.0 + jnp.tanh(math.sqrt(2.0 / math.pi) * (y + 0.044715 * (y * y * y))))
    for j in range(seg):
        z_ref[pl.ds(j, SUBLANES, stride=seg), :] = z[j * SUBLANES:(j + 1) * SUBLANES, :]

    @pl.when(t == pl.num_programs(2) - 1)
    def _():
        xo_ref[:, 0:sw] = fr
        xo_ref[:, sw:2 * sw] = fi


def _s5_tables(a_re, a_im, log_dt, b_re, b_im, c_re, c_im, seg):
    g, p = a_re.shape
    hg = b_re.shape[-1]
    gb = g // S5_LANE_GROUPS
    lr_, li_ = a_re.astype(F32), a_im.astype(F32)
    dt = jnp.exp(log_dt.astype(F32))[:, None]
    mag = jnp.exp(lr_ * dt)
    ab_r, ab_i = mag * jnp.cos(li_ * dt), mag * jnp.sin(li_ * dt)
    den = lr_ * lr_ + li_ * li_
    cr_ = ((ab_r - 1.0) * lr_ + ab_i * li_) / den
    ci_ = (ab_i * lr_ - (ab_r - 1.0) * li_) / den
    br_, bi_ = b_re.astype(F32), b_im.astype(F32)
    bb_r = cr_[..., None] * br_ - ci_[..., None] * bi_
    bb_i = cr_[..., None] * bi_ + ci_[..., None] * br_
    eye = jnp.eye(S5_LANE_GROUPS, dtype=F32)

    def blockdiag_in(w):
        w = w.reshape(gb, S5_LANE_GROUPS, p, hg)
        return jnp.einsum('bgph,gk->bghkp', w, eye).reshape(gb, S5_LANE_GROUPS * hg, S5_LANE_GROUPS * p)

    def blockdiag_out(w):
        w = w.reshape(gb, S5_LANE_GROUPS, hg, p)
        return jnp.einsum('bgkp,gj->bgpjk', w, eye).reshape(gb, S5_LANE_GROUPS * p, S5_LANE_GROUPS * hg)

    bt = jnp.concatenate([blockdiag_in(bb_r), blockdiag_in(bb_i)], axis=2).astype(BF16)
    ct = jnp.concatenate([blockdiag_out(c_re.astype(F32)), blockdiag_out(-c_im.astype(F32))], axis=1).astype(BF16)
    sw = S5_LANE_GROUPS * p

    def rows8(z):
        return jnp.broadcast_to(z.reshape(gb, 1, sw), (gb, SUBLANES, sw))

    def cmul(x, y):
        return x[0] * y[0] - x[1] * y[1], x[0] * y[1] + x[1] * y[0]

    a_seg, sq, e = None, (ab_r, ab_i), seg
    while e:
        if e & 1:
            a_seg = sq if a_seg is None else cmul(a_seg, sq)
        sq, e = cmul(sq, sq), e >> 1
    pows = [a_seg]
    for _ in range(SUBLANES - 1):
        pows.append(cmul(pows[-1], a_seg))
    pieces = [rows8(ab_r), rows8(ab_i)]
    trow = jnp.arange(SUBLANES)[None, :, None]
    for d in (1, 2, 4):
        pieces += [jnp.where(trow >= d, rows8(pows[d - 1][part]), 0.0) for part in (0, 1)]
    for part in (0, 1):
        pieces.append(jnp.stack([pw[part] for pw in pows], axis=0).reshape(SUBLANES, gb, sw).transpose(1, 0, 2))
    sc = jnp.concatenate(pieces, axis=1)
    return bt, ct, sc


def _s5(proj, n, l, tables, d_skip, layer, x0, t_rows, independent=False):
    bt, ct, sc = tables
    gb, cw, sw2 = bt.shape
    if independent:
        assert t_rows == SUBLANES * l and n % SUBLANES == 0
        nt, grid = 1, (n // SUBLANES, gb, 1)
        st_spec = pl.BlockSpec((None, SUBLANES, sw2), lambda b, g, t: (g, b, 0))
        st_shape = jax.ShapeDtypeStruct((gb, n, sw2), F32)
    else:
        nt = l // t_rows
        grid = (n, gb, nt)
        st_spec = pl.BlockSpec((None, 1, sw2), lambda b, g, t: (b * gb + g, 0, 0))
        st_shape = jax.ShapeDtypeStruct((n * gb, 1, sw2), F32)
    z, xo = pl.pallas_call(
        functools.partial(_s5_body, t_rows=t_rows, independent=independent), grid=grid,
        in_specs=[
            pl.BlockSpec((t_rows, cw), lambda b, g, t: (b * nt + t, g)),
            pl.BlockSpec((None, cw, sw2), lambda b, g, t: (g, 0, 0)),
            pl.BlockSpec((None, sw2, cw), lambda b, g, t: (g, 0, 0)),
            pl.BlockSpec((None, 10 * SUBLANES, sw2 // 2), lambda b, g, t: (g, 0, 0)),
            pl.BlockSpec((None, 1, cw), lambda b, g, t: (layer, 0, g)),
            st_spec,
        ],
        out_specs=[pl.BlockSpec((t_rows, cw), lambda b, g, t: (b * nt + t, g)), st_spec],
        out_shape=[jax.ShapeDtypeStruct((n * l, gb * cw), F32), st_shape],
        scratch_shapes=[pltpu.VMEM((t_rows, sw2), F32), pltpu.VMEM((1, sw2), F32)],
        compiler_params=_params("parallel", "parallel", "arbitrary"), name="s5_scan",
    )(proj, bt, ct, sc, d_skip, x0)
    return z, xo


def _strict_upper2(tk):
    rj = lax.broadcasted_iota(jnp.int32, (tk, tk), 0)
    cs = lax.broadcasted_iota(jnp.int32, (tk, tk), 1)
    upper = jnp.where(rj > cs, 1.0, 0.0).astype(BF16)
    return jnp.concatenate([upper, upper], axis=0)


def _sb_tile(q, kt, vt, carry, mask, scale, upper2):
    z = lax.dot_general(q, _bf(kt), _NT, preferred_element_type=F32) * scale
    ls = _log_sigmoid(z)
    lf = ls - z
    if mask is not None:
        lf = jnp.where(mask, lf, 0.0)
    hi = lf.astype(BF16)
    lo = (lf - hi.astype(F32)).astype(BF16)
    newer = jnp.dot(jnp.concatenate([hi, lo], axis=1), upper2, preferred_element_type=F32)
    w = jnp.exp(ls + carry + newer)
    if mask is not None:
        w = jnp.where(mask, w, 0.0)
    out = jnp.dot(w.astype(BF16), _bf(vt), preferred_element_type=F32)
    return out, carry + jnp.sum(lf, axis=1, keepdims=True)


def _sb_body(q_ref, kd_ref, vd_ref, kp_ref, vp_ref, o_ref, *, tk, prev_tiles, hb):
    tq = q_ref.shape[0]
    d = q_ref.shape[1] // hb
    scale = d ** -0.5
    r = lax.broadcasted_iota(jnp.int32, (tq, tq), 0)
    c = lax.broadcasted_iota(jnp.int32, (tq, tq), 1)
    upper2 = _strict_upper2(tk)
    upper2_diag = upper2 if tq == tk else _strict_upper2(tq)
    nprev = prev_tiles(pl.program_id(2))
    first_off = pl.multiple_of(jnp.maximum(nprev - 1, 0) * tk, tk)

    def earlier(ref, off, h):
        if len(ref.shape) == 3:
            return ref[pl.ds(off, tk), h, :]
        return ref[pl.ds(off, tk), h * d:(h + 1) * d]

    states = []
    for h in range(hb):
        cols = slice(h * d, (h + 1) * d)
        q = _bf(q_ref[:, cols])
        acc, carry = _sb_tile(q, kd_ref[:, cols], vd_ref[:, cols], jnp.zeros((tq, 1), F32), c < r, scale,
                              upper2_diag)
        out, carry = _sb_tile(q, earlier(kp_ref, first_off, h), earlier(vp_ref, first_off, h), carry,
                              nprev > 0, scale, upper2)
        states.append((q, acc + out, carry))

    def cond(state):
        j, _, _, cmax = state
        return jnp.logical_and(j >= 0, cmax > SB_EXIT)

    for h, (q, acc, carry) in enumerate(states):
        cols = slice(h * d, (h + 1) * d)

        def body(state, q=q, h=h):
            j, acc, carry, _ = state
            off = pl.multiple_of(j * tk, tk)
            out, carry = _sb_tile(q, earlier(kp_ref, off, h), earlier(vp_ref, off, h), carry, None, scale, upper2)
            return j - 1, acc + out, carry, jnp.max(carry)

        first = jnp.asarray(nprev - 2, jnp.int32)
        _, acc, _, _ = lax.while_loop(cond, body, (first, acc, carry, jnp.max(carry)))
        o_ref[:, cols] = acc.astype(o_ref.dtype)


def _stick_breaking(proj, n, l, heads, hd, qcol, kcol, vcol, prev=None):
    if prev is None:
        hb = 4
        tq = _pick(l, (256, 128, 64, 32, 16, 8))
        tk = tq
        nq = l // tq
        kp, vp = proj, proj
        kp_spec = pl.BlockSpec((l, hb * hd), lambda b, h, i: (b, kcol // hb + h))
        vp_spec = pl.BlockSpec((l, hb * hd), lambda b, h, i: (b, vcol // hb + h))
        prev_tiles = lambda i: i
    else:
        hb = heads
        kc, vc, layer = prev
        past = kc.shape[1]
        tq, nq = l, 1
        tk = _pick(past, (256, 128, 64, 32, 16, 8))
        kp, vp = kc, vc
        kp_spec = pl.BlockSpec((None, past, heads, hd), lambda b, h, i: (layer * n + b, 0, 0, 0))
        vp_spec = pl.BlockSpec((None, past, heads, hd), lambda b, h, i: (layer * n + b, 0, 0, 0))
        prev_tiles = lambda i: past // tk
    assert heads % hb == 0 and qcol % hb == 0 and kcol % hb == 0 and vcol % hb == 0
    w = hb * hd
    return pl.pallas_call(
        functools.partial(_sb_body, tk=tk, prev_tiles=prev_tiles, hb=hb), grid=(n, heads // hb, nq),
        in_specs=[
            pl.BlockSpec((tq, w), lambda b, h, i: (b * nq + i, qcol // hb + h)),
            pl.BlockSpec((tq, w), lambda b, h, i: (b * nq + i, kcol // hb + h)),
            pl.BlockSpec((tq, w), lambda b, h, i: (b * nq + i, vcol // hb + h)),
            kp_spec, vp_spec,
        ],
        out_specs=pl.BlockSpec((tq, w), lambda b, h, i: (b * nq + i, h)),
        out_shape=jax.ShapeDtypeStruct((n * l, heads * hd), BF16),
        compiler_params=_params("parallel", "parallel", "arbitrary"), name="stick_breaking",
    )(proj, proj, proj, kp, vp)


def _gla_body(q_ref, k_ref, v_ref, g_ref, lr_ref, wlr_ref, blr_ref, nw_ref, s0_ref, y_ref, so_ref, st_ref, *, heads):
    t = pl.program_id(1)
    c = q_ref.shape[0]

    @pl.when(t == 0)
    def _():
        st_ref[...] = s0_ref[...]

    gate_in = jnp.dot(_bf(lr_ref[...]), _bf(wlr_ref[...]), preferred_element_type=F32) + blr_ref[...]
    logg = _log_sigmoid(gate_in) * (1.0 / GLA_GATE_NORM)
    rt = lax.broadcasted_iota(jnp.int32, (c, c), 0)
    cs = lax.broadcasted_iota(jnp.int32, (c, c), 1)
    lower = jnp.where(cs <= rt, 1.0, 0.0).astype(BF16)
    hi = logg.astype(BF16)
    lo = (logg - hi.astype(F32)).astype(BF16)
    b_all = jnp.dot(lower, hi, preferred_element_type=F32) + jnp.dot(lower, lo, preferred_element_type=F32)
    last = t == pl.num_programs(1) - 1
    for hh in range(heads):
        _gla_head(hh, b_all, q_ref, k_ref, v_ref, g_ref, nw_ref, y_ref, so_ref, st_ref, last, heads)


def _gla_head(hh, b_all, q_ref, k_ref, v_ref, g_ref, nw_ref, y_ref, so_ref, st_ref, last, heads):
    c = q_ref.shape[0]
    dk = q_ref.shape[1] // heads
    dv = v_ref.shape[1] // heads
    q = q_ref[:, hh * dk:(hh + 1) * dk] * (dk ** -0.5)
    k = k_ref[:, hh * dk:(hh + 1) * dk]
    vb = _bf(v_ref[:, hh * dv:(hh + 1) * dv])
    b = b_all[:, hh * dk:(hh + 1) * dk]
    st = st_ref[hh]
    o_inter = lax.dot_general(_bf(q * jnp.exp(b)), _bf(st), _NT, preferred_element_type=F32)
    row = lax.broadcasted_iota(jnp.int32, (GLA_SUB, 1), 0)
    lane = lax.broadcasted_iota(jnp.int32, (GLA_SUB, GLA_SUB), 1)
    outs = []
    for blk in range(c // GLA_SUB):
        r0, r1 = blk * GLA_SUB, (blk + 1) * GLA_SUB
        bi, qi, ki = b[r0:r1], q[r0:r1], k[r0:r1]
        att = jnp.zeros((GLA_SUB, GLA_SUB), F32)
        for s in range(GLA_SUB):
            dec = jnp.exp(jnp.where(row >= s, bi - bi[s:s + 1, :], NEG))
            col = jnp.sum(qi * dec * ki[s:s + 1, :], axis=1, keepdims=True)
            att = jnp.where(lane == s, col, att)
        oi = jnp.dot(_bf(att), vb[r0:r1], preferred_element_type=F32)
        if blk > 0:
            bref = b[r0 - 1:r0, :]
            qt = qi * jnp.exp(bi - bref)
            kt = k[0:r0] * jnp.exp(bref - b[0:r0])
            a_off = lax.dot_general(_bf(qt), _bf(kt), _NT, preferred_element_type=F32)
            oi = oi + jnp.dot(_bf(a_off), vb[0:r0], preferred_element_type=F32)
        outs.append(oi)
    o = (outs[0] if len(outs) == 1 else jnp.concatenate(outs, axis=0)) + o_inter
    o = o * lax.rsqrt(jnp.mean(o * o, axis=-1, keepdims=True) + RMS_EPS) * nw_ref[...]
    y_ref[:, hh * dv:(hh + 1) * dv] = (o * _silu(g_ref[:, hh * dv:(hh + 1) * dv])).astype(y_ref.dtype)

    bl = b[c - 1:c, :]
    kh = k * jnp.exp(bl - b)
    new = st * jnp.exp(bl) + lax.dot_general(vb, _bf(kh), _TN, preferred_element_type=F32)
    st_ref[hh] = new

    @pl.when(last)
    def _():
        so_ref[hh] = new


def _gla(proj, lr, n, l, heads, dk, dv, w_lr, b_lr, norm_w, layer, s0t):
    c = min(CHUNK, l)
    nc = l // c
    wk, wv = heads * dk, heads * dv
    assert (2 * wk) % wv == 0
    v_blk = (2 * wk) // wv
    rank_pad = lr.shape[1]
    y, so = pl.pallas_call(
        functools.partial(_gla_body, heads=heads), grid=(n, nc),
        in_specs=[
            pl.BlockSpec((c, wk), lambda b, t: (b * nc + t, 0)),
            pl.BlockSpec((c, wk), lambda b, t: (b * nc + t, 1)),
            pl.BlockSpec((c, wv), lambda b, t: (b * nc + t, v_blk)),
            pl.BlockSpec((c, wv), lambda b, t: (b * nc + t, v_blk + 1)),
            pl.BlockSpec((c, rank_pad), lambda b, t: (b * nc + t, 0)),
            pl.BlockSpec((None, rank_pad, wk), lambda b, t: (layer, 0, 0)),
            pl.BlockSpec((None, 1, wk), lambda b, t: (layer, 0, 0)),
            pl.BlockSpec((None, 1, dv), lambda b, t: (layer, 0, 0)),
            pl.BlockSpec((heads, dv, dk), lambda b, t: (b, 0, 0)),
        ],
        out_specs=[
            pl.BlockSpec((c, wv), lambda b, t: (b * nc + t, 0)),
            pl.BlockSpec((heads, dv, dk), lambda b, t: (b, 0, 0)),
        ],
        out_shape=[jax.ShapeDtypeStruct((n * l, wv), BF16),
                   jax.ShapeDtypeStruct((n * heads, dv, dk), F32)],
        scratch_shapes=[pltpu.VMEM((heads, dv, dk), F32)],
        compiler_params=_params("parallel", "arbitrary"), name="gla",
    )(proj, proj, proj, proj, lr, w_lr, b_lr, norm_w, s0t)
    return y, so


def _band_body(q_ref, k_ref, v_ref, bias_ref, o_ref, *, window, past, hb):
    tq = q_ref.shape[0]
    d = q_ref.shape[1] // hb
    qs = pl.program_id(2) * tq
    starts = [pl.multiple_of(jnp.maximum(qs - past + blk * tq, 0), tq) for blk in range(window // tq)]
    exists = lax.broadcasted_iota(jnp.int32, (tq, window), 1) >= past - qs
    for h in range(hb):
        cols = slice(h * d, (h + 1) * d)
        q = _bf(q_ref[:, cols])
        kw = jnp.concatenate([_bf(k_ref[pl.ds(st, tq), cols]) for st in starts], axis=0)
        vw = jnp.concatenate([_bf(v_ref[pl.ds(st, tq), cols]) for st in starts], axis=0)
        s = lax.dot_general(q, kw, _NT, preferred_element_type=F32) * (d ** -0.5) + bias_ref[h]
        s = jnp.where(exists, s, NEG)
        p = jnp.exp(s - jnp.max(s, axis=-1, keepdims=True))
        o = jnp.dot(p.astype(BF16), vw, preferred_element_type=F32) / jnp.sum(p, axis=-1, keepdims=True)
        o_ref[:, cols] = o.astype(o_ref.dtype)


def _toeplitz_bias(table, tq, window, offset, static_mask=None):
    rows, heads = table.shape
    clip = (rows - 1) // 2
    period = tq + window
    m = np.arange(period)
    m = np.where(m >= window, m - period, m)
    idx = np.clip(offset - m, -clip, clip) + clip
    vec = table.astype(F32)[idx, :].T
    flat = jnp.tile(vec, (1, tq))[:, :tq * (period - 1)]
    bias = flat.reshape(heads, tq, period - 1)[:, :, :window]
    if static_mask is not None:
        bias = jnp.where(static_mask[None], bias, NEG)
    return bias


def _band(q_arr, qcol, k_arr, v_arr, kcol, vcol, n, l, heads, hd, bias, tq, window, past, out_rows):
    nq = l // tq
    lk = k_arr.shape[0] // n
    hb = 4
    assert heads % hb == 0 and qcol % hb == 0 and kcol % hb == 0 and vcol % hb == 0
    w = hb * hd
    return pl.pallas_call(
        functools.partial(_band_body, window=window, past=past, hb=hb), grid=(n, heads // hb, nq),
        in_specs=[
            pl.BlockSpec((tq, w), lambda b, h, i: (b * nq + i, qcol // hb + h)),
            pl.BlockSpec((lk, w), lambda b, h, i: (b, kcol // hb + h)),
            pl.BlockSpec((lk, w), lambda b, h, i: (b, vcol // hb + h)),
            pl.BlockSpec((hb, tq, window), lambda b, h, i: (h, 0, 0)),
        ],
        out_specs=pl.BlockSpec((tq, w), lambda b, h, i: (b * nq + i, h)),
        out_shape=jax.ShapeDtypeStruct((out_rows, heads * hd), BF16),
        compiler_params=_params("parallel", "parallel", "arbitrary"), name="band_attention",
    )(q_arr, k_arr, v_arr, bias)


def _band_decode_body(q_ref, kn_ref, vn_ref, kc_ref, vc_ref, bc_ref, bn_ref, o_ref, *, heads):
    d = q_ref.shape[1] // heads
    scale = d ** -0.5
    for h in range(heads):
        cols = slice(h * d, (h + 1) * d)
        q = _bf(q_ref[:, cols])
        sc = lax.dot_general(q, _bf(kc_ref[:, h, :]), _NT, preferred_element_type=F32) * scale + bc_ref[h]
        sn = lax.dot_general(q, _bf(kn_ref[:, cols]), _NT, preferred_element_type=F32) * scale + bn_ref[h]
        m = jnp.maximum(jnp.max(sc, axis=-1, keepdims=True), jnp.max(sn, axis=-1, keepdims=True))
        pc, pn = jnp.exp(sc - m), jnp.exp(sn - m)
        o = (jnp.dot(pc.astype(BF16), _bf(vc_ref[:, h, :]), preferred_element_type=F32)
             + jnp.dot(pn.astype(BF16), _bf(vn_ref[:, cols]), preferred_element_type=F32))
        o = o / (jnp.sum(pc, axis=-1, keepdims=True) + jnp.sum(pn, axis=-1, keepdims=True))
        o_ref[:, cols] = o.astype(o_ref.dtype)


def _band_decode(projb, n, l, heads, hd, k_cache, v_cache, layer, bias_c, bias_n):
    w = heads * hd
    r = k_cache.shape[1]
    new = lambda blk: pl.BlockSpec((l, w), lambda b: (b, blk))
    cache = pl.BlockSpec((None, r, heads, hd), lambda b: (layer * n + b, 0, 0, 0))
    return pl.pallas_call(
        functools.partial(_band_decode_body, heads=heads), grid=(n,),
        in_specs=[new(0), new(1), new(2), cache, cache,
                  pl.BlockSpec((heads, l, r), lambda b: (0, 0, 0)),
                  pl.BlockSpec((heads, l, l), lambda b: (0, 0, 0))],
        out_specs=pl.BlockSpec((l, w), lambda b: (b, 0)),
        out_shape=jax.ShapeDtypeStruct((n * l, w), BF16),
        compiler_params=_params("parallel"), name="band_decode",
    )(projb, projb, projb, k_cache, v_cache, bias_c, bias_n)


def _kv_rows_body(*refs, nl, heads, hd):
    k_refs, v_refs = refs[0:nl], refs[nl:2 * nl]
    ko_ref, vo_ref = refs[2 * nl], refs[2 * nl + 1]
    layer = pl.program_id(0)
    for li in range(nl):
        @pl.when(layer == li)
        def _(li=li):
            for h in range(heads):
                ko_ref[:, h, :] = k_refs[li][:, h * hd:(h + 1) * hd]
                vo_ref[:, h, :] = v_refs[li][:, h * hd:(h + 1) * hd]


def _kv_rows(projs, kblk, vblk, heads, hd):
    nl = len(projs)
    m = projs[0].shape[0]
    w = heads * hd
    tm = _pick(m, (256, 128, 64, 32, 16, 8))
    nt = m // tm

    def in_spec(li, blk):
        return pl.BlockSpec((tm, w), lambda layer, i: (jnp.clip(i + (layer - li) * nt, 0, nt - 1), blk))

    out_spec = pl.BlockSpec((None, tm, heads, hd), lambda layer, i: (layer, i, 0, 0))
    out_sds = jax.ShapeDtypeStruct((nl, m, heads, hd), F32)
    return pl.pallas_call(
        functools.partial(_kv_rows_body, nl=nl, heads=heads, hd=hd), grid=(nl, nt),
        in_specs=[in_spec(li, kblk) for li in range(nl)] + [in_spec(li, vblk) for li in range(nl)],
        out_specs=[out_spec, out_spec], out_shape=[out_sds, out_sds],
        compiler_params=_params("arbitrary", "arbitrary"), name="kv_rows",
    )(*projs, *projs)


def _trunk(x, n, l, mods, st, p, dims):
    m, d = x.shape
    depth = p['w_mod'].shape[0]
    alpha = (2 * depth) ** 0.25
    expanded = l % 256 != 0
    tm_big = m if expanded else _pick(l, (1024, 512, 256))
    tm_row = m if expanded else 256
    tn_wide = 512
    d_ff = p['ffn_w_gate'].shape[2]
    g_, p_, hg = dims['s5']
    mix_a = g_ * hg
    sb_heads, sb_hd = dims['sb']
    gla_heads, gla_dk, gla_dv = dims['gla']
    ca_heads, ca_hd = dims['ca']
    sb_w = sb_heads * sb_hd
    ca_w = ca_heads * ca_hd
    gla_w = gla_heads * gla_dv

    def mod_of(layer):
        if expanded:
            arr = jnp.repeat(mods[layer], l, axis=0)[None]
        else:
            arr = mods[layer][:, None, :]
        return _Mod(arr, d, l, expanded)

    outs = {k: [] for k in ('s5_re', 's5_im', 'gla', 'band_k', 'band_v')}
    projs_ab = []
    mod = mod_of(0)
    h = _modulate(x, mod, tm_row)
    for layer in range(depth):
        i = layer // 2
        if layer % 2 == 0:
            in_ab = p['ab_w_in'].shape[2]
            proj, = _matmul(h, [(p['ab_w_in'], i, 0)], n=in_ab, tm=tm_big, tn=tn_wide, tk=d, out_dtypes=[F32],
                            epilogue=_epi_identity, name="ab_in")
            gb = g_ // S5_LANE_GROUPS
            sw = S5_LANE_GROUPS * p_
            x0 = jnp.concatenate([st['s5_re'][i].reshape(n, gb, sw), st['s5_im'][i].reshape(n, gb, sw)], axis=2)
            together = l <= 64 and n % SUBLANES == 0
            if together:
                t_rows = SUBLANES * l
                x0 = jnp.swapaxes(x0, 0, 1)
            else:
                t_rows = _pick(l, (512, 256, 128, 64, 32, 16, 8))
                x0 = x0.reshape(n * gb, 1, 2 * sw)
            tables = _s5_tables(*p['s5_raw'][i], seg=t_rows // SUBLANES)
            z, xo = _s5(proj, n, l, tables, p['s5_d'], i, x0, t_rows, independent=together)
            xo = jnp.swapaxes(xo, 0, 1) if together else xo.reshape(n, gb, 2 * sw)
            outs['s5_re'].append(xo[:, :, :sw].reshape(n, g_, p_))
            outs['s5_im'].append(xo[:, :, sw:].reshape(n, g_, p_))
            tn_glu = tn_wide
            ya, = _matmul(z, [(p['s5_w_glu'], i, 0)], n=mix_a, tm=tm_big, tn=tn_glu, tk=mix_a, out_dtypes=[BF16],
                          epilogue=_epi_glu, name="s5_glu",
                          extras=[(z, (tm_big, tn_glu), lambda a, b: (a, b)),
                                  (p['s5_b_glu'], (None, 1, tn_glu), lambda a, b, i=i: (i, 0, b))])
            qc, kc, vc = mix_a // sb_hd, (mix_a + sb_w) // sb_hd, (mix_a + 2 * sb_w) // sb_hd
            prev = None if st['sb_k'] is None else (st['sb_k'], st['sb_v'], i)
            yb = _stick_breaking(proj, n, l, sb_heads, sb_hd, qc, kc, vc, prev)
            projs_ab.append(proj)
            merged = [ya, yb]
            w_out = p['ab_w_out']
        else:
            n_gla = 2 * gla_heads * gla_dk + 2 * gla_w
            proj, = _matmul(h, [(p['cd_w_gla'][i], None, 0)], n=n_gla, tm=tm_big, tn=tn_wide, tk=d, out_dtypes=[F32],
                            epilogue=_epi_identity, name="cd_in_gla")
            projb, = _matmul(h, [(p['cd_w_band'][i], None, 0)], n=3 * ca_w, tm=tm_big, tn=tn_wide, tk=d,
                             out_dtypes=[F32], epilogue=_epi_identity, name="cd_in_band")
            lr, = _matmul(h, [(p['cd_w_lrin'][i], None, 0)], n=LANES, tm=tm_big, tn=LANES, tk=d, out_dtypes=[F32],
                          epilogue=_epi_identity, name="cd_lr")
            s0t = jnp.swapaxes(st['gla'][i], -1, -2).reshape(n * gla_heads, gla_dv, gla_dk)
            yc, so = _gla(proj, lr, n, l, gla_heads, gla_dk, gla_dv, p['gla_w_lr_pad'], p['gla_b_lr'],
                          p['gla_norm_w'], i, s0t)
            outs['gla'].append(jnp.swapaxes(so.reshape(n, gla_heads, gla_dv, gla_dk), -1, -2))
            ka = projb[:, ca_w:2 * ca_w]
            va = projb[:, 2 * ca_w:3 * ca_w]
            table = p['ca_rel_bias'][i]
            if st['band_k'] is None:
                tq = _pick(l, (256, 128, 64))
                window = tq + BAND_PAST
                qchunk = np.arange(tq)[:, None] // CHUNK
                kchunk = np.arange(window)[None, :] // CHUNK
                static_mask = (kchunk >= qchunk) & (kchunk <= qchunk + BAND_PREV)
                bias = _toeplitz_bias(table, tq, window, BAND_PAST, static_mask)
                yd = _band(projb, 0, projb, projb, ca_heads, 2 * ca_heads, n, l, ca_heads, ca_hd, bias, tq, window,
                           BAND_PAST, m)
                keep = min(BAND_PAST, l)
                outs['band_k'].append(ka.reshape(n, l, ca_heads, ca_hd)[:, l - keep:])
                outs['band_v'].append(va.reshape(n, l, ca_heads, ca_hd)[:, l - keep:])
            else:
                r = st['band_k'].shape[1]
                bias = _toeplitz_bias(table, l, r + l, r)
                yd = _band_decode(projb, n, l, ca_heads, ca_hd, st['band_k'], st['band_v'], i,
                                  bias[:, :, :r], bias[:, :, r:])
                outs['band_k'].append(ka.reshape(n, l, ca_heads, ca_hd))
                outs['band_v'].append(va.reshape(n, l, ca_heads, ca_hd))
            merged = [yc, yd]
            w_out = p['cd_w_out']
        f, = _matmul(merged, [(w_out, i, 0)], n=d, tm=tm_big, tn=tn_wide, tk=sum(y.shape[1] for y in merged),
                     out_dtypes=[F32], epilogue=_epi_identity, name="mix_out")
        x, h = _deepnorm(x, f, mod, 2, p['ln1_w'], p['ln1_b'], layer, alpha, tm_row, nxt=(mod, 4, 3))
        tn_ff = _pick(d_ff, (256, 128))
        tall = not expanded and l % 2048 == 0
        act, = _matmul(h, [(p['ffn_w_gate'], layer, 0), (p['ffn_w_up'], layer, 0)], n=d_ff,
                       tm=2048 if tall else tm_big, tn=tn_ff, tk=d, out_dtypes=[BF16], epilogue=_epi_swiglu,
                       a_buffers=1 if tall else None, name="ffn_in")
        ff, = _matmul(act, [(p['ffn_w_down'], layer, 0)], n=d, tm=min(tm_big, 512), tn=tn_wide, tk=d_ff,
                      out_dtypes=[F32], epilogue=_epi_identity, name="ffn_out")
        if layer + 1 < depth:
            nmod = mod_of(layer + 1)
            x, h = _deepnorm(x, ff, mod, 5, p['ln2_w'], p['ln2_b'], layer, alpha, tm_row, nxt=(nmod, 1, 0))
            mod = nmod
        else:
            x, _ = _deepnorm(x, ff, mod, 5, p['ln2_w'], p['ln2_b'], layer, alpha, tm_row)
    outs = {k: jnp.stack(v) for k, v in outs.items()}
    assert mix_a % sb_w == 0
    sb_k, sb_v = _kv_rows(projs_ab, mix_a // sb_w + 1, mix_a // sb_w + 2, sb_heads, sb_hd)
    outs['sb_k'] = sb_k.reshape(len(projs_ab), n, l, sb_heads, sb_hd)
    outs['sb_v'] = sb_v.reshape(len(projs_ab), n, l, sb_heads, sb_hd)
    return x, outs


def kernel(x_prompt, x_sample, state_s5_re, state_s5_im, cache_sb_k, cache_sb_v, state_gla, cache_band_k, cache_band_v, c_prompt, c_sample, w_mod, b_mod, ln1_w, ln1_b, ln2_w, ln2_b, ab_w_in, ab_w_out, s5_a_re, s5_a_im, s5_log_dt, s5_b_re, s5_b_im, s5_c_re, s5_c_im, s5_d, s5_w_glu, s5_b_glu, cd_w_in, cd_w_out, gla_w_lr, gla_b_lr, gla_norm_w, ca_rel_bias, ffn_w_gate, ffn_w_up, ffn_w_down):
    nb, seq, d = x_prompt.shape
    db, dseq, _ = x_sample.shape
    depth = w_mod.shape[0]
    n_even, g_, p_ = s5_a_re.shape
    hg = s5_b_re.shape[-1]
    _, _, past, sb_heads, sb_hd = cache_sb_k.shape
    n_odd, _, gla_heads, gla_dk, gla_dv = state_gla.shape
    _, _, band_rows, ca_heads, ca_hd = cache_band_k.shape
    rank = gla_w_lr.shape[1]
    dims = {'s5': (g_, p_, hg), 'sb': (sb_heads, sb_hd), 'gla': (gla_heads, gla_dk, gla_dv), 'ca': (ca_heads, ca_hd)}

    lr0 = 2 * gla_heads * gla_dk + 2 * gla_heads * gla_dv
    p = dict(
        w_mod=w_mod, ln1_w=ln1_w[:, None, :], ln1_b=ln1_b[:, None, :], ln2_w=ln2_w[:, None, :], ln2_b=ln2_b[:, None, :],
        ab_w_in=ab_w_in, ab_w_out=ab_w_out, s5_d=s5_d[:, None, :], s5_w_glu=s5_w_glu, s5_b_glu=s5_b_glu[:, None, :],
        cd_w_out=cd_w_out, gla_b_lr=gla_b_lr[:, None, :], gla_norm_w=gla_norm_w[:, None, :], ca_rel_bias=ca_rel_bias,
        ffn_w_gate=ffn_w_gate, ffn_w_up=ffn_w_up, ffn_w_down=ffn_w_down.astype(BF16),
        cd_w_gla=[cd_w_in[i, :, :lr0] for i in range(n_odd)],
        cd_w_band=[cd_w_in[i, :, lr0 + rank:] for i in range(n_odd)],
        cd_w_lrin=[jnp.pad(cd_w_in[i, :, lr0:lr0 + rank], ((0, 0), (0, LANES - rank))) for i in range(n_odd)],
        gla_w_lr_pad=jnp.pad(gla_w_lr, ((0, 0), (0, LANES - rank), (0, 0))),
        s5_raw=[(s5_a_re[i], s5_a_im[i], s5_log_dt[i], s5_b_re[i], s5_b_im[i], s5_c_re[i], s5_c_im[i])
                for i in range(n_even)],
    )

    rows = nb + db
    rows_pad = -(-rows // 16) * 16
    c_all = jnp.concatenate([c_prompt, c_sample, jnp.zeros((rows_pad - rows, d), F32)], axis=0)
    mods = []
    for layer in range(depth):
        tn_mod = _pick(6 * d, (1024, 512, 256, 128))
        mod_l, = _matmul(c_all, [(w_mod, layer, 0)], n=6 * d, tm=rows_pad, tn=tn_mod, tk=d, out_dtypes=[F32],
                         epilogue=_epi_bias, prologue=_silu, name="adaln_mod",
                         extras=[(b_mod[:, None, :], (None, 1, tn_mod), lambda a, b, layer=layer: (layer, 0, b))])
        mods.append(mod_l)
    mods = jnp.stack(mods)
    mods_p, mods_s = mods[:, :nb], mods[:, nb:rows]

    zeros_s5 = jnp.zeros((n_even, nb, g_, p_), F32)
    st_p = dict(s5_re=zeros_s5, s5_im=zeros_s5, sb_k=None, sb_v=None,
                gla=jnp.zeros((n_odd, nb, gla_heads, gla_dk, gla_dv), F32), band_k=None, band_v=None)
    y_p, o_p = _trunk(x_prompt.reshape(nb * seq, d), nb, seq, mods_p, st_p, p, dims)
    st_s = dict(s5_re=state_s5_re, s5_im=state_s5_im,
                sb_k=cache_sb_k.reshape(n_even * db, past, sb_heads, sb_hd),
                sb_v=cache_sb_v.reshape(n_even * db, past, sb_heads, sb_hd),
                gla=state_gla,
                band_k=cache_band_k.reshape(n_odd * db, band_rows, ca_heads, ca_hd),
                band_v=cache_band_v.reshape(n_odd * db, band_rows, ca_heads, ca_hd))
    y_s, o_s = _trunk(x_sample.reshape(db * dseq, d), db, dseq, mods_s, st_s, p, dims)
    return (y_p.reshape(nb, seq, d), y_s.reshape(db, dseq, d),
            o_p['s5_re'], o_p['s5_im'], o_p['sb_k'], o_p['sb_v'], o_p['gla'], o_p['band_k'], o_p['band_v'],
            o_s['s5_re'], o_s['s5_im'], o_s['sb_k'], o_s['sb_v'], o_s['gla'], o_s['band_k'], o_s['band_v'])
```
